```python
import math
import jax, jax.numpy as jnp
from jax import lax
import numpy as np

D_MODEL = 1024
BATCH = 16
SEQ = 2048
DEPTH = 2

N_A = DEPTH // 2
N_B = DEPTH - N_A
N_EVEN = (DEPTH + 1) // 2
N_ODD = DEPTH // 2

SSM_GROUP = 16
N_GROUPS = D_MODEL // SSM_GROUP
SSM_STATE = 64
SSM_CHUNK = 128
DT_MIN = 1e-3
DT_MAX = 1e-1

N_HEADS = 8
HEAD_DIM = D_MODEL // N_HEADS
MOBA_BLOCK = 256
MOBA_TOPK = 3
QUERY_CHUNK = 16
ROPE_THETA = 500000.0
ROT_DIM = HEAD_DIM // 4

D_FF_DENSE = 2816
N_EXPERTS = 8
MOE_TOPK = 2
D_FF_EXPERT = 3584
MOE_ROWS = 256

NORM_EPS = 1e-6
NEG_INF = -1e30

kernel_name = "yoco_s5_moba_moe_hybrid"


def rmsnorm(x, g):
    xf = x.astype(jnp.float32)
    y = xf * lax.rsqrt(jnp.mean(xf * xf, axis=-1, keepdims=True) + NORM_EPS)
    return (y * g.astype(jnp.float32)).astype(x.dtype)


def partial_rotary(x, positions):
    half = ROT_DIM // 2
    inv_freq = 1.0 / (ROPE_THETA ** (jnp.arange(half, dtype=jnp.float32) / half))
    ang = positions[:, None, :, None].astype(jnp.float32) * inv_freq
    cos, sin = jnp.cos(ang), jnp.sin(ang)
    xr = x[..., :ROT_DIM].astype(jnp.float32)
    x1, x2 = xr[..., :half], xr[..., half:]
    rot = jnp.concatenate([x1 * cos - x2 * sin, x2 * cos + x1 * sin], axis=-1)
    return jnp.concatenate([rot.astype(x.dtype), x[..., ROT_DIM:]], axis=-1)


def _complex_combine(e1, e2):
    a1r, a1i, b1r, b1i = e1
    a2r, a2i, b2r, b2i = e2
    return (a2r * a1r - a2i * a1i,
            a2r * a1i + a2i * a1r,
            a2r * b1r - a2i * b1i + b2r,
            a2r * b1i + a2i * b1r + b2i)


def s5_mixer(h, lam_re, lam_im, log_dt, b_re, b_im, c_re, c_im, d_skip, w_glu):
    f32 = jnp.float32
    Bsz, S, D = h.shape
    G, P = lam_re.shape
    lr, li = lam_re.astype(f32), lam_im.astype(f32)
    dt = jnp.exp(log_dt.astype(f32))[:, None]
    zr, zi = lr * dt, li * dt
    mag = jnp.exp(zr)
    a_r, a_i = mag * jnp.cos(zr * 0 + zi), mag * jnp.sin(zi)
    den = lr * lr + li * li
    coef_r = ((a_r - 1.0) * lr + a_i * li) / den
    coef_i = (a_i * lr - (a_r - 1.0) * li) / den
    br, bi = b_re.astype(f32), b_im.astype(f32)
    bb_r = coef_r[:, :, None] * br - coef_i[:, :, None] * bi
    bb_i = coef_r[:, :, None] * bi + coef_i[:, :, None] * br
    cr, ci = c_re.astype(f32), c_im.astype(f32)

    L = SSM_CHUNK
    nc = S // L
    steps = jnp.arange(1, L + 1, dtype=f32)[:, None, None]
    pmag = jnp.exp(steps * zr)
    pw_r, pw_i = pmag * jnp.cos(steps * zi), pmag * jnp.sin(steps * zi)
    ar_full = jnp.broadcast_to(a_r, (Bsz, L, G, P))
    ai_full = jnp.broadcast_to(a_i, (Bsz, L, G, P))

    u = h.astype(f32).reshape(Bsz, nc, L, G, SSM_GROUP).transpose(1, 0, 2, 3, 4)

    def chunk_step(carry, u_c):
        sr0, si0 = carry
        bu_r = jnp.einsum('blgi,gpi->blgp', u_c, bb_r)
        bu_i = jnp.einsum('blgi,gpi->blgp', u_c, bb_i)
        _, _, sr, si = lax.associative_scan(_complex_combine, (ar_full, ai_full, bu_r, bu_i), axis=1)
        sr = sr + pw_r * sr0[:, None] - pw_i * si0[:, None]
        si = si + pw_r * si0[:, None] + pw_i * sr0[:, None]
        y = jnp.einsum('blgp,gip->blgi', sr, cr) - jnp.einsum('blgp,gip->blgi', si, ci)
        return (sr[:, -1], si[:, -1]), y

    init = (jnp.zeros((Bsz, G, P), f32), jnp.zeros((Bsz, G, P), f32))
    _, ys = lax.scan(chunk_step, init, u)
    y = ys.transpose(1, 0, 2, 3, 4).reshape(Bsz, S, D) + d_skip.astype(f32) * h.astype(f32)
    g = jax.nn.gelu(y).astype(h.dtype)
    z = jnp.dot(g, w_glu)
    val, gate = z[..., :D], z[..., D:]
    return (val.astype(f32) * jax.nn.sigmoid(gate.astype(f32))).astype(h.dtype)


def shared_kv(s, g_kv, w_kv, positions):
    Bsz, S, D = s.shape
    hk = rmsnorm(s, g_kv)
    kv = jnp.dot(hk, w_kv)
    k = kv[..., :N_HEADS * HEAD_DIM].reshape(Bsz, S, N_HEADS, HEAD_DIM).transpose(0, 2, 1, 3)
    v = kv[..., N_HEADS * HEAD_DIM:].reshape(Bsz, S, N_HEADS, HEAD_DIM).transpose(0, 2, 1, 3)
    k = partial_rotary(k, positions)
    nb = -(-S // MOBA_BLOCK)
    pad = nb * MOBA_BLOCK - S
    k = jnp.pad(k, ((0, 0), (0, 0), (0, pad), (0, 0)))
    v = jnp.pad(v, ((0, 0), (0, 0), (0, pad), (0, 0)))
    kb = k.reshape(Bsz, N_HEADS, nb, MOBA_BLOCK, HEAD_DIM)
    vb = v.reshape(Bsz, N_HEADS, nb, MOBA_BLOCK, HEAD_DIM)
    kmean = jnp.mean(kb.astype(jnp.float32), axis=3)
    return kb, vb, kmean


def moba_attention(q, kb, vb, kmean):
    Bsz, H, S, Dh = q.shape
    nb = kb.shape[2]
    scale = Dh ** -0.5
    qblk = jnp.arange(S) // MOBA_BLOCK
    gate = jnp.einsum('bhsd,bhnd->bhsn', q.astype(jnp.float32), kmean)
    past = jnp.arange(nb)[None, :] < qblk[:, None]
    gate = jnp.where(past, gate, NEG_INF)
    k_sel = min(MOBA_TOPK, nb)
    _, idx = lax.top_k(gate, k_sel)
    valid = jnp.arange(k_sel)[None, :] < qblk[:, None]

    nq = S // QUERY_CHUNK
    q_c = q.reshape(Bsz, H, nq, QUERY_CHUNK, Dh).transpose(2, 0, 1, 3, 4)
    idx_c = idx.reshape(Bsz, H, nq, QUERY_CHUNK, k_sel).transpose(2, 0, 1, 3, 4)
    valid_c = valid.reshape(nq, QUERY_CHUNK, k_sel)
    b_ix = jnp.arange(Bsz)[:, None, None, None]
    h_ix = jnp.arange(H)[None, :, None, None]

    def one_chunk(args):
        qc, ic, vc, c = args
        t0 = c * QUERY_CHUNK
        qpos = t0 + jnp.arange(QUERY_CHUNK)
        j = t0 // MOBA_BLOCK
        ok = lax.dynamic_index_in_dim(kb, j, axis=2, keepdims=False)
        ov = lax.dynamic_index_in_dim(vb, j, axis=2, keepdims=False)
        gk = kb[b_ix, h_ix, ic]
        gv = vb[b_ix, h_ix, ic]
        s_sel = jnp.einsum('bhtd,bhtkid->bhtki', qc, gk).astype(jnp.float32) * scale
        s_sel = jnp.where(vc[None, None, :, :, None], s_sel, NEG_INF)
        s_sel = s_sel.reshape(Bsz, H, QUERY_CHUNK, k_sel * MOBA_BLOCK)
        s_own = jnp.einsum('bhtd,bhid->bhti', qc, ok).astype(jnp.float32) * scale
        kpos = j * MOBA_BLOCK + jnp.arange(MOBA_BLOCK)
        s_own = jnp.where(kpos[None, :] <= qpos[:, None], s_own, NEG_INF)
        p = jax.nn.softmax(jnp.concatenate([s_sel, s_own], axis=-1), axis=-1)
        p_sel = p[..., :k_sel * MOBA_BLOCK].reshape(Bsz, H, QUERY_CHUNK, k_sel, MOBA_BLOCK).astype(vb.dtype)
        p_own = p[..., k_sel * MOBA_BLOCK:].astype(vb.dtype)
        return (jnp.einsum('bhtki,bhtkid->bhtd', p_sel, gv)
                + jnp.einsum('bhti,bhid->bhtd', p_own, ov))

    outs = lax.map(one_chunk, (q_c, idx_c, valid_c, jnp.arange(nq)))
    return outs.transpose(1, 2, 0, 3, 4).reshape(Bsz, H, S, Dh)


def moba_mixer(h, w_q, w_o, kb, vb, kmean, positions):
    Bsz, S, D = h.shape
    q = jnp.dot(h, w_q).reshape(Bsz, S, N_HEADS, HEAD_DIM).transpose(0, 2, 1, 3)
    q = partial_rotary(q, positions)
    o = moba_attention(q, kb, vb, kmean)
    o = o.transpose(0, 2, 1, 3).reshape(Bsz, S, N_HEADS * HEAD_DIM)
    return jnp.dot(o, w_o)


def dense_swiglu(h, w1, w3, w2):
    return jnp.dot(jax.nn.silu(jnp.dot(h, w1)) * jnp.dot(h, w3), w2)


def moe_swiglu(h, w_router, b_router, w1, w3, w2):
    Bsz, S, D = h.shape
    N = Bsz * S
    E, R = N_EXPERTS, MOE_ROWS
    t = h.reshape(N, D)
    logits = jnp.dot(t, w_router).astype(jnp.float32) + b_router.astype(jnp.float32)
    top_v, top_e = lax.top_k(logits, MOE_TOPK)
    top_w = jax.nn.softmax(top_v, axis=-1)
    A = N * MOE_TOPK
    flat_e = top_e.reshape(A)
    flat_w = top_w.reshape(A)
    flat_tok = jnp.arange(A, dtype=jnp.int32) // MOE_TOPK
    order = jnp.argsort(flat_e)
    se = flat_e[order]
    counts = jnp.zeros((E,), jnp.int32).at[flat_e].add(1)
    starts = jnp.cumsum(counts) - counts
    pcounts = (counts + R - 1) // R * R
    pends = jnp.cumsum(pcounts)
    pstarts = pends - pcounts
    dest = pstarts[se] + jnp.arange(A, dtype=jnp.int32) - starts[se]
    total = A + E * R
    nblk = total // R
    row_tok = jnp.full((total,), N, jnp.int32).at[dest].set(flat_tok[order])
    row_w = jnp.zeros((total,), jnp.float32).at[dest].set(flat_w[order])
    blk_e = jnp.minimum(jnp.searchsorted(pends, jnp.arange(nblk, dtype=jnp.int32) * R, side='right'), E - 1)
    t_pad = jnp.concatenate([t, jnp.zeros((1, D), t.dtype)], axis=0)
    xin = t_pad[row_tok].reshape(nblk, R, D)

    def expert_block(args):
        xb, e = args
        return jnp.dot(jax.nn.silu(jnp.dot(xb, w1[e])) * jnp.dot(xb, w3[e]), w2[e])

    yb = lax.map(expert_block, (xin, blk_e)).reshape(total, D)
    out = jnp.zeros((N + 1, D), t.dtype).at[row_tok].add(
        (yb.astype(jnp.float32) * row_w[:, None]).astype(t.dtype))
    return out[:N].reshape(Bsz, S, D)


def setup_inputs(seed: int = 0) -> dict:
    key = jax.random.key(seed)
    ks = jax.random.split(key, 24)
    f32 = jnp.float32
    D, G, P, Q = D_MODEL, N_GROUPS, SSM_STATE, SSM_GROUP
    HD = N_HEADS * HEAD_DIM
    nrm = lambda k, shape, s: jax.random.normal(k, shape, f32) * s
    x = jax.random.normal(ks[0], (BATCH, SEQ, D), f32)
    offset = jax.random.randint(ks[1], (BATCH, 1), 0, 4096, dtype=jnp.int32)
    positions = offset + jnp.arange(SEQ, dtype=jnp.int32)[None, :]
    norms = 1.0 + nrm(ks[2], (DEPTH, 4, D), 0.02)
    s5_lam_re = -0.5 + nrm(ks[3], (N_A, G, P), 0.01)
    s5_lam_im = math.pi * jnp.arange(P, dtype=f32)[None, None, :] + nrm(ks[4], (N_A, G, P), 0.01)
    s5_log_dt = jax.random.uniform(ks[5], (N_A, G), f32, math.log(DT_MIN), math.log(DT_MAX))
    s5_b_re = nrm(ks[6], (N_A, G, P, Q), (2 * Q) ** -0.5)
    s5_b_im = nrm(ks[7], (N_A, G, P, Q), (2 * Q) ** -0.5)
    s5_c_re = nrm(ks[8], (N_A, G, Q, P), 0.5)
    s5_c_im = nrm(ks[9], (N_A, G, Q, P), 0.5)
    s5_d = nrm(ks[10], (N_A, D), 0.5)
    s5_w_glu = nrm(ks[11], (N_A, D, 2 * D), D ** -0.5)
    kv_norm = 1.0 + nrm(ks[12], (D,), 0.02)
    w_kv = nrm(ks[13], (D, 2 * HD), D ** -0.5)
    w_q = nrm(ks[14], (N_B, D, HD), D ** -0.5)
    w_o = nrm(ks[15], (N_B, HD, D), HD ** -0.5)
    ffn_w1 = nrm(ks[16], (N_EVEN, D, D_FF_DENSE), D ** -0.5)
    ffn_w3 = nrm(ks[17], (N_EVEN, D, D_FF_DENSE), D ** -0.5)
    ffn_w2 = nrm(ks[18], (N_EVEN, D_FF_DENSE, D), D_FF_DENSE ** -0.5)
    moe_router = nrm(ks[19], (N_ODD, D, N_EXPERTS), D ** -0.5)
    moe_bias = nrm(ks[20], (N_ODD, N_EXPERTS), 0.01)
    moe_w1 = nrm(ks[21], (N_ODD, N_EXPERTS, D, D_FF_EXPERT), D ** -0.5)
    moe_w3 = nrm(ks[22], (N_ODD, N_EXPERTS, D, D_FF_EXPERT), D ** -0.5)
    moe_w2 = nrm(ks[23], (N_ODD, N_EXPERTS, D_FF_EXPERT, D), D_FF_EXPERT ** -0.5)
    return {"x": x, "positions": positions, "norms": norms,
            "s5_lam_re": s5_lam_re, "s5_lam_im": s5_lam_im, "s5_log_dt": s5_log_dt,
            "s5_b_re": s5_b_re, "s5_b_im": s5_b_im, "s5_c_re": s5_c_re, "s5_c_im": s5_c_im,
            "s5_d": s5_d, "s5_w_glu": s5_w_glu,
            "kv_norm": kv_norm, "w_kv": w_kv, "w_q": w_q, "w_o": w_o,
            "ffn_w1": ffn_w1, "ffn_w3": ffn_w3, "ffn_w2": ffn_w2,
            "moe_router": moe_router, "moe_bias": moe_bias,
            "moe_w1": moe_w1, "moe_w3": moe_w3, "moe_w2": moe_w2}


def reference(x, positions, norms, s5_lam_re, s5_lam_im, s5_log_dt, s5_b_re, s5_b_im,
              s5_c_re, s5_c_im, s5_d, s5_w_glu, kv_norm, w_kv, w_q, w_o,
              ffn_w1, ffn_w3, ffn_w2, moe_router, moe_bias, moe_w1, moe_w3, moe_w2):
    h = x
    kb = vb = kmean = None
    for layer in range(DEPTH):
        g = norms[layer]
        if layer < N_A:
            mix = s5_mixer(rmsnorm(h, g[0]), s5_lam_re[layer], s5_lam_im[layer], s5_log_dt[layer],
                           s5_b_re[layer], s5_b_im[layer], s5_c_re[layer], s5_c_im[layer],
                           s5_d[layer], s5_w_glu[layer])
        else:
            if layer == N_A:
                kb, vb, kmean = shared_kv(h, kv_norm, w_kv, positions)
            i = layer - N_A
            mix = moba_mixer(rmsnorm(h, g[0]), w_q[i], w_o[i], kb, vb, kmean, positions)
        h = h + rmsnorm(mix, g[1])
        f_in = rmsnorm(h, g[2])
        if layer % 2 == 0:
            j = layer // 2
            f = dense_swiglu(f_in, ffn_w1[j], ffn_w3[j], ffn_w2[j])
        else:
            j = layer // 2
            f = moe_swiglu(f_in, moe_router[j], moe_bias[j], moe_w1[j], moe_w3[j], moe_w2[j])
        h = h + rmsnorm(f, g[3])
    return h
```

```python
import functools
import math

import numpy as np
import jax
import jax.numpy as jnp
from jax import lax
from jax.experimental import pallas as pl
from jax.experimental.pallas import tpu as pltpu

F32 = jnp.float32
BF16 = jnp.bfloat16

NORM_EPS = 1e-6
NEG_INF = -1e30

SSM_GROUP = 16
N_HEADS = 8
HEAD_DIM = 128
ROT_DIM = HEAD_DIM // 4
ROPE_THETA = 500000.0
MOBA_BLOCK = 256
MOBA_TOPK = 3
N_EXPERTS = 8

LANES = 128
GROUPS_PER_TILE = LANES // SSM_GROUP

VMEM_LIMIT = 56 * 1024 * 1024

S5_CHUNK = 32
S5_ROW_PAD = 8
ROW_TILE = 512
QKV_TILE = 512
MOE_ROWS = 512
DISPATCH_TILE = 256
RANK_TILE = 512


def _cparams(*sem):
    return pltpu.CompilerParams(dimension_semantics=sem, vmem_limit_bytes=VMEM_LIMIT)


def _rms(x, g):
    return x * lax.rsqrt(jnp.mean(x * x, axis=-1, keepdims=True) + NORM_EPS) * g


def _const_spec(shape):
    n = len(shape)
    return pl.BlockSpec(shape, lambda *_: (0,) * n, pipeline_mode=pl.Buffered(1))


def _s5_prep_kernel(lr_ref, li_ref, ldt_ref, br_ref, bi_ref,
                    ar_ref, ai_ref, bbr_ref, bbi_ref):
    lr = lr_ref[...]
    li = li_ref[...]
    dt = jnp.exp(ldt_ref[...])
    zr = lr * dt
    zi = li * dt
    mag = jnp.exp(zr)
    a_r = mag * jnp.cos(zr * 0 + zi)
    a_i = mag * jnp.sin(zi)
    den = lr * lr + li * li
    coef_r = ((a_r - 1.0) * lr + a_i * li) / den
    coef_i = (a_i * lr - (a_r - 1.0) * li) / den
    br = br_ref[...]
    bi = bi_ref[...]
    ar_ref[...] = a_r
    ai_ref[...] = a_i
    bbr_ref[...] = coef_r * br - coef_i * bi
    bbi_ref[...] = coef_r * bi + coef_i * br


def _s5_prep(lam_re, lam_im, log_dt, b_re, b_im):
    G, P = lam_re.shape
    Q = b_re.shape[-1]
    rep = lambda a: jnp.repeat(a.astype(F32), Q, axis=1)
    out = jax.ShapeDtypeStruct((G, P * Q), F32)
    a_r, a_i, bb_r, bb_i = pl.pallas_call(
        _s5_prep_kernel, out_shape=(out, out, out, out), name="s5_prep",
    )(rep(lam_re), rep(lam_im), log_dt.astype(F32).reshape(G, 1),
      b_re.astype(F32).reshape(G, P * Q), b_im.astype(F32).reshape(G, P * Q))
    a_r = a_r.reshape(G, P, Q)[:, :, 0]
    a_i = a_i.reshape(G, P, Q)[:, :, 0]
    return a_r, a_i, bb_r.reshape(G, P, Q), bb_i.reshape(G, P, Q)


def _s5_block_diag(a_r, a_i, bb_r, bb_i, c_re, c_im):
    G, P, Q = bb_r.shape
    gt = GROUPS_PER_TILE
    nt = G // gt
    eye = jnp.eye(gt, dtype=F32)
    def bmat(bb):
        return jnp.einsum('kgpi,gh->kgihp', bb.reshape(nt, gt, P, Q), eye).reshape(nt, gt * Q, gt * P)
    def cmat(c):
        return jnp.einsum('kgip,gh->kgphi', c.astype(F32).reshape(nt, gt, Q, P), eye).reshape(nt, gt * P, gt * Q)
    b_bd = jnp.concatenate([bmat(bb_r), bmat(bb_i)], axis=-1).astype(BF16)
    c_bd = jnp.concatenate([cmat(c_re), -cmat(c_im)], axis=1).astype(BF16)
    a_r_t = a_r.reshape(nt, 1, gt * P)
    a_i_t = a_i.reshape(nt, 1, gt * P)
    return a_r_t, a_i_t, b_bd, c_bd


def _s5_layer_kernel(x_ref, g0_ref, g1_ref, d_ref, ar_ref, ai_ref, bbd_ref, cbd_ref, wglu_ref,
                     o_ref, bm, tm, bu, y_tm, state, *, nb, lc, d, nt, hw):
    c = pl.program_id(0)
    pitch = lc + S5_ROW_PAD

    @pl.when(c == 0)
    def _():
        state[...] = jnp.zeros_like(state)

    for b in range(nb):
        u = _rms(x_ref[b], g0_ref[...])
        for k in range(nt):
            bm[k, b * pitch:b * pitch + lc, :] = u[:, k * LANES:(k + 1) * LANES]
    for t in range(lc):
        for k in range(nt):
            tm[k, t * nb:(t + 1) * nb, :] = bm[k, pl.ds(t, nb, stride=pitch), :]

    for k in range(nt):
        uk = tm[k]
        bu[...] = jnp.dot(uk.astype(BF16), bbd_ref[k], preferred_element_type=F32)
        a_r = jnp.broadcast_to(ar_ref[k], (nb, hw))
        a_i = jnp.broadcast_to(ai_ref[k], (nb, hw))

        def step(t, carry):
            s_r, s_i = carry
            r0 = pl.multiple_of(t * nb, nb)
            n_r = a_r * s_r - a_i * s_i + bu[pl.ds(r0, nb), 0:hw]
            n_i = a_r * s_i + a_i * s_r + bu[pl.ds(r0, nb), hw:2 * hw]
            bu[pl.ds(r0, nb), 0:hw] = n_r
            bu[pl.ds(r0, nb), hw:2 * hw] = n_i
            return n_r, n_i

        s_r, s_i = lax.fori_loop(0, lc, step, (state[k, :, 0:hw], state[k, :, hw:2 * hw]), unroll=4)
        state[k, :, 0:hw] = s_r
        state[k, :, hw:2 * hw] = s_i
        y_tm[:, k * LANES:(k + 1) * LANES] = (
            jnp.dot(bu[...].astype(BF16), cbd_ref[k], preferred_element_type=F32)
            + d_ref[:, k * LANES:(k + 1) * LANES] * uk)

    gl = jax.nn.gelu(y_tm[...]).astype(BF16)
    z = jnp.dot(gl, wglu_ref[...], preferred_element_type=F32)
    mix = z[:, :d] * jax.nn.sigmoid(z[:, d:])
    mixn = _rms(mix, g1_ref[...])
    for k in range(nt):
        tm[k] = mixn[:, k * LANES:(k + 1) * LANES]
    for t in range(lc):
        for k in range(nt):
            bm[k, pl.ds(t, nb, stride=pitch), :] = tm[k, t * nb:(t + 1) * nb, :]
    for b in range(nb):
        for k in range(nt):
            sl = slice(k * LANES, (k + 1) * LANES)
            o_ref[b, :, sl] = x_ref[b, :, sl] + bm[k, b * pitch:b * pitch + lc, :]


def _s5_layer(x, g0, g1, d_skip, a_r_t, a_i_t, b_bd, c_bd, w_glu):
    B, S, D = x.shape
    lc = S5_CHUNK
    nt = D // LANES
    hw = a_r_t.shape[-1]
    rows = B * lc
    kern = functools.partial(_s5_layer_kernel, nb=B, lc=lc, d=D, nt=nt, hw=hw)
    return pl.pallas_call(
        kern,
        grid=(S // lc,),
        in_specs=[
            pl.BlockSpec((B, lc, D), lambda c: (0, c, 0)),
            _const_spec((1, D)), _const_spec((1, D)), _const_spec((1, D)),
            _const_spec((nt, 1, hw)), _const_spec((nt, 1, hw)),
            _const_spec((nt, LANES, 2 * hw)), _const_spec((nt, 2 * hw, LANES)),
            _const_spec((D, 2 * D)),
        ],
        out_specs=pl.BlockSpec((B, lc, D), lambda c: (0, c, 0)),
        out_shape=jax.ShapeDtypeStruct((B, S, D), F32),
        scratch_shapes=[
            pltpu.VMEM((nt, B * (lc + S5_ROW_PAD), LANES), F32), pltpu.VMEM((nt, rows, LANES), F32),
            pltpu.VMEM((rows, 2 * hw), F32), pltpu.VMEM((rows, D), F32),
            pltpu.VMEM((nt, B, 2 * hw), F32),
        ],
        compiler_params=_cparams("arbitrary"),
        name="s5_layer",
    )(x, g0, g1, d_skip, a_r_t, a_i_t, b_bd, c_bd, w_glu)


def _ffn_kernel(h_ref, g2_ref, g3_ref, w1_ref, w3_ref, w2_ref, o_ref):
    h = h_ref[...]
    f_in = _rms(h, g2_ref[...]).astype(BF16)
    a = jnp.dot(f_in, w1_ref[...], preferred_element_type=F32)
    b = jnp.dot(f_in, w3_ref[...], preferred_element_type=F32)
    act = (jax.nn.silu(a) * b).astype(BF16)
    f = jnp.dot(act, w2_ref[...], preferred_element_type=F32)
    o_ref[...] = h + _rms(f, g3_ref[...])


def _dense_ffn(h, g2, g3, w1, w3, w2):
    N, D = h.shape
    Fd = w1.shape[1]
    tm = ROW_TILE
    return pl.pallas_call(
        _ffn_kernel,
        grid=(N // tm,),
        in_specs=[
            pl.BlockSpec((tm, D), lambda i: (i, 0)),
            _const_spec((1, D)), _const_spec((1, D)),
            _const_spec((D, Fd)), _const_spec((D, Fd)), _const_spec((Fd, D)),
        ],
        out_specs=pl.BlockSpec((tm, D), lambda i: (i, 0)),
        out_shape=jax.ShapeDtypeStruct((N, D), F32),
        compiler_params=_cparams("parallel"),
        name="dense_ffn",
    )(h, g2, g3, w1, w3, w2)


def _rotary(xh, cs, s_neg, s_pos):
    half = ROT_DIM // 2
    return xh * cs + pltpu.roll(xh, LANES - half, 1) * s_neg + pltpu.roll(xh, half, 1) * s_pos


def _qkv_kernel(h_ref, pos_ref, invf_ref, sneg_ref, spos_ref, gq_ref, gkv_ref,
                wq_ref, wk_ref, wvt_ref, q_ref, k_ref, vt_ref, km_ref, *, nblk_tile):
    ti = pl.program_id(1)
    h = h_ref[0]
    r = lax.rsqrt(jnp.mean(h * h, axis=-1, keepdims=True) + NORM_EPS)
    hq = (h * r * gq_ref[...]).astype(BF16)
    hk = (h * r * gkv_ref[...]).astype(BF16)
    ang = pos_ref[0] * invf_ref[...]
    cs = jnp.cos(ang)
    sn = jnp.sin(ang)
    s_neg = sn * sneg_ref[...]
    s_pos = sn * spos_ref[...]
    q = jnp.dot(hq, wq_ref[...], preferred_element_type=F32)
    k = jnp.dot(hk, wk_ref[...], preferred_element_type=F32)
    vt = lax.dot_general(wvt_ref[...], hk, (((1,), (1,)), ((), ())), preferred_element_type=F32)
    for hh in range(N_HEADS):
        sl = slice(hh * HEAD_DIM, (hh + 1) * HEAD_DIM)
        q_ref[0, hh] = _rotary(q[:, sl], cs, s_neg, s_pos).astype(BF16)
        kr = _rotary(k[:, sl], cs, s_neg, s_pos)
        for j in range(nblk_tile):
            rs = slice(j * MOBA_BLOCK, (j + 1) * MOBA_BLOCK)
            k_ref[0, hh, j] = kr[rs].astype(BF16)
            vt_ref[0, hh, j] = vt[sl, rs].astype(BF16)
            km_ref[0, hh, pl.ds(ti * nblk_tile + j, 1), :] = jnp.mean(kr[rs], axis=0, keepdims=True)


def _qkv(h2, posb, invf, sneg, spos, gq, gkv, wq, wk, wvt):
    B, S, D = h2.shape
    T = QKV_TILE
    H, Dh, blk = N_HEADS, HEAD_DIM, MOBA_BLOCK
    nb = S // blk
    nbt = T // blk
    kern = functools.partial(_qkv_kernel, nblk_tile=nbt)
    return pl.pallas_call(
        kern,
        grid=(B, S // T),
        in_specs=[
            pl.BlockSpec((1, T, D), lambda b, t: (b, t, 0)),
            pl.BlockSpec((1, T, LANES), lambda b, t: (b, t, 0)),
            _const_spec((1, LANES)), _const_spec((1, LANES)), _const_spec((1, LANES)),
            _const_spec((1, D)), _const_spec((1, D)),
            _const_spec((D, D)), _const_spec((D, D)), _const_spec((D, D)),
        ],
        out_specs=[
            pl.BlockSpec((1, H, T, Dh), lambda b, t: (b, 0, t, 0)),
            pl.BlockSpec((1, H, nbt, blk, Dh), lambda b, t: (b, 0, t, 0, 0)),
            pl.BlockSpec((1, H, nbt, Dh, blk), lambda b, t: (b, 0, t, 0, 0)),
            pl.BlockSpec((1, H, nb, Dh), lambda b, t: (b, 0, 0, 0)),
        ],
        out_shape=[
            jax.ShapeDtypeStruct((B, H, S, Dh), BF16),
            jax.ShapeDtypeStruct((B, H, nb, blk, Dh), BF16),
            jax.ShapeDtypeStruct((B, H, nb, Dh, blk), BF16),
            jax.ShapeDtypeStruct((B, H, nb, Dh), F32),
        ],
        compiler_params=_cparams("parallel", "arbitrary"),
        name="qkv_proj",
    )(h2, posb, invf, sneg, spos, gq, gkv, wq, wk, wvt)


def _moba_kernel(q_ref, k_ref, vt_ref, km_ref, h_ref, g1_ref, wo_ref, o_ref,
                 st_scr, bias_scr, o_scr, *, nb, k_sel):
    j = pl.program_id(1)
    blk = MOBA_BLOCK
    scale = HEAD_DIM ** -0.5
    nt_dims = (((1,), (1,)), ((), ()))
    blk_id = lax.broadcasted_iota(jnp.int32, (nb, blk), 0)
    past = blk_id < j
    kidx = lax.broadcasted_iota(jnp.int32, (blk, blk), 0)
    qidx = lax.broadcasted_iota(jnp.int32, (blk, blk), 1)
    causal = kidx <= qidx

    for hh in range(N_HEADS):
        qh = q_ref[0, hh]
        gate = lax.dot_general(km_ref[0, hh], qh.astype(F32), nt_dims,
                               precision=lax.Precision.HIGHEST, preferred_element_type=F32)
        gate = jnp.where(past, gate, NEG_INF)
        rank = jnp.zeros((nb, blk), F32)
        for m in range(nb):
            gm = gate[m:m + 1, :]
            beats = jnp.where(gm > gate, 1.0, jnp.where(gm == gate, jnp.where(m < blk_id, 1.0, 0.0), 0.0))
            rank = rank + beats
        sel = jnp.where(past, jnp.where(rank < k_sel, 1.0, 0.0), 0.0)
        bias_scr[...] = sel

        def scores(n, m_run):
            st = lax.dot_general(k_ref[0, hh, n], qh, nt_dims, preferred_element_type=F32) * scale
            st = jnp.where(bias_scr[pl.ds(n, 1), :] > 0.5, st, NEG_INF)
            st_scr[n] = st
            return jnp.maximum(m_run, jnp.max(st, axis=0, keepdims=True))

        m_run = lax.fori_loop(0, j, scores, jnp.full((1, blk), NEG_INF, F32))
        st = lax.dot_general(k_ref[0, hh, j], qh, nt_dims, preferred_element_type=F32) * scale
        st = jnp.where(causal, st, NEG_INF)
        m_fin = jnp.maximum(m_run, jnp.max(st, axis=0, keepdims=True))
        p = jnp.exp(st - m_fin)
        l0 = jnp.sum(p, axis=0, keepdims=True)
        acc0 = jnp.dot(vt_ref[0, hh, j], p.astype(BF16), preferred_element_type=F32)

        def pv(n, carry):
            l_run, acc = carry
            pn = jnp.exp(st_scr[n] - m_fin)
            return (l_run + jnp.sum(pn, axis=0, keepdims=True),
                    acc + jnp.dot(vt_ref[0, hh, n], pn.astype(BF16), preferred_element_type=F32))

        l_fin, acc = lax.fori_loop(0, j, pv, (l0, acc0))
        o_t = acc / l_fin
        o_scr[:, hh * HEAD_DIM:(hh + 1) * HEAD_DIM] = o_t.T.astype(BF16)

    mix = jnp.dot(o_scr[...], wo_ref[...], preferred_element_type=F32)
    o_ref[0] = h_ref[0] + _rms(mix, g1_ref[...])


def _moba_layer(q, k5, vt5, kmean, h2, g1, wo):
    B, S, D = h2.shape
    H, Dh, blk = N_HEADS, HEAD_DIM, MOBA_BLOCK
    nb = S // blk
    kern = functools.partial(_moba_kernel, nb=nb, k_sel=min(MOBA_TOPK, nb))
    return pl.pallas_call(
        kern,
        grid=(B, nb),
        in_specs=[
            pl.BlockSpec((1, H, blk, Dh), lambda b, j: (b, 0, j, 0)),
            pl.BlockSpec((1, H, nb, blk, Dh), lambda b, j: (b, 0, 0, 0, 0)),
            pl.BlockSpec((1, H, nb, Dh, blk), lambda b, j: (b, 0, 0, 0, 0)),
            pl.BlockSpec((1, H, nb, Dh), lambda b, j: (b, 0, 0, 0)),
            pl.BlockSpec((1, blk, D), lambda b, j: (b, j, 0)),
            _const_spec((1, D)), _const_spec((D, D)),
        ],
        out_specs=pl.BlockSpec((1, blk, D), lambda b, j: (b, j, 0)),
        out_shape=jax.ShapeDtypeStruct((B, S, D), F32),
        scratch_shapes=[
            pltpu.VMEM((nb, blk, blk), F32), pltpu.VMEM((nb, blk), F32), pltpu.VMEM((blk, D), BF16),
        ],
        compiler_params=_cparams("parallel", "arbitrary"),
        name="moba_attn",
    )(q, k5, vt5, kmean, h2, g1, wo)


def _router_kernel(h_ref, g2_ref, wrt_ref, br_ref, f_ref, e_ref, w_ref, wcol_ref):
    f = _rms(h_ref[...], g2_ref[...])
    f_ref[...] = f
    logit = lax.dot_general(wrt_ref[...], f, (((1,), (1,)), ((), ())),
                            precision=lax.Precision.HIGHEST, preferred_element_type=F32) + br_ref[...]
    ne, t = logit.shape
    eidx = lax.broadcasted_iota(jnp.int32, (ne, t), 0)
    m1 = jnp.max(logit, axis=0, keepdims=True)
    e1 = jnp.min(jnp.where(logit == m1, eidx, ne), axis=0, keepdims=True)
    rest = jnp.where(eidx == e1, -jnp.inf, logit)
    m2 = jnp.max(rest, axis=0, keepdims=True)
    e2 = jnp.min(jnp.where(rest == m2, eidx, ne), axis=0, keepdims=True)
    ex = jnp.exp(m2 - m1)
    den = 1.0 + ex
    w1 = 1.0 / den
    w2 = ex / den
    e_ref[0] = jnp.concatenate([e1, e2], axis=0)
    w_ref[0] = jnp.concatenate([w1, w2], axis=0)
    wpad = jnp.concatenate([w1, w2, jnp.zeros((LANES - 2, t), F32)], axis=0)
    wcol_ref[...] = wpad.T


def _router(h3, g2, wrt, br):
    N, D = h3.shape
    E = wrt.shape[0]
    T = ROW_TILE
    nt = N // T
    return pl.pallas_call(
        _router_kernel,
        grid=(nt,),
        in_specs=[
            pl.BlockSpec((T, D), lambda i: (i, 0)),
            _const_spec((1, D)), _const_spec((E, D)), _const_spec((E, 1)),
        ],
        out_specs=[
            pl.BlockSpec((T, D), lambda i: (i, 0)),
            pl.BlockSpec((1, 2, T), lambda i: (i, 0, 0)),
            pl.BlockSpec((1, 2, T), lambda i: (i, 0, 0)),
            pl.BlockSpec((T, LANES), lambda i: (i, 0)),
        ],
        out_shape=[
            jax.ShapeDtypeStruct((N, D), F32),
            jax.ShapeDtypeStruct((nt, 2, T), jnp.int32),
            jax.ShapeDtypeStruct((nt, 2, T), F32),
            jax.ShapeDtypeStruct((N, LANES), F32),
        ],
        compiler_params=_cparams("parallel"),
        name="moe_router",
    )(h3, g2, wrt, br)


def _rank_kernel(e_ref, dest_ref, blke_ref, nused_ref, *, ne, rows_log2, nblk_pad):
    ntile, _, t = e_ref.shape
    eidx = lax.broadcasted_iota(jnp.int32, (ne, t), 0)

    def onehots(i):
        e = e_ref[i]
        oh0 = jnp.where(eidx == e[0:1, :], 1.0, 0.0)
        oh1 = jnp.where(eidx == e[1:2, :], 1.0, 0.0)
        return oh0, oh1

    def count(i, cnt):
        oh0, oh1 = onehots(i)
        return cnt + jnp.sum(oh0 + oh1, axis=1, keepdims=True).astype(jnp.int32)

    cnt = lax.fori_loop(0, ntile, count, jnp.zeros((ne, 1), jnp.int32))
    pcnt = ((cnt + ((1 << rows_log2) - 1)) >> rows_log2) << rows_log2
    run = jnp.zeros((1, 1), jnp.int32)
    starts, ends = [], []
    for e in range(ne):
        starts.append(run)
        run = run + pcnt[e:e + 1, :]
        ends.append(run)
    pstart = jnp.concatenate(starts, axis=0)
    pend = jnp.concatenate(ends, axis=0)

    upper = jnp.where(lax.broadcasted_iota(jnp.int32, (t, t), 0) < lax.broadcasted_iota(jnp.int32, (t, t), 1),
                      1.0, 0.0).astype(BF16)

    def place(i, base):
        oh0, oh1 = onehots(i)
        both = oh0 + oh1
        before = jnp.dot(both.astype(BF16), upper, preferred_element_type=F32).astype(jnp.int32)
        slot = base + before
        d0 = jnp.sum(jnp.where(oh0 > 0.5, slot, 0), axis=0, keepdims=True)
        d1 = jnp.sum(jnp.where(oh1 > 0.5, slot, 0), axis=0, keepdims=True)
        dest_ref[i] = jnp.concatenate([d0, d1], axis=0)
        return base + jnp.sum(both, axis=1, keepdims=True).astype(jnp.int32)

    lax.fori_loop(0, ntile, place, pstart)

    blk_row = lax.broadcasted_iota(jnp.int32, (ne, nblk_pad), 1) << rows_log2
    blke = jnp.sum(jnp.where(pend <= blk_row, 1, 0), axis=0, keepdims=True)
    blke_ref[...] = jnp.minimum(blke, ne - 1)
    nused_ref[...] = jnp.broadcast_to(pend[ne - 1:ne, :] >> rows_log2, (1, LANES))


def _rank(e3, ne, nblk_pad):
    ntile, _, t = e3.shape
    kern = functools.partial(_rank_kernel, ne=ne, rows_log2=int(math.log2(MOE_ROWS)), nblk_pad=nblk_pad)
    return pl.pallas_call(
        kern,
        out_shape=[
            jax.ShapeDtypeStruct((ntile, 2, t), jnp.int32),
            jax.ShapeDtypeStruct((1, nblk_pad), jnp.int32),
            jax.ShapeDtypeStruct((1, LANES), jnp.int32),
        ],
        compiler_params=pltpu.CompilerParams(vmem_limit_bytes=VMEM_LIMIT),
        name="moe_rank",
    )(e3)


def _dispatch_kernel(dest_ref, f_ref, xs_in_ref, xs_ref, sem, *, td):
    del xs_in_ref

    def issue(t, _):
        for s in range(2):
            d = dest_ref[0, s, t]
            pltpu.make_async_copy(f_ref.at[pl.ds(t, 1), :], xs_ref.at[pl.ds(d, 1), :], sem).start()
        return 0

    lax.fori_loop(0, td, issue, 0, unroll=8)
    for s in range(2):
        pltpu.make_async_copy(f_ref, xs_ref.at[pl.ds(0, td), :], sem).wait()


def _dispatch(dest3, f, total):
    N, D = f.shape
    td = DISPATCH_TILE
    kern = functools.partial(_dispatch_kernel, td=td)
    return pl.pallas_call(
        kern,
        grid=(N // td,),
        in_specs=[
            pl.BlockSpec((1, 2, td), lambda i: (i, 0, 0), memory_space=pltpu.SMEM),
            pl.BlockSpec((td, D), lambda i: (i, 0)),
            pl.BlockSpec(memory_space=pl.ANY),
        ],
        out_specs=pl.BlockSpec(memory_space=pl.ANY),
        out_shape=jax.ShapeDtypeStruct((total, D), F32),
        scratch_shapes=[pltpu.SemaphoreType.DMA(())],
        input_output_aliases={2: 0},
        compiler_params=_cparams("arbitrary"),
        name="moe_dispatch",
    )(dest3, f, jnp.zeros((total, D), F32))


def _expert_kernel(blke_ref, nused_ref, xs_ref, w1_ref, w3_ref, w2_ref, y_ref):
    i = pl.program_id(0)

    @pl.when(i < nused_ref[0])
    def _():
        x = xs_ref[...].astype(BF16)
        a = jnp.dot(x, w1_ref[0], preferred_element_type=F32)
        b = jnp.dot(x, w3_ref[0], preferred_element_type=F32)
        act = (jax.nn.silu(a) * b).astype(BF16)
        y_ref[...] = jnp.dot(act, w2_ref[0], preferred_element_type=F32)

    @pl.when(i >= nused_ref[0])
    def _():
        y_ref[...] = jnp.zeros_like(y_ref)


def _experts(blke, nused, xs, w1, w3, w2):
    total, D = xs.shape
    E, _, Fe = w1.shape
    rb = MOE_ROWS
    nblk = total // rb
    wspec = lambda shape: pl.BlockSpec(shape, lambda i, be, nu: (be[i], 0, 0), pipeline_mode=pl.Buffered(1))
    return pl.pallas_call(
        _expert_kernel,
        grid_spec=pltpu.PrefetchScalarGridSpec(
            num_scalar_prefetch=2,
            grid=(nblk,),
            in_specs=[
                pl.BlockSpec((rb, D), lambda i, be, nu: (i, 0)),
                wspec((1, D, Fe)), wspec((1, D, Fe)), wspec((1, Fe, D)),
            ],
            out_specs=pl.BlockSpec((rb, D), lambda i, be, nu: (i, 0)),
        ),
        out_shape=jax.ShapeDtypeStruct((total, D), F32),
        compiler_params=_cparams("arbitrary"),
        name="moe_experts",
    )(blke, nused, xs, w1, w3, w2)


def _combine_kernel(dest_ref, yb_ref, wcol_ref, h_ref, g3_ref, o_ref, buf, sem, *, td):
    def issue(t, _):
        for s in range(2):
            d = dest_ref[0, s, t]
            pltpu.make_async_copy(yb_ref.at[pl.ds(d, 1), :], buf.at[s, pl.ds(t, 1), :], sem).start()
        return 0

    lax.fori_loop(0, td, issue, 0, unroll=8)
    for s in range(2):
        pltpu.make_async_copy(yb_ref.at[pl.ds(0, td), :], buf.at[s], sem).wait()
    w = wcol_ref[...]
    f = buf[0] * w[:, 0:1] + buf[1] * w[:, 1:2]
    o_ref[...] = h_ref[...] + _rms(f, g3_ref[...])


def _combine(dest3, yb, wcol, h3, g3):
    N, D = h3.shape
    td = DISPATCH_TILE
    kern = functools.partial(_combine_kernel, td=td)
    return pl.pallas_call(
        kern,
        grid=(N // td,),
        in_specs=[
            pl.BlockSpec((1, 2, td), lambda i: (i, 0, 0), memory_space=pltpu.SMEM),
            pl.BlockSpec(memory_space=pl.ANY),
            pl.BlockSpec((td, LANES), lambda i: (i, 0)),
            pl.BlockSpec((td, D), lambda i: (i, 0)),
            _const_spec((1, D)),
        ],
        out_specs=pl.BlockSpec((td, D), lambda i: (i, 0)),
        out_shape=jax.ShapeDtypeStruct((N, D), F32),
        scratch_shapes=[pltpu.VMEM((2, td, D), F32), pltpu.SemaphoreType.DMA(())],
        compiler_params=_cparams("arbitrary"),
        name="moe_combine",
    )(dest3, yb, wcol, h3, g3)


def _rope_lane_tables():
    half = ROT_DIM // 2
    inv_freq = 1.0 / (ROPE_THETA ** (np.arange(half, dtype=np.float32) / half))
    invf = np.zeros((1, LANES), np.float32)
    invf[0, :half] = inv_freq
    invf[0, half:ROT_DIM] = inv_freq
    sneg = np.zeros((1, LANES), np.float32)
    sneg[0, :half] = -1.0
    spos = np.zeros((1, LANES), np.float32)
    spos[0, half:ROT_DIM] = 1.0
    return jnp.asarray(invf), jnp.asarray(sneg), jnp.asarray(spos)


def kernel(x, positions, norms, s5_lam_re, s5_lam_im, s5_log_dt, s5_b_re, s5_b_im, s5_c_re, s5_c_im, s5_d, s5_w_glu, kv_norm, w_kv, w_q, w_o, ffn_w1, ffn_w3, ffn_w2, moe_router, moe_bias, moe_w1, moe_w3, moe_w2):
    B, S, D = x.shape
    N = B * S
    HD = N_HEADS * HEAD_DIM
    E = moe_router.shape[-1]
    row = lambda v: v.astype(F32).reshape(1, -1)

    a_r, a_i, bb_r, bb_i = _s5_prep(s5_lam_re[0], s5_lam_im[0], s5_log_dt[0], s5_b_re[0], s5_b_im[0])
    a_r_t, a_i_t, b_bd, c_bd = _s5_block_diag(a_r, a_i, bb_r, bb_i, s5_c_re[0], s5_c_im[0])
    h1 = _s5_layer(x, row(norms[0, 0]), row(norms[0, 1]), row(s5_d[0]), a_r_t, a_i_t, b_bd, c_bd,
                   s5_w_glu[0].astype(BF16))
    h2 = _dense_ffn(h1.reshape(N, D), row(norms[0, 2]), row(norms[0, 3]),
                    ffn_w1[0].astype(BF16), ffn_w3[0].astype(BF16), ffn_w2[0].astype(BF16))

    invf, sneg, spos = _rope_lane_tables()
    posb = jnp.broadcast_to(positions.astype(F32)[:, :, None], (B, S, LANES))
    q, k5, vt5, kmean = _qkv(h2.reshape(B, S, D), posb, invf, sneg, spos, row(norms[1, 0]), row(kv_norm),
                             w_q[0].astype(BF16), w_kv[:, :HD].astype(BF16), w_kv[:, HD:].T.astype(BF16))
    h3 = _moba_layer(q, k5, vt5, kmean, h2.reshape(B, S, D), row(norms[1, 1]), w_o[0].astype(BF16))
    h3 = h3.reshape(N, D)

    f_in, e3, w3_, wcol = _router(h3, row(norms[1, 2]), moe_router[0].T.astype(F32),
                                  moe_bias[0].astype(F32).reshape(E, 1))
    del w3_
    total = 2 * N + E * MOE_ROWS
    nblk = total // MOE_ROWS
    nblk_pad = -(-nblk // LANES) * LANES
    rt = RANK_TILE
    e_rank = e3.transpose(1, 0, 2).reshape(2, N // rt, rt).transpose(1, 0, 2)
    dest, blke, nused = _rank(e_rank, E, nblk_pad)
    td = DISPATCH_TILE
    dest3 = dest.transpose(1, 0, 2).reshape(2, N // td, td).transpose(1, 0, 2)
    xs = _dispatch(dest3, f_in, total)
    yb = _experts(blke[0, :nblk], nused[0, :1], xs,
                  moe_w1[0].astype(BF16), moe_w3[0].astype(BF16), moe_w2[0].astype(BF16))
    out = _combine(dest3, yb, wcol, h3, row(norms[1, 3]))
    return out.reshape(B, S, D)
```

```python
import functools
import math

import numpy as np
import jax
import jax.numpy as jnp
from jax import lax
from jax.experimental import pallas as pl
from jax.experimental.pallas import tpu as pltpu

F32 = jnp.float32
BF16 = jnp.bfloat16

NORM_EPS = 1e-6
NEG_INF = -1e30

SSM_GROUP = 16
N_HEADS = 8
HEAD_DIM = 128
ROT_DIM = HEAD_DIM // 4
ROPE_THETA = 500000.0
MOBA_BLOCK = 256
MOBA_TOPK = 3
N_EXPERTS = 8

Q_SCALE_LOG2 = HEAD_DIM ** -0.5 * math.log2(math.e)

LANES = 128
GROUPS_PER_TILE = LANES // SSM_GROUP

VMEM_LIMIT = 56 * 1024 * 1024

S5_CHUNK = 32
S5_ROW_PAD = 8
ROW_TILE = 512
QKV_TILE = 512
MOE_ROWS = 512
DISPATCH_TILE = 256
RANK_TILE = 512


def _cparams(*sem):
    return pltpu.CompilerParams(dimension_semantics=sem, vmem_limit_bytes=VMEM_LIMIT)


def _rms(x, g):
    return x * lax.rsqrt(jnp.mean(x * x, axis=-1, keepdims=True) + NORM_EPS) * g


def _const_spec(shape):
    n = len(shape)
    return pl.BlockSpec(shape, lambda *_: (0,) * n, pipeline_mode=pl.Buffered(1))


def _s5_prep_kernel(lr_ref, li_ref, ldt_ref, br_ref, bi_ref,
                    ar_ref, ai_ref, bbr_ref, bbi_ref):
    lr = lr_ref[...]
    li = li_ref[...]
    dt = jnp.exp(ldt_ref[...])
    zr = lr * dt
    zi = li * dt
    mag = jnp.exp(zr)
    a_r = mag * jnp.cos(zr * 0 + zi)
    a_i = mag * jnp.sin(zi)
    den = lr * lr + li * li
    coef_r = ((a_r - 1.0) * lr + a_i * li) / den
    coef_i = (a_i * lr - (a_r - 1.0) * li) / den
    br = br_ref[...]
    bi = bi_ref[...]
    ar_ref[...] = a_r
    ai_ref[...] = a_i
    bbr_ref[...] = coef_r * br - coef_i * bi
    bbi_ref[...] = coef_r * bi + coef_i * br


def _s5_prep(lam_re, lam_im, log_dt, b_re, b_im):
    G, P = lam_re.shape
    Q = b_re.shape[-1]
    rep = lambda a: jnp.repeat(a.astype(F32), Q, axis=1)
    out = jax.ShapeDtypeStruct((G, P * Q), F32)
    a_r, a_i, bb_r, bb_i = pl.pallas_call(
        _s5_prep_kernel, out_shape=(out, out, out, out), name="s5_prep",
    )(rep(lam_re), rep(lam_im), log_dt.astype(F32).reshape(G, 1),
      b_re.astype(F32).reshape(G, P * Q), b_im.astype(F32).reshape(G, P * Q))
    a_r = a_r.reshape(G, P, Q)[:, :, 0]
    a_i = a_i.reshape(G, P, Q)[:, :, 0]
    return a_r, a_i, bb_r.reshape(G, P, Q), bb_i.reshape(G, P, Q)


def _s5_block_diag(a_r, a_i, bb_r, bb_i, c_re, c_im):
    G, P, Q = bb_r.shape
    gt = GROUPS_PER_TILE
    nt = G // gt
    eye = jnp.eye(gt, dtype=F32)
    def bmat(bb):
        return jnp.einsum('kgpi,gh->kgihp', bb.reshape(nt, gt, P, Q), eye).reshape(nt, gt * Q, gt * P)
    def cmat(c):
        return jnp.einsum('kgip,gh->kgphi', c.astype(F32).reshape(nt, gt, Q, P), eye).reshape(nt, gt * P, gt * Q)
    b_bd = jnp.concatenate([bmat(bb_r), bmat(bb_i)], axis=-1).astype(BF16)
    c_bd = jnp.concatenate([cmat(c_re), -cmat(c_im)], axis=1).astype(BF16)
    a_r_t = a_r.reshape(nt, 1, gt * P)
    a_i_t = a_i.reshape(nt, 1, gt * P)
    return a_r_t, a_i_t, b_bd, c_bd


def _s5_layer_kernel(x_ref, g0_ref, g1_ref, d_ref, ar_ref, ai_ref, bbd_ref, cbd_ref, wglu_ref,
                     o_ref, bm, tm, bu, y_tm, state, *, nb, lc, d, nt, hw):
    c = pl.program_id(0)
    pitch = lc + S5_ROW_PAD

    @pl.when(c == 0)
    def _():
        state[...] = jnp.zeros_like(state)

    for b in range(nb):
        u = _rms(x_ref[b], g0_ref[...])
        for k in range(nt):
            bm[k, b * pitch:b * pitch + lc, :] = u[:, k * LANES:(k + 1) * LANES]
    for t in range(lc):
        for k in range(nt):
            tm[k, t * nb:(t + 1) * nb, :] = bm[k, pl.ds(t, nb, stride=pitch), :]

    for k in range(nt):
        uk = tm[k]
        bu[...] = jnp.dot(uk.astype(BF16), bbd_ref[k], preferred_element_type=F32)
        a_r = jnp.broadcast_to(ar_ref[k], (nb, hw))
        a_i = jnp.broadcast_to(ai_ref[k], (nb, hw))

        def step(t, carry):
            s_r, s_i = carry
            r0 = pl.multiple_of(t * nb, nb)
            n_r = a_r * s_r - a_i * s_i + bu[pl.ds(r0, nb), 0:hw]
            n_i = a_r * s_i + a_i * s_r + bu[pl.ds(r0, nb), hw:2 * hw]
            bu[pl.ds(r0, nb), 0:hw] = n_r
            bu[pl.ds(r0, nb), hw:2 * hw] = n_i
            return n_r, n_i

        s_r, s_i = lax.fori_loop(0, lc, step, (state[k, :, 0:hw], state[k, :, hw:2 * hw]), unroll=4)
        state[k, :, 0:hw] = s_r
        state[k, :, hw:2 * hw] = s_i
        y_tm[:, k * LANES:(k + 1) * LANES] = (
            jnp.dot(bu[...].astype(BF16), cbd_ref[k], preferred_element_type=F32)
            + d_ref[:, k * LANES:(k + 1) * LANES] * uk)

    gl = jax.nn.gelu(y_tm[...]).astype(BF16)
    z = jnp.dot(gl, wglu_ref[...], preferred_element_type=F32)
    mix = z[:, :d] * jax.nn.sigmoid(z[:, d:])
    mixn = _rms(mix, g1_ref[...])
    for k in range(nt):
        tm[k] = mixn[:, k * LANES:(k + 1) * LANES]
    for t in range(lc):
        for k in range(nt):
            bm[k, pl.ds(t, nb, stride=pitch), :] = tm[k, t * nb:(t + 1) * nb, :]
    for b in range(nb):
        for k in range(nt):
            sl = slice(k * LANES, (k + 1) * LANES)
            o_ref[b, :, sl] = x_ref[b, :, sl] + bm[k, b * pitch:b * pitch + lc, :]


def _s5_layer(x, g0, g1, d_skip, a_r_t, a_i_t, b_bd, c_bd, w_glu):
    B, S, D = x.shape
    lc = S5_CHUNK
    nt = D // LANES
    hw = a_r_t.shape[-1]
    rows = B * lc
    kern = functools.partial(_s5_layer_kernel, nb=B, lc=lc, d=D, nt=nt, hw=hw)
    return pl.pallas_call(
        kern,
        grid=(S // lc,),
        in_specs=[
            pl.BlockSpec((B, lc, D), lambda c: (0, c, 0)),
            _const_spec((1, D)), _const_spec((1, D)), _const_spec((1, D)),
            _const_spec((nt, 1, hw)), _const_spec((nt, 1, hw)),
            _const_spec((nt, LANES, 2 * hw)), _const_spec((nt, 2 * hw, LANES)),
            _const_spec((D, 2 * D)),
        ],
        out_specs=pl.BlockSpec((B, lc, D), lambda c: (0, c, 0)),
        out_shape=jax.ShapeDtypeStruct((B, S, D), F32),
        scratch_shapes=[
            pltpu.VMEM((nt, B * (lc + S5_ROW_PAD), LANES), F32), pltpu.VMEM((nt, rows, LANES), F32),
            pltpu.VMEM((rows, 2 * hw), F32), pltpu.VMEM((rows, D), F32),
            pltpu.VMEM((nt, B, 2 * hw), F32),
        ],
        compiler_params=_cparams("arbitrary"),
        name="s5_layer",
    )(x, g0, g1, d_skip, a_r_t, a_i_t, b_bd, c_bd, w_glu)


def _ffn_kernel(h_ref, g2_ref, g3_ref, w1_ref, w3_ref, w2_ref, o_ref):
    h = h_ref[...]
    f_in = _rms(h, g2_ref[...]).astype(BF16)
    a = jnp.dot(f_in, w1_ref[...], preferred_element_type=F32)
    b = jnp.dot(f_in, w3_ref[...], preferred_element_type=F32)
    act = (jax.nn.silu(a) * b).astype(BF16)
    f = jnp.dot(act, w2_ref[...], preferred_element_type=F32)
    o_ref[...] = h + _rms(f, g3_ref[...])


def _dense_ffn(h, g2, g3, w1, w3, w2):
    N, D = h.shape
    Fd = w1.shape[1]
    tm = ROW_TILE
    return pl.pallas_call(
        _ffn_kernel,
        grid=(N // tm,),
        in_specs=[
            pl.BlockSpec((tm, D), lambda i: (i, 0)),
            _const_spec((1, D)), _const_spec((1, D)),
            _const_spec((D, Fd)), _const_spec((D, Fd)), _const_spec((Fd, D)),
        ],
        out_specs=pl.BlockSpec((tm, D), lambda i: (i, 0)),
        out_shape=jax.ShapeDtypeStruct((N, D), F32),
        compiler_params=_cparams("parallel"),
        name="dense_ffn",
    )(h, g2, g3, w1, w3, w2)


def _rotary(xh, cs, s_neg, s_pos):
    half = ROT_DIM // 2
    return xh * cs + pltpu.roll(xh, LANES - half, 1) * s_neg + pltpu.roll(xh, half, 1) * s_pos


def _qkv_kernel(h_ref, pos_ref, invf_ref, sneg_ref, spos_ref, gq_ref, gkv_ref,
                wq_ref, wk_ref, wvt_ref, q_ref, k_ref, vt_ref, km_ref, *, nblk_tile):
    ti = pl.program_id(1)
    h = h_ref[0]
    r = lax.rsqrt(jnp.mean(h * h, axis=-1, keepdims=True) + NORM_EPS)
    hq = (h * r * gq_ref[...]).astype(BF16)
    hk = (h * r * gkv_ref[...]).astype(BF16)
    ang = pos_ref[0] * invf_ref[...]
    cs = jnp.cos(ang)
    sn = jnp.sin(ang)
    s_neg = sn * sneg_ref[...]
    s_pos = sn * spos_ref[...]
    q = jnp.dot(hq, wq_ref[...], preferred_element_type=F32)
    k = jnp.dot(hk, wk_ref[...], preferred_element_type=F32)
    vt = lax.dot_general(wvt_ref[...], hk, (((1,), (1,)), ((), ())), preferred_element_type=F32)
    for hh in range(N_HEADS):
        sl = slice(hh * HEAD_DIM, (hh + 1) * HEAD_DIM)
        q_ref[0, hh] = (_rotary(q[:, sl], cs, s_neg, s_pos) * Q_SCALE_LOG2).astype(BF16)
        kr = _rotary(k[:, sl], cs, s_neg, s_pos)
        for j in range(nblk_tile):
            rs = slice(j * MOBA_BLOCK, (j + 1) * MOBA_BLOCK)
            k_ref[0, hh, j] = kr[rs].astype(BF16)
            vt_ref[0, hh, j] = vt[sl, rs].astype(BF16)
            km_ref[0, hh, pl.ds(ti * nblk_tile + j, 1), :] = jnp.mean(kr[rs], axis=0, keepdims=True)


def _qkv(h2, posb, invf, sneg, spos, gq, gkv, wq, wk, wvt):
    B, S, D = h2.shape
    T = QKV_TILE
    H, Dh, blk = N_HEADS, HEAD_DIM, MOBA_BLOCK
    nb = S // blk
    nbt = T // blk
    kern = functools.partial(_qkv_kernel, nblk_tile=nbt)
    return pl.pallas_call(
        kern,
        grid=(B, S // T),
        in_specs=[
            pl.BlockSpec((1, T, D), lambda b, t: (b, t, 0)),
            pl.BlockSpec((1, T, LANES), lambda b, t: (b, t, 0)),
            _const_spec((1, LANES)), _const_spec((1, LANES)), _const_spec((1, LANES)),
            _const_spec((1, D)), _const_spec((1, D)),
            _const_spec((D, D)), _const_spec((D, D)), _const_spec((D, D)),
        ],
        out_specs=[
            pl.BlockSpec((1, H, T, Dh), lambda b, t: (b, 0, t, 0)),
            pl.BlockSpec((1, H, nbt, blk, Dh), lambda b, t: (b, 0, t, 0, 0)),
            pl.BlockSpec((1, H, nbt, Dh, blk), lambda b, t: (b, 0, t, 0, 0)),
            pl.BlockSpec((1, H, nb, Dh), lambda b, t: (b, 0, 0, 0)),
        ],
        out_shape=[
            jax.ShapeDtypeStruct((B, H, S, Dh), BF16),
            jax.ShapeDtypeStruct((B, H, nb, blk, Dh), BF16),
            jax.ShapeDtypeStruct((B, H, nb, Dh, blk), BF16),
            jax.ShapeDtypeStruct((B, H, nb, Dh), F32),
        ],
        compiler_params=_cparams("parallel", "arbitrary"),
        name="qkv_proj",
    )(h2, posb, invf, sneg, spos, gq, gkv, wq, wk, wvt)


def _moba_kernel(q_ref, k_ref, vt_ref, km_ref, h_ref, g1_ref, wo_ref, o_ref,
                 sel_scr, st_scr, acc_scr, o_scr, *, nb, k_sel):
    j = pl.program_id(1)
    blk = MOBA_BLOCK
    nt_dims = (((1,), (1,)), ((), ()))
    blk_id = lax.broadcasted_iota(jnp.int32, (nb, blk), 0)
    past = blk_id < j
    causal = (lax.broadcasted_iota(jnp.int32, (blk, blk), 0)
              <= lax.broadcasted_iota(jnp.int32, (blk, blk), 1))

    for hh in range(N_HEADS):
        gate = lax.dot_general(km_ref[0, hh], q_ref[0, hh].astype(F32), nt_dims,
                               precision=lax.Precision.HIGHEST, preferred_element_type=F32)
        gate = jnp.where(past, gate, NEG_INF)
        rank = jnp.zeros((nb, blk), F32)
        for m in range(nb):
            gm = gate[m:m + 1, :]
            beats = jnp.where(gm > gate, 1.0, jnp.where(gm == gate, jnp.where(m < blk_id, 1.0, 0.0), 0.0))
            rank = rank + beats
        sel_scr[hh] = jnp.where(past, jnp.where(rank < k_sel, 1.0, 0.0), 0.0)

    def qk_pass(n, ms, own):
        out = []
        for hh in range(N_HEADS):
            st = lax.dot_general(k_ref[0, hh, n], q_ref[0, hh], nt_dims, preferred_element_type=F32)
            allowed = causal if own else sel_scr[hh, pl.ds(n, 1), :] > 0.5
            st = jnp.where(allowed, st, NEG_INF)
            st_scr[hh, n] = st
            out.append(jnp.maximum(ms[hh], jnp.max(st, axis=0, keepdims=True)))
        return tuple(out)

    m_init = tuple(jnp.full((1, blk), NEG_INF, F32) for _ in range(N_HEADS))
    m_own = qk_pass(j, m_init, True)
    m_fin = lax.fori_loop(0, j, lambda n, ms: qk_pass(n, ms, False), m_own)

    def pv_pass(n, ls, first):
        out = []
        for hh in range(N_HEADS):
            p = jnp.exp2(st_scr[hh, n] - m_fin[hh])
            out.append(ls[hh] + jnp.sum(p, axis=0, keepdims=True))
            pv = jnp.dot(vt_ref[0, hh, n], p.astype(BF16), preferred_element_type=F32)
            acc_scr[hh] = pv if first else acc_scr[hh] + pv
        return tuple(out)

    l_own = pv_pass(j, tuple(jnp.zeros((1, blk), F32) for _ in range(N_HEADS)), True)
    l_fin = lax.fori_loop(0, j, lambda n, ls: pv_pass(n, ls, False), l_own)
    for hh in range(N_HEADS):
        o_t = acc_scr[hh] / l_fin[hh]
        o_scr[:, hh * HEAD_DIM:(hh + 1) * HEAD_DIM] = o_t.T.astype(BF16)

    mix = jnp.dot(o_scr[...], wo_ref[...], preferred_element_type=F32)
    o_ref[0] = h_ref[0] + _rms(mix, g1_ref[...])


def _moba_layer(q, k5, vt5, kmean, h2, g1, wo):
    B, S, D = h2.shape
    H, Dh, blk = N_HEADS, HEAD_DIM, MOBA_BLOCK
    nb = S // blk
    kern = functools.partial(_moba_kernel, nb=nb, k_sel=min(MOBA_TOPK, nb))
    return pl.pallas_call(
        kern,
        grid=(B, nb),
        in_specs=[
            pl.BlockSpec((1, H, blk, Dh), lambda b, j: (b, 0, j, 0)),
            pl.BlockSpec((1, H, nb, blk, Dh), lambda b, j: (b, 0, 0, 0, 0)),
            pl.BlockSpec((1, H, nb, Dh, blk), lambda b, j: (b, 0, 0, 0, 0)),
            pl.BlockSpec((1, H, nb, Dh), lambda b, j: (b, 0, 0, 0)),
            pl.BlockSpec((1, blk, D), lambda b, j: (b, j, 0)),
            _const_spec((1, D)), _const_spec((D, D)),
        ],
        out_specs=pl.BlockSpec((1, blk, D), lambda b, j: (b, j, 0)),
        out_shape=jax.ShapeDtypeStruct((B, S, D), F32),
        scratch_shapes=[
            pltpu.VMEM((H, nb, blk), F32), pltpu.VMEM((H, nb, blk, blk), F32),
            pltpu.VMEM((H, Dh, blk), F32), pltpu.VMEM((blk, D), BF16),
        ],
        compiler_params=_cparams("parallel", "arbitrary"),
        name="moba_attn",
    )(q, k5, vt5, kmean, h2, g1, wo)


def _router_kernel(h_ref, g2_ref, wrt_ref, br_ref, f_ref, e_ref, w_ref, wcol_ref):
    f = _rms(h_ref[...], g2_ref[...])
    f_ref[...] = f
    logit = lax.dot_general(wrt_ref[...], f, (((1,), (1,)), ((), ())),
                            precision=lax.Precision.HIGHEST, preferred_element_type=F32) + br_ref[...]
    ne, t = logit.shape
    eidx = lax.broadcasted_iota(jnp.int32, (ne, t), 0)
    m1 = jnp.max(logit, axis=0, keepdims=True)
    e1 = jnp.min(jnp.where(logit == m1, eidx, ne), axis=0, keepdims=True)
    rest = jnp.where(eidx == e1, -jnp.inf, logit)
    m2 = jnp.max(rest, axis=0, keepdims=True)
    e2 = jnp.min(jnp.where(rest == m2, eidx, ne), axis=0, keepdims=True)
    ex = jnp.exp(m2 - m1)
    den = 1.0 + ex
    w1 = 1.0 / den
    w2 = ex / den
    e_ref[0] = jnp.concatenate([e1, e2], axis=0)
    w_ref[0] = jnp.concatenate([w1, w2], axis=0)
    wpad = jnp.concatenate([w1, w2, jnp.zeros((LANES - 2, t), F32)], axis=0)
    wcol_ref[...] = wpad.T


def _router(h3, g2, wrt, br):
    N, D = h3.shape
    E = wrt.shape[0]
    T = ROW_TILE
    nt = N // T
    return pl.pallas_call(
        _router_kernel,
        grid=(nt,),
        in_specs=[
            pl.BlockSpec((T, D), lambda i: (i, 0)),
            _const_spec((1, D)), _const_spec((E, D)), _const_spec((E, 1)),
        ],
        out_specs=[
            pl.BlockSpec((T, D), lambda i: (i, 0)),
            pl.BlockSpec((1, 2, T), lambda i: (i, 0, 0)),
            pl.BlockSpec((1, 2, T), lambda i: (i, 0, 0)),
            pl.BlockSpec((T, LANES), lambda i: (i, 0)),
        ],
        out_shape=[
            jax.ShapeDtypeStruct((N, D), F32),
            jax.ShapeDtypeStruct((nt, 2, T), jnp.int32),
            jax.ShapeDtypeStruct((nt, 2, T), F32),
            jax.ShapeDtypeStruct((N, LANES), F32),
        ],
        compiler_params=_cparams("parallel"),
        name="moe_router",
    )(h3, g2, wrt, br)


def _rank_kernel(e_ref, dest_ref, blke_ref, nused_ref, *, ne, rows_log2, nblk_pad):
    ntile, _, t = e_ref.shape
    eidx = lax.broadcasted_iota(jnp.int32, (ne, t), 0)

    def onehots(i):
        e = e_ref[i]
        oh0 = jnp.where(eidx == e[0:1, :], 1.0, 0.0)
        oh1 = jnp.where(eidx == e[1:2, :], 1.0, 0.0)
        return oh0, oh1

    def count(i, cnt):
        oh0, oh1 = onehots(i)
        return cnt + jnp.sum(oh0 + oh1, axis=1, keepdims=True).astype(jnp.int32)

    cnt = lax.fori_loop(0, ntile, count, jnp.zeros((ne, 1), jnp.int32))
    pcnt = ((cnt + ((1 << rows_log2) - 1)) >> rows_log2) << rows_log2
    run = jnp.zeros((1, 1), jnp.int32)
    starts, ends = [], []
    for e in range(ne):
        starts.append(run)
        run = run + pcnt[e:e + 1, :]
        ends.append(run)
    pstart = jnp.concatenate(starts, axis=0)
    pend = jnp.concatenate(ends, axis=0)

    upper = jnp.where(lax.broadcasted_iota(jnp.int32, (t, t), 0) < lax.broadcasted_iota(jnp.int32, (t, t), 1),
                      1.0, 0.0).astype(BF16)

    def place(i, base):
        oh0, oh1 = onehots(i)
        both = oh0 + oh1
        before = jnp.dot(both.astype(BF16), upper, preferred_element_type=F32).astype(jnp.int32)
        slot = base + before
        d0 = jnp.sum(jnp.where(oh0 > 0.5, slot, 0), axis=0, keepdims=True)
        d1 = jnp.sum(jnp.where(oh1 > 0.5, slot, 0), axis=0, keepdims=True)
        dest_ref[i] = jnp.concatenate([d0, d1], axis=0)
        return base + jnp.sum(both, axis=1, keepdims=True).astype(jnp.int32)

    lax.fori_loop(0, ntile, place, pstart)

    blk_row = lax.broadcasted_iota(jnp.int32, (ne, nblk_pad), 1) << rows_log2
    blke = jnp.sum(jnp.where(pend <= blk_row, 1, 0), axis=0, keepdims=True)
    blke_ref[...] = jnp.minimum(blke, ne - 1)
    nused_ref[...] = jnp.broadcast_to(pend[ne - 1:ne, :] >> rows_log2, (1, LANES))


def _rank(e3, ne, nblk_pad):
    ntile, _, t = e3.shape
    kern = functools.partial(_rank_kernel, ne=ne, rows_log2=int(math.log2(MOE_ROWS)), nblk_pad=nblk_pad)
    return pl.pallas_call(
        kern,
        out_shape=[
            jax.ShapeDtypeStruct((ntile, 2, t), jnp.int32),
            jax.ShapeDtypeStruct((1, nblk_pad), jnp.int32),
            jax.ShapeDtypeStruct((1, LANES), jnp.int32),
        ],
        compiler_params=pltpu.CompilerParams(vmem_limit_bytes=VMEM_LIMIT),
        name="moe_rank",
    )(e3)


def _dispatch_kernel(dest_ref, f_ref, xs_in_ref, xs_ref, sem, *, td):
    del xs_in_ref

    def issue(t, _):
        for s in range(2):
            d = dest_ref[0, s, t]
            pltpu.make_async_copy(f_ref.at[pl.ds(t, 1), :], xs_ref.at[pl.ds(d, 1), :], sem).start()
        return 0

    lax.fori_loop(0, td, issue, 0, unroll=8)
    for s in range(2):
        pltpu.make_async_copy(f_ref, xs_ref.at[pl.ds(0, td), :], sem).wait()


def _dispatch(dest3, f, total):
    N, D = f.shape
    td = DISPATCH_TILE
    kern = functools.partial(_dispatch_kernel, td=td)
    return pl.pallas_call(
        kern,
        grid=(N // td,),
        in_specs=[
            pl.BlockSpec((1, 2, td), lambda i: (i, 0, 0), memory_space=pltpu.SMEM),
            pl.BlockSpec((td, D), lambda i: (i, 0)),
            pl.BlockSpec(memory_space=pl.ANY),
        ],
        out_specs=pl.BlockSpec(memory_space=pl.ANY),
        out_shape=jax.ShapeDtypeStruct((total, D), F32),
        scratch_shapes=[pltpu.SemaphoreType.DMA(())],
        input_output_aliases={2: 0},
        compiler_params=_cparams("arbitrary"),
        name="moe_dispatch",
    )(dest3, f, jnp.zeros((total, D), F32))


def _expert_kernel(blke_ref, nused_ref, xs_ref, w1_ref, w3_ref, w2_ref, y_ref):
    i = pl.program_id(0)

    @pl.when(i < nused_ref[0])
    def _():
        x = xs_ref[...].astype(BF16)
        a = jnp.dot(x, w1_ref[0], preferred_element_type=F32)
        b = jnp.dot(x, w3_ref[0], preferred_element_type=F32)
        act = (jax.nn.silu(a) * b).astype(BF16)
        y_ref[...] = jnp.dot(act, w2_ref[0], preferred_element_type=F32)

    @pl.when(i >= nused_ref[0])
    def _():
        y_ref[...] = jnp.zeros_like(y_ref)


def _experts(blke, nused, xs, w1, w3, w2):
    total, D = xs.shape
    E, _, Fe = w1.shape
    rb = MOE_ROWS
    nblk = total // rb
    wspec = lambda shape: pl.BlockSpec(shape, lambda i, be, nu: (be[i], 0, 0), pipeline_mode=pl.Buffered(1))
    return pl.pallas_call(
        _expert_kernel,
        grid_spec=pltpu.PrefetchScalarGridSpec(
            num_scalar_prefetch=2,
            grid=(nblk,),
            in_specs=[
                pl.BlockSpec((rb, D), lambda i, be, nu: (i, 0)),
                wspec((1, D, Fe)), wspec((1, D, Fe)), wspec((1, Fe, D)),
            ],
            out_specs=pl.BlockSpec((rb, D), lambda i, be, nu: (i, 0)),
        ),
        out_shape=jax.ShapeDtypeStruct((total, D), F32),
        compiler_params=_cparams("arbitrary"),
        name="moe_experts",
    )(blke, nused, xs, w1, w3, w2)


def _combine_kernel(dest_ref, yb_ref, wcol_ref, h_ref, g3_ref, o_ref, buf, sem, *, td):
    def issue(t, _):
        for s in range(2):
            d = dest_ref[0, s, t]
            pltpu.make_async_copy(yb_ref.at[pl.ds(d, 1), :], buf.at[s, pl.ds(t, 1), :], sem).start()
        return 0

    lax.fori_loop(0, td, issue, 0, unroll=8)
    for s in range(2):
        pltpu.make_async_copy(yb_ref.at[pl.ds(0, td), :], buf.at[s], sem).wait()
    w = wcol_ref[...]
    f = buf[0] * w[:, 0:1] + buf[1] * w[:, 1:2]
    o_ref[...] = h_ref[...] + _rms(f, g3_ref[...])


def _combine(dest3, yb, wcol, h3, g3):
    N, D = h3.shape
    td = DISPATCH_TILE
    kern = functools.partial(_combine_kernel, td=td)
    return pl.pallas_call(
        kern,
        grid=(N // td,),
        in_specs=[
            pl.BlockSpec((1, 2, td), lambda i: (i, 0, 0), memory_space=pltpu.SMEM),
            pl.BlockSpec(memory_space=pl.ANY),
            pl.BlockSpec((td, LANES), lambda i: (i, 0)),
            pl.BlockSpec((td, D), lambda i: (i, 0)),
            _const_spec((1, D)),
        ],
        out_specs=pl.BlockSpec((td, D), lambda i: (i, 0)),
        out_shape=jax.ShapeDtypeStruct((N, D), F32),
        scratch_shapes=[pltpu.VMEM((2, td, D), F32), pltpu.SemaphoreType.DMA(())],
        compiler_params=_cparams("arbitrary"),
        name="moe_combine",
    )(dest3, yb, wcol, h3, g3)


def _rope_lane_tables():
    half = ROT_DIM // 2
    inv_freq = 1.0 / (ROPE_THETA ** (np.arange(half, dtype=np.float32) / half))
    invf = np.zeros((1, LANES), np.float32)
    invf[0, :half] = inv_freq
    invf[0, half:ROT_DIM] = inv_freq
    sneg = np.zeros((1, LANES), np.float32)
    sneg[0, :half] = -1.0
    spos = np.zeros((1, LANES), np.float32)
    spos[0, half:ROT_DIM] = 1.0
    return jnp.asarray(invf), jnp.asarray(sneg), jnp.asarray(spos)


def kernel(x, positions, norms, s5_lam_re, s5_lam_im, s5_log_dt, s5_b_re, s5_b_im, s5_c_re, s5_c_im, s5_d, s5_w_glu, kv_norm, w_kv, w_q, w_o, ffn_w1, ffn_w3, ffn_w2, moe_router, moe_bias, moe_w1, moe_w3, moe_w2):
    B, S, D = x.shape
    N = B * S
    HD = N_HEADS * HEAD_DIM
    E = moe_router.shape[-1]
    row = lambda v: v.astype(F32).reshape(1, -1)

    a_r, a_i, bb_r, bb_i = _s5_prep(s5_lam_re[0], s5_lam_im[0], s5_log_dt[0], s5_b_re[0], s5_b_im[0])
    a_r_t, a_i_t, b_bd, c_bd = _s5_block_diag(a_r, a_i, bb_r, bb_i, s5_c_re[0], s5_c_im[0])
    h1 = _s5_layer(x, row(norms[0, 0]), row(norms[0, 1]), row(s5_d[0]), a_r_t, a_i_t, b_bd, c_bd,
                   s5_w_glu[0].astype(BF16))
    h2 = _dense_ffn(h1.reshape(N, D), row(norms[0, 2]), row(norms[0, 3]),
                    ffn_w1[0].astype(BF16), ffn_w3[0].astype(BF16), ffn_w2[0].astype(BF16))

    invf, sneg, spos = _rope_lane_tables()
    posb = jnp.broadcast_to(positions.astype(F32)[:, :, None], (B, S, LANES))
    q, k5, vt5, kmean = _qkv(h2.reshape(B, S, D), posb, invf, sneg, spos, row(norms[1, 0]), row(kv_norm),
                             w_q[0].astype(BF16), w_kv[:, :HD].astype(BF16), w_kv[:, HD:].T.astype(BF16))
    h3 = _moba_layer(q, k5, vt5, kmean, h2.reshape(B, S, D), row(norms[1, 1]), w_o[0].astype(BF16))
    h3 = h3.reshape(N, D)

    f_in, e3, w3_, wcol = _router(h3, row(norms[1, 2]), moe_router[0].T.astype(F32),
                                  moe_bias[0].astype(F32).reshape(E, 1))
    del w3_
    total = 2 * N + E * MOE_ROWS
    nblk = total // MOE_ROWS
    nblk_pad = -(-nblk // LANES) * LANES
    rt = RANK_TILE
    e_rank = e3.transpose(1, 0, 2).reshape(2, N // rt, rt).transpose(1, 0, 2)
    dest, blke, nused = _rank(e_rank, E, nblk_pad)
    td = DISPATCH_TILE
    dest3 = dest.transpose(1, 0, 2).reshape(2, N // td, td).transpose(1, 0, 2)
    xs = _dispatch(dest3, f_in, total)
    yb = _experts(blke[0, :nblk], nused[0, :1], xs,
                  moe_w1[0].astype(BF16), moe_w3[0].astype(BF16), moe_w2[0].astype(BF16))
    out = _combine(dest3, yb, wcol, h3, row(norms[1, 3]))
    return out.reshape(B, S, D)
```

```python
import functools
import math

import numpy as np
import jax
import jax.numpy as jnp
from jax import lax
from jax.experimental import pallas as pl
from jax.experimental.pallas import tpu as pltpu

F32 = jnp.float32
BF16 = jnp.bfloat16

NORM_EPS = 1e-6
NEG_INF = -1e30

SSM_GROUP = 16
N_HEADS = 8
HEAD_DIM = 128
ROT_DIM = HEAD_DIM // 4
ROPE_THETA = 500000.0
MOBA_BLOCK = 256
MOBA_TOPK = 3
N_EXPERTS = 8

Q_SCALE_LOG2 = HEAD_DIM ** -0.5 * math.log2(math.e)

KMEAN_PIECES = 3

LANES = 128
GROUPS_PER_TILE = LANES // SSM_GROUP

VMEM_LIMIT = 56 * 1024 * 1024

S5_CHUNK = 32
S5_ROW_PAD = 8
ROW_TILE = 512
QKV_TILE = 512
MOE_ROWS = 512
DISPATCH_TILE = 256


def _cparams(*sem):
    return pltpu.CompilerParams(dimension_semantics=sem, vmem_limit_bytes=VMEM_LIMIT)


def _rms(x, g):
    return x * lax.rsqrt(jnp.mean(x * x, axis=-1, keepdims=True) + NORM_EPS) * g


def _const_spec(shape):
    n = len(shape)
    return pl.BlockSpec(shape, lambda *_: (0,) * n, pipeline_mode=pl.Buffered(1))


def _s5_prep_kernel(lr_ref, li_ref, ldt_ref, br_ref, bi_ref,
                    ar_ref, ai_ref, bbr_ref, bbi_ref):
    lr = lr_ref[...]
    li = li_ref[...]
    dt = jnp.exp(ldt_ref[...])
    zr = lr * dt
    zi = li * dt
    mag = jnp.exp(zr)
    a_r = mag * jnp.cos(zr * 0 + zi)
    a_i = mag * jnp.sin(zi)
    den = lr * lr + li * li
    coef_r = ((a_r - 1.0) * lr + a_i * li) / den
    coef_i = (a_i * lr - (a_r - 1.0) * li) / den
    br = br_ref[...]
    bi = bi_ref[...]
    ar_ref[...] = a_r
    ai_ref[...] = a_i
    bbr_ref[...] = coef_r * br - coef_i * bi
    bbi_ref[...] = coef_r * bi + coef_i * br


def _s5_prep(lam_re, lam_im, log_dt, b_re, b_im):
    G, P = lam_re.shape
    Q = b_re.shape[-1]
    rep = lambda a: jnp.repeat(a.astype(F32), Q, axis=1)
    out = jax.ShapeDtypeStruct((G, P * Q), F32)
    a_r, a_i, bb_r, bb_i = pl.pallas_call(
        _s5_prep_kernel, out_shape=(out, out, out, out), name="s5_prep",
    )(rep(lam_re), rep(lam_im), log_dt.astype(F32).reshape(G, 1),
      b_re.astype(F32).reshape(G, P * Q), b_im.astype(F32).reshape(G, P * Q))
    a_r = a_r.reshape(G, P, Q)[:, :, 0]
    a_i = a_i.reshape(G, P, Q)[:, :, 0]
    return a_r, a_i, bb_r.reshape(G, P, Q), bb_i.reshape(G, P, Q)


def _s5_block_diag(a_r, a_i, bb_r, bb_i, c_re, c_im):
    G, P, Q = bb_r.shape
    gt = GROUPS_PER_TILE
    nt = G // gt
    eye = jnp.eye(gt, dtype=F32)
    def bmat(bb):
        return jnp.einsum('kgpi,gh->kgihp', bb.reshape(nt, gt, P, Q), eye).reshape(nt, gt * Q, gt * P)
    def cmat(c):
        return jnp.einsum('kgip,gh->kgphi', c.astype(F32).reshape(nt, gt, Q, P), eye).reshape(nt, gt * P, gt * Q)
    b_bd = jnp.concatenate([bmat(bb_r), bmat(bb_i)], axis=-1).astype(BF16)
    c_bd = jnp.concatenate([cmat(c_re), -cmat(c_im)], axis=1).astype(BF16)
    a_r_t = a_r.reshape(nt, 1, gt * P)
    a_i_t = a_i.reshape(nt, 1, gt * P)
    return a_r_t, a_i_t, b_bd, c_bd


def _s5_layer_kernel(x_ref, g0_ref, g1_ref, d_ref, ar_ref, ai_ref, bbd_ref, cbd_ref, wglu_ref,
                     o_ref, bm, tm, bu, y_tm, state, *, nb, lc, d, nt, hw):
    c = pl.program_id(0)
    pitch = lc + S5_ROW_PAD

    @pl.when(c == 0)
    def _():
        state[...] = jnp.zeros_like(state)

    for b in range(nb):
        u = _rms(x_ref[b], g0_ref[...])
        for k in range(nt):
            bm[k, b * pitch:b * pitch + lc, :] = u[:, k * LANES:(k + 1) * LANES]
    for t in range(lc):
        for k in range(nt):
            tm[k, t * nb:(t + 1) * nb, :] = bm[k, pl.ds(t, nb, stride=pitch), :]

    for k in range(nt):
        uk = tm[k]
        bu[...] = jnp.dot(uk.astype(BF16), bbd_ref[k], preferred_element_type=F32)
        a_r = jnp.broadcast_to(ar_ref[k], (nb, hw))
        a_i = jnp.broadcast_to(ai_ref[k], (nb, hw))

        def step(t, carry):
            s_r, s_i = carry
            r0 = pl.multiple_of(t * nb, nb)
            n_r = a_r * s_r - a_i * s_i + bu[pl.ds(r0, nb), 0:hw]
            n_i = a_r * s_i + a_i * s_r + bu[pl.ds(r0, nb), hw:2 * hw]
            bu[pl.ds(r0, nb), 0:hw] = n_r
            bu[pl.ds(r0, nb), hw:2 * hw] = n_i
            return n_r, n_i

        s_r, s_i = lax.fori_loop(0, lc, step, (state[k, :, 0:hw], state[k, :, hw:2 * hw]), unroll=4)
        state[k, :, 0:hw] = s_r
        state[k, :, hw:2 * hw] = s_i
        y_tm[:, k * LANES:(k + 1) * LANES] = (
            jnp.dot(bu[...].astype(BF16), cbd_ref[k], preferred_element_type=F32)
            + d_ref[:, k * LANES:(k + 1) * LANES] * uk)

    gl = jax.nn.gelu(y_tm[...]).astype(BF16)
    z = jnp.dot(gl, wglu_ref[...], preferred_element_type=F32)
    mix = z[:, :d] * jax.nn.sigmoid(z[:, d:])
    mixn = _rms(mix, g1_ref[...])
    for k in range(nt):
        tm[k] = mixn[:, k * LANES:(k + 1) * LANES]
    for t in range(lc):
        for k in range(nt):
            bm[k, pl.ds(t, nb, stride=pitch), :] = tm[k, t * nb:(t + 1) * nb, :]
    for b in range(nb):
        for k in range(nt):
            sl = slice(k * LANES, (k + 1) * LANES)
            o_ref[b, :, sl] = x_ref[b, :, sl] + bm[k, b * pitch:b * pitch + lc, :]


def _s5_layer(x, g0, g1, d_skip, a_r_t, a_i_t, b_bd, c_bd, w_glu):
    B, S, D = x.shape
    lc = S5_CHUNK
    nt = D // LANES
    hw = a_r_t.shape[-1]
    rows = B * lc
    kern = functools.partial(_s5_layer_kernel, nb=B, lc=lc, d=D, nt=nt, hw=hw)
    return pl.pallas_call(
        kern,
        grid=(S // lc,),
        in_specs=[
            pl.BlockSpec((B, lc, D), lambda c: (0, c, 0)),
            _const_spec((1, D)), _const_spec((1, D)), _const_spec((1, D)),
            _const_spec((nt, 1, hw)), _const_spec((nt, 1, hw)),
            _const_spec((nt, LANES, 2 * hw)), _const_spec((nt, 2 * hw, LANES)),
            _const_spec((D, 2 * D)),
        ],
        out_specs=pl.BlockSpec((B, lc, D), lambda c: (0, c, 0)),
        out_shape=jax.ShapeDtypeStruct((B, S, D), F32),
        scratch_shapes=[
            pltpu.VMEM((nt, B * (lc + S5_ROW_PAD), LANES), F32), pltpu.VMEM((nt, rows, LANES), F32),
            pltpu.VMEM((rows, 2 * hw), F32), pltpu.VMEM((rows, D), F32),
            pltpu.VMEM((nt, B, 2 * hw), F32),
        ],
        compiler_params=_cparams("arbitrary"),
        name="s5_layer",
    )(x, g0, g1, d_skip, a_r_t, a_i_t, b_bd, c_bd, w_glu)


def _ffn_kernel(h_ref, g2_ref, g3_ref, w1_ref, w3_ref, w2_ref, o_ref):
    h = h_ref[...]
    f_in = _rms(h, g2_ref[...]).astype(BF16)
    a = jnp.dot(f_in, w1_ref[...], preferred_element_type=F32)
    b = jnp.dot(f_in, w3_ref[...], preferred_element_type=F32)
    act = (jax.nn.silu(a) * b).astype(BF16)
    f = jnp.dot(act, w2_ref[...], preferred_element_type=F32)
    o_ref[...] = h + _rms(f, g3_ref[...])


def _dense_ffn(h, g2, g3, w1, w3, w2):
    N, D = h.shape
    Fd = w1.shape[1]
    tm = ROW_TILE
    return pl.pallas_call(
        _ffn_kernel,
        grid=(N // tm,),
        in_specs=[
            pl.BlockSpec((tm, D), lambda i: (i, 0)),
            _const_spec((1, D)), _const_spec((1, D)),
            _const_spec((D, Fd)), _const_spec((D, Fd)), _const_spec((Fd, D)),
        ],
        out_specs=pl.BlockSpec((tm, D), lambda i: (i, 0)),
        out_shape=jax.ShapeDtypeStruct((N, D), F32),
        compiler_params=_cparams("parallel"),
        name="dense_ffn",
    )(h, g2, g3, w1, w3, w2)


def _rotary(xh, cs, s_neg, s_pos):
    half = ROT_DIM // 2
    return xh * cs + pltpu.roll(xh, LANES - half, 1) * s_neg + pltpu.roll(xh, half, 1) * s_pos


def _qkv_kernel(h_ref, pos_ref, invf_ref, sneg_ref, spos_ref, gq_ref, gkv_ref,
                wq_ref, wk_ref, wvt_ref, q_ref, k_ref, vt_ref, km_ref, *, nblk_tile, nblk_seq):
    ti = pl.program_id(1)
    h = h_ref[0]
    r = lax.rsqrt(jnp.mean(h * h, axis=-1, keepdims=True) + NORM_EPS)
    hq = (h * r * gq_ref[...]).astype(BF16)
    hk = (h * r * gkv_ref[...]).astype(BF16)
    ang = pos_ref[0] * invf_ref[...]
    cs = jnp.cos(ang)
    sn = jnp.sin(ang)
    s_neg = sn * sneg_ref[...]
    s_pos = sn * spos_ref[...]
    q = jnp.dot(hq, wq_ref[...], preferred_element_type=F32)
    k = jnp.dot(hk, wk_ref[...], preferred_element_type=F32)
    vt = lax.dot_general(wvt_ref[...], hk, (((1,), (1,)), ((), ())), preferred_element_type=F32)
    for hh in range(N_HEADS):
        sl = slice(hh * HEAD_DIM, (hh + 1) * HEAD_DIM)
        q_ref[0, hh] = (_rotary(q[:, sl], cs, s_neg, s_pos) * Q_SCALE_LOG2).astype(BF16)
        kr = _rotary(k[:, sl], cs, s_neg, s_pos)
        for j in range(nblk_tile):
            rs = slice(j * MOBA_BLOCK, (j + 1) * MOBA_BLOCK)
            k_ref[0, hh, j] = kr[rs].astype(BF16)
            vt_ref[0, hh, j] = vt[sl, rs].astype(BF16)
            km = jnp.mean(kr[rs], axis=0, keepdims=True)
            for piece in range(KMEAN_PIECES):
                part = km.astype(BF16).astype(F32)
                km_ref[0, hh, pl.ds(piece * nblk_seq + ti * nblk_tile + j, 1), :] = part
                km = km - part


def _qkv(h2, posb, invf, sneg, spos, gq, gkv, wq, wk, wvt):
    B, S, D = h2.shape
    T = QKV_TILE
    H, Dh, blk = N_HEADS, HEAD_DIM, MOBA_BLOCK
    nb = S // blk
    nbt = T // blk
    kern = functools.partial(_qkv_kernel, nblk_tile=nbt, nblk_seq=nb)
    return pl.pallas_call(
        kern,
        grid=(B, S // T),
        in_specs=[
            pl.BlockSpec((1, T, D), lambda b, t: (b, t, 0)),
            pl.BlockSpec((1, T, LANES), lambda b, t: (b, t, 0)),
            _const_spec((1, LANES)), _const_spec((1, LANES)), _const_spec((1, LANES)),
            _const_spec((1, D)), _const_spec((1, D)),
            _const_spec((D, D)), _const_spec((D, D)), _const_spec((D, D)),
        ],
        out_specs=[
            pl.BlockSpec((1, H, T, Dh), lambda b, t: (b, 0, t, 0)),
            pl.BlockSpec((1, H, nbt, blk, Dh), lambda b, t: (b, 0, t, 0, 0)),
            pl.BlockSpec((1, H, nbt, Dh, blk), lambda b, t: (b, 0, t, 0, 0)),
            pl.BlockSpec((1, H, KMEAN_PIECES * nb, Dh), lambda b, t: (b, 0, 0, 0)),
        ],
        out_shape=[
            jax.ShapeDtypeStruct((B, H, S, Dh), BF16),
            jax.ShapeDtypeStruct((B, H, nb, blk, Dh), BF16),
            jax.ShapeDtypeStruct((B, H, nb, Dh, blk), BF16),
            jax.ShapeDtypeStruct((B, H, KMEAN_PIECES * nb, Dh), F32),
        ],
        compiler_params=_cparams("parallel", "arbitrary"),
        name="qkv_proj",
    )(h2, posb, invf, sneg, spos, gq, gkv, wq, wk, wvt)


def _moba_kernel(q_ref, k_ref, vt_ref, km_ref, h_ref, g1_ref, wo_ref, o_ref,
                 sel_scr, st_scr, acc_scr, o_scr, *, nb, k_sel):
    j = pl.program_id(1)
    blk = MOBA_BLOCK
    nt_dims = (((1,), (1,)), ((), ()))
    blk_id = lax.broadcasted_iota(jnp.int32, (nb, blk), 0)
    past = blk_id < j
    causal = (lax.broadcasted_iota(jnp.int32, (blk, blk), 0)
              <= lax.broadcasted_iota(jnp.int32, (blk, blk), 1))

    for hh in range(N_HEADS):
        parts = lax.dot_general(km_ref[0, hh].astype(BF16), q_ref[0, hh], nt_dims,
                                preferred_element_type=F32)
        gate = parts[0:nb]
        for piece in range(1, KMEAN_PIECES):
            gate = gate + parts[piece * nb:(piece + 1) * nb]
        gate = jnp.where(past, gate, NEG_INF)
        rank = jnp.zeros((nb, blk), F32)
        for m in range(nb):
            gm = gate[m:m + 1, :]
            beats = jnp.where(gm > gate, 1.0, jnp.where(gm == gate, jnp.where(m < blk_id, 1.0, 0.0), 0.0))
            rank = rank + beats
        sel_scr[hh] = jnp.where(past, jnp.where(rank < k_sel, 1.0, 0.0), 0.0)

    def qk_pass(n, ms, own):
        out = []
        for hh in range(N_HEADS):
            st = lax.dot_general(k_ref[0, hh, n], q_ref[0, hh], nt_dims, preferred_element_type=F32)
            allowed = causal if own else sel_scr[hh, pl.ds(n, 1), :] > 0.5
            st = jnp.where(allowed, st, NEG_INF)
            st_scr[hh, n] = st
            out.append(jnp.maximum(ms[hh], jnp.max(st, axis=0, keepdims=True)))
        return tuple(out)

    m_init = tuple(jnp.full((1, blk), NEG_INF, F32) for _ in range(N_HEADS))
    m_own = qk_pass(j, m_init, True)
    m_fin = lax.fori_loop(0, j, lambda n, ms: qk_pass(n, ms, False), m_own)

    def pv_pass(n, ls, first):
        out = []
        for hh in range(N_HEADS):
            p = jnp.exp2(st_scr[hh, n] - m_fin[hh])
            out.append(ls[hh] + jnp.sum(p, axis=0, keepdims=True))
            pv = jnp.dot(vt_ref[0, hh, n], p.astype(BF16), preferred_element_type=F32)
            acc_scr[hh] = pv if first else acc_scr[hh] + pv
        return tuple(out)

    l_own = pv_pass(j, tuple(jnp.zeros((1, blk), F32) for _ in range(N_HEADS)), True)
    l_fin = lax.fori_loop(0, j, lambda n, ls: pv_pass(n, ls, False), l_own)
    for hh in range(N_HEADS):
        o_t = acc_scr[hh] / l_fin[hh]
        o_scr[:, hh * HEAD_DIM:(hh + 1) * HEAD_DIM] = o_t.T.astype(BF16)

    mix = jnp.dot(o_scr[...], wo_ref[...], preferred_element_type=F32)
    o_ref[0] = h_ref[0] + _rms(mix, g1_ref[...])


def _moba_layer(q, k5, vt5, kmean, h2, g1, wo):
    B, S, D = h2.shape
    H, Dh, blk = N_HEADS, HEAD_DIM, MOBA_BLOCK
    nb = S // blk
    kern = functools.partial(_moba_kernel, nb=nb, k_sel=min(MOBA_TOPK, nb))
    return pl.pallas_call(
        kern,
        grid=(B, nb),
        in_specs=[
            pl.BlockSpec((1, H, blk, Dh), lambda b, j: (b, 0, j, 0)),
            pl.BlockSpec((1, H, nb, blk, Dh), lambda b, j: (b, 0, 0, 0, 0)),
            pl.BlockSpec((1, H, nb, Dh, blk), lambda b, j: (b, 0, 0, 0, 0)),
            pl.BlockSpec((1, H, KMEAN_PIECES * nb, Dh), lambda b, j: (b, 0, 0, 0)),
            pl.BlockSpec((1, blk, D), lambda b, j: (b, j, 0)),
            _const_spec((1, D)), _const_spec((D, D)),
        ],
        out_specs=pl.BlockSpec((1, blk, D), lambda b, j: (b, j, 0)),
        out_shape=jax.ShapeDtypeStruct((B, S, D), F32),
        scratch_shapes=[
            pltpu.VMEM((H, nb, blk), F32), pltpu.VMEM((H, nb, blk, blk), F32),
            pltpu.VMEM((H, Dh, blk), F32), pltpu.VMEM((blk, D), BF16),
        ],
        compiler_params=_cparams("parallel", "arbitrary"),
        name="moba_attn",
    )(q, k5, vt5, kmean, h2, g1, wo)


def _router_kernel(h_ref, g2_ref, wrt_ref, br_ref, f_ref, e_ref, wcol_ref):
    f = _rms(h_ref[...], g2_ref[...])
    f_ref[...] = f.astype(BF16)
    logit = lax.dot_general(wrt_ref[...], f, (((1,), (1,)), ((), ())),
                            precision=lax.Precision.HIGHEST, preferred_element_type=F32) + br_ref[...]
    ne, t = logit.shape
    eidx = lax.broadcasted_iota(jnp.int32, (ne, t), 0)
    m1 = jnp.max(logit, axis=0, keepdims=True)
    e1 = jnp.min(jnp.where(logit == m1, eidx, ne), axis=0, keepdims=True)
    rest = jnp.where(eidx == e1, -jnp.inf, logit)
    m2 = jnp.max(rest, axis=0, keepdims=True)
    e2 = jnp.min(jnp.where(rest == m2, eidx, ne), axis=0, keepdims=True)
    ex = jnp.exp(m2 - m1)
    den = 1.0 + ex
    w1 = 1.0 / den
    w2 = ex / den
    e_ref[0] = jnp.concatenate([e1, e2], axis=0)
    wpad = jnp.concatenate([w1, w2, jnp.zeros((LANES - 2, t), F32)], axis=0)
    wcol_ref[...] = wpad.T


def _router(h3, g2, wrt, br):
    N, D = h3.shape
    E = wrt.shape[0]
    T = ROW_TILE
    nt = N // T
    return pl.pallas_call(
        _router_kernel,
        grid=(nt,),
        in_specs=[
            pl.BlockSpec((T, D), lambda i: (i, 0)),
            _const_spec((1, D)), _const_spec((E, D)), _const_spec((E, 1)),
        ],
        out_specs=[
            pl.BlockSpec((T, D), lambda i: (i, 0)),
            pl.BlockSpec((1, 2, T), lambda i: (i, 0, 0)),
            pl.BlockSpec((T, LANES), lambda i: (i, 0)),
        ],
        out_shape=[
            jax.ShapeDtypeStruct((N, D), BF16),
            jax.ShapeDtypeStruct((nt, 2, T), jnp.int32),
            jax.ShapeDtypeStruct((N, LANES), F32),
        ],
        compiler_params=_cparams("parallel"),
        name="moe_router",
    )(h3, g2, wrt, br)


META_ROWS = 32


def _rank_kernel(e_ref, lpos_ref, lcol_ref, meta_ref, pad_ref, blke_ref, nused_ref, *, ne, rows_log2, nblk_pad):
    ntile, _, t = e_ref.shape
    eidx = lax.broadcasted_iota(jnp.int32, (ne, t), 0)

    def onehots(i):
        e = e_ref[i]
        oh0 = jnp.where(eidx == e[0:1, :], 1.0, 0.0)
        oh1 = jnp.where(eidx == e[1:2, :], 1.0, 0.0)
        return oh0, oh1

    def excl_cumsum(col):
        run = jnp.zeros((1, 1), jnp.int32)
        parts = []
        for e in range(ne):
            parts.append(run)
            run = run + col[e:e + 1, :]
        return jnp.concatenate(parts, axis=0), run

    def count(i, cnt):
        oh0, oh1 = onehots(i)
        return cnt + jnp.sum(oh0 + oh1, axis=1, keepdims=True).astype(jnp.int32)

    cnt = lax.fori_loop(0, ntile, count, jnp.zeros((ne, 1), jnp.int32))
    pcnt = ((cnt + ((1 << rows_log2) - 1)) >> rows_log2) << rows_log2
    pstart, ptotal = excl_cumsum(pcnt)
    pend = pstart + pcnt

    upper = jnp.where(lax.broadcasted_iota(jnp.int32, (t, t), 0) < lax.broadcasted_iota(jnp.int32, (t, t), 1),
                      1.0, 0.0).astype(BF16)
    lane_b = lambda col: jnp.broadcast_to(col, (ne, LANES))

    def place(i, base):
        oh0, oh1 = onehots(i)
        both = oh0 + oh1
        before = jnp.dot(both.astype(BF16), upper, preferred_element_type=F32).astype(jnp.int32)
        cnt_i = jnp.sum(both, axis=1, keepdims=True).astype(jnp.int32)
        loff, _ = excl_cumsum(cnt_i)
        slot = loff + before
        l0 = jnp.sum(jnp.where(oh0 > 0.5, slot, 0), axis=0, keepdims=True)
        l1 = jnp.sum(jnp.where(oh1 > 0.5, slot, 0), axis=0, keepdims=True)
        lpos_ref[i] = jnp.concatenate([l0, l1], axis=0)
        lpad = jnp.concatenate([l0.astype(F32), l1.astype(F32), jnp.zeros((LANES - 2, t), F32)], axis=0)
        lcol_ref[i] = lpad.T
        meta_ref[i] = jnp.concatenate(
            [lane_b(cnt_i), lane_b(loff), lane_b(pstart + base),
             jnp.zeros((META_ROWS - 3 * ne, LANES), jnp.int32)], axis=0)
        return base + cnt_i

    lax.fori_loop(0, ntile, place, jnp.zeros((ne, 1), jnp.int32))

    pad_ref[...] = jnp.concatenate([lane_b(pstart + cnt), lane_b(pcnt - cnt),
                                    jnp.broadcast_to(ptotal, (ne, LANES))], axis=0)
    blk_row = lax.broadcasted_iota(jnp.int32, (ne, nblk_pad), 1) << rows_log2
    blke = jnp.sum(jnp.where(pend <= blk_row, 1, 0), axis=0, keepdims=True)
    blke_ref[...] = jnp.minimum(blke, ne - 1)
    nused_ref[...] = jnp.broadcast_to(ptotal >> rows_log2, (1, LANES))


def _rank(e3, ne, nblk_pad):
    ntile, _, t = e3.shape
    kern = functools.partial(_rank_kernel, ne=ne, rows_log2=int(math.log2(MOE_ROWS)), nblk_pad=nblk_pad)
    return pl.pallas_call(
        kern,
        out_shape=[
            jax.ShapeDtypeStruct((ntile, 2, t), jnp.int32),
            jax.ShapeDtypeStruct((ntile, t, LANES), F32),
            jax.ShapeDtypeStruct((ntile, META_ROWS, LANES), jnp.int32),
            jax.ShapeDtypeStruct((3 * ne, LANES), jnp.int32),
            jax.ShapeDtypeStruct((1, nblk_pad), jnp.int32),
            jax.ShapeDtypeStruct((1, LANES), jnp.int32),
        ],
        compiler_params=pltpu.CompilerParams(vmem_limit_bytes=VMEM_LIMIT),
        name="moe_rank",
    )(e3)


SUBROWS = 8


def _rows_to_interleaved(ref, x):
    n = x.shape[0]
    for s in range(SUBROWS):
        ref[pl.ds(s, n, stride=SUBROWS), :] = x[:, s * LANES:(s + 1) * LANES]


def _interleaved_to_rows(ref, n):
    return jnp.concatenate([ref[pl.ds(s, n, stride=SUBROWS), :] for s in range(SUBROWS)], axis=1)


def _segment_copies(count, src_ref, src_row, dst_ref, dst_row, sem, max_piece, wait=False, advance_src=True):
    p = max_piece
    while p >= 1:
        hit = (count & p) != 0

        @pl.when(hit)
        def _(p=p, src_row=src_row, dst_row=dst_row):
            cp = pltpu.make_async_copy(
                src_ref.at[pl.ds(pl.multiple_of(src_row * SUBROWS, SUBROWS), p * SUBROWS), :],
                dst_ref.at[pl.ds(pl.multiple_of(dst_row * SUBROWS, SUBROWS), p * SUBROWS), :], sem)
            if wait:
                cp.wait()
            else:
                cp.start()

        step = count & p
        if advance_src:
            src_row = src_row + step
        dst_row = dst_row + step
        p //= 2


def _dispatch_kernel(meta_ref, pad_ref, lpos_ref, f_ref, xs_ref, srt, zbuf, sem, zsem, *, td, ne):
    i = pl.program_id(0)

    @pl.when(i == 0)
    def _():
        zbuf[...] = jnp.zeros_like(zbuf)
        zrows = zbuf.shape[0] // SUBROWS
        total = xs_ref.shape[0] // SUBROWS
        for wait in (False, True):
            for e in range(ne):
                _segment_copies(pad_ref[ne + e], zbuf, 0, xs_ref, pad_ref[e], zsem,
                                zrows, wait=wait, advance_src=False)
            for r in range(0, ne * MOE_ROWS, zrows):
                row = pad_ref[2 * ne] + r

                @pl.when(row < total)
                def _(row=row):
                    cp = pltpu.make_async_copy(
                        zbuf, xs_ref.at[pl.ds(pl.multiple_of(row * SUBROWS, SUBROWS), zrows * SUBROWS), :], zsem)
                    if wait:
                        cp.wait()
                    else:
                        cp.start()

    lp = lpos_ref[0]
    r = lax.broadcasted_iota(jnp.int32, (2 * td, td), 0)
    perm = (jnp.where(r == lp[0:1, :], 1.0, 0.0) + jnp.where(r == lp[1:2, :], 1.0, 0.0)).astype(BF16)
    _rows_to_interleaved(srt, jnp.dot(perm, f_ref[...], preferred_element_type=F32))
    for e in range(ne):
        _segment_copies(meta_ref[0, 0, e], srt, meta_ref[0, 0, ne + e], xs_ref, meta_ref[0, 0, 2 * ne + e],
                        sem, td)
    for h in range(2):
        pltpu.make_async_copy(srt.at[pl.ds(h * td * SUBROWS, td * SUBROWS), :],
                              xs_ref.at[pl.ds(0, td * SUBROWS), :], sem).wait()


def _dispatch(meta, pad, lpos, f, total):
    N, D = f.shape
    ntile, _, td = lpos.shape
    ne = N_EXPERTS
    assert D == SUBROWS * LANES
    kern = functools.partial(_dispatch_kernel, td=td, ne=ne)
    return pl.pallas_call(
        kern,
        grid=(ntile,),
        in_specs=[
            pl.BlockSpec((1, 1, META_ROWS), lambda i: (i, 0, 0), memory_space=pltpu.SMEM),
            pl.BlockSpec(memory_space=pltpu.SMEM),
            pl.BlockSpec((1, 2, td), lambda i: (i, 0, 0)),
            pl.BlockSpec((td, D), lambda i: (i, 0)),
        ],
        out_specs=pl.BlockSpec(memory_space=pl.ANY),
        out_shape=jax.ShapeDtypeStruct((total * SUBROWS, LANES), F32),
        scratch_shapes=[pltpu.VMEM((2 * td * SUBROWS, LANES), F32),
                        pltpu.VMEM((MOE_ROWS // 2 * SUBROWS, LANES), F32),
                        pltpu.SemaphoreType.DMA(()), pltpu.SemaphoreType.DMA(())],
        compiler_params=_cparams("arbitrary"),
        name="moe_dispatch",
    )(meta, pad, lpos, f)


def _expert_kernel(blke_ref, nused_ref, xs_ref, w1_ref, w3_ref, w2_ref, y_ref, *, rb):
    i = pl.program_id(0)

    @pl.when(i < nused_ref[0])
    def _():
        x = _interleaved_to_rows(xs_ref, rb).astype(BF16)
        a = jnp.dot(x, w1_ref[0], preferred_element_type=F32)
        b = jnp.dot(x, w3_ref[0], preferred_element_type=F32)
        act = (jax.nn.silu(a) * b).astype(BF16)
        _rows_to_interleaved(y_ref, jnp.dot(act, w2_ref[0], preferred_element_type=F32))

    @pl.when(i >= nused_ref[0])
    def _():
        y_ref[...] = jnp.zeros_like(y_ref)


def _experts(blke, nused, xs, w1, w3, w2):
    E, D, Fe = w1.shape
    total = xs.shape[0] // SUBROWS
    rb = MOE_ROWS
    nblk = total // rb
    wspec = lambda shape: pl.BlockSpec(shape, lambda i, be, nu: (be[i], 0, 0), pipeline_mode=pl.Buffered(1))
    return pl.pallas_call(
        functools.partial(_expert_kernel, rb=rb),
        grid_spec=pltpu.PrefetchScalarGridSpec(
            num_scalar_prefetch=2,
            grid=(nblk,),
            in_specs=[
                pl.BlockSpec((rb * SUBROWS, LANES), lambda i, be, nu: (jnp.minimum(i, nu[0] - 1), 0)),
                wspec((1, D, Fe)), wspec((1, D, Fe)), wspec((1, Fe, D)),
            ],
            out_specs=pl.BlockSpec((rb * SUBROWS, LANES), lambda i, be, nu: (i, 0)),
        ),
        out_shape=jax.ShapeDtypeStruct((total * SUBROWS, LANES), F32),
        compiler_params=_cparams("arbitrary"),
        name="moe_experts",
    )(blke, nused, xs, w1, w3, w2)


def _combine_kernel(meta_ref, yb_ref, lcol_ref, wcol_ref, h_ref, g3_ref, o_ref, buf, sem, *, td, ne):
    for e in range(ne):
        _segment_copies(meta_ref[0, 0, e], yb_ref, meta_ref[0, 0, 2 * ne + e], buf, meta_ref[0, 0, ne + e],
                        sem, td)
    for h in range(2):
        pltpu.make_async_copy(yb_ref.at[pl.ds(0, td * SUBROWS), :],
                              buf.at[pl.ds(h * td * SUBROWS, td * SUBROWS), :], sem).wait()
    yb = _interleaved_to_rows(buf, 2 * td).astype(BF16)
    lc = lcol_ref[0]
    w = wcol_ref[...]
    c = lax.broadcasted_iota(jnp.int32, (td, 2 * td), 1).astype(F32)
    pick0 = jnp.where(c == lc[:, 0:1], 1.0, 0.0).astype(BF16)
    pick1 = jnp.where(c == lc[:, 1:2], 1.0, 0.0).astype(BF16)
    f = (w[:, 0:1] * jnp.dot(pick0, yb, preferred_element_type=F32)
         + w[:, 1:2] * jnp.dot(pick1, yb, preferred_element_type=F32))
    o_ref[...] = h_ref[...] + _rms(f, g3_ref[...])


def _combine(meta, yb, lcol, wcol, h3, g3):
    N, D = h3.shape
    ntile, td, _ = lcol.shape
    ne = N_EXPERTS
    kern = functools.partial(_combine_kernel, td=td, ne=ne)
    return pl.pallas_call(
        kern,
        grid=(ntile,),
        in_specs=[
            pl.BlockSpec((1, 1, META_ROWS), lambda i: (i, 0, 0), memory_space=pltpu.SMEM),
            pl.BlockSpec(memory_space=pl.ANY),
            pl.BlockSpec((1, td, LANES), lambda i: (i, 0, 0)),
            pl.BlockSpec((td, LANES), lambda i: (i, 0)),
            pl.BlockSpec((td, D), lambda i: (i, 0)),
            _const_spec((1, D)),
        ],
        out_specs=pl.BlockSpec((td, D), lambda i: (i, 0)),
        out_shape=jax.ShapeDtypeStruct((N, D), F32),
        scratch_shapes=[pltpu.VMEM((2 * td * SUBROWS, LANES), F32), pltpu.SemaphoreType.DMA(())],
        compiler_params=_cparams("arbitrary"),
        name="moe_combine",
    )(meta, yb, lcol, wcol, h3, g3)


def _rope_lane_tables():
    half = ROT_DIM // 2
    inv_freq = 1.0 / (ROPE_THETA ** (np.arange(half, dtype=np.float32) / half))
    invf = np.zeros((1, LANES), np.float32)
    invf[0, :half] = inv_freq
    invf[0, half:ROT_DIM] = inv_freq
    sneg = np.zeros((1, LANES), np.float32)
    sneg[0, :half] = -1.0
    spos = np.zeros((1, LANES), np.float32)
    spos[0, half:ROT_DIM] = 1.0
    return jnp.asarray(invf), jnp.asarray(sneg), jnp.asarray(spos)


def kernel(x, positions, norms, s5_lam_re, s5_lam_im, s5_log_dt, s5_b_re, s5_b_im, s5_c_re, s5_c_im, s5_d, s5_w_glu, kv_norm, w_kv, w_q, w_o, ffn_w1, ffn_w3, ffn_w2, moe_router, moe_bias, moe_w1, moe_w3, moe_w2):
    B, S, D = x.shape
    N = B * S
    HD = N_HEADS * HEAD_DIM
    E = moe_router.shape[-1]
    row = lambda v: v.astype(F32).reshape(1, -1)

    a_r, a_i, bb_r, bb_i = _s5_prep(s5_lam_re[0], s5_lam_im[0], s5_log_dt[0], s5_b_re[0], s5_b_im[0])
    a_r_t, a_i_t, b_bd, c_bd = _s5_block_diag(a_r, a_i, bb_r, bb_i, s5_c_re[0], s5_c_im[0])
    h1 = _s5_layer(x, row(norms[0, 0]), row(norms[0, 1]), row(s5_d[0]), a_r_t, a_i_t, b_bd, c_bd,
                   s5_w_glu[0].astype(BF16))
    h2 = _dense_ffn(h1.reshape(N, D), row(norms[0, 2]), row(norms[0, 3]),
                    ffn_w1[0].astype(BF16), ffn_w3[0].astype(BF16), ffn_w2[0].astype(BF16))

    invf, sneg, spos = _rope_lane_tables()
    posb = jnp.broadcast_to(positions.astype(F32)[:, :, None], (B, S, LANES))
    q, k5, vt5, kmean = _qkv(h2.reshape(B, S, D), posb, invf, sneg, spos, row(norms[1, 0]), row(kv_norm),
                             w_q[0].astype(BF16), w_kv[:, :HD].astype(BF16), w_kv[:, HD:].T.astype(BF16))
    h3 = _moba_layer(q, k5, vt5, kmean, h2.reshape(B, S, D), row(norms[1, 1]), w_o[0].astype(BF16))
    h3 = h3.reshape(N, D)

    f_in, e3, wcol = _router(h3, row(norms[1, 2]), moe_router[0].T.astype(F32),
                             moe_bias[0].astype(F32).reshape(E, 1))
    total = 2 * N + E * MOE_ROWS
    nblk = total // MOE_ROWS
    nblk_pad = -(-nblk // LANES) * LANES
    td = DISPATCH_TILE
    e_tiles = e3.transpose(1, 0, 2).reshape(2, N // td, td).transpose(1, 0, 2)
    lpos, lcol, meta, pad, blke, nused = _rank(e_tiles, E, nblk_pad)
    meta = meta[:, :, 0].reshape(N // td, 1, META_ROWS)
    xs = _dispatch(meta, pad[:, 0], lpos, f_in, total)
    yb = _experts(blke[0, :nblk], nused[0, :1], xs,
                  moe_w1[0].astype(BF16), moe_w3[0].astype(BF16), moe_w2[0].astype(BF16))
    out = _combine(meta, yb, lcol, wcol, h3, row(norms[1, 3]))
    return out.reshape(B, S, D)
```

```python
import functools
import math

import numpy as np
import jax
import jax.numpy as jnp
from jax import lax
from jax.experimental import pallas as pl
from jax.experimental.pallas import tpu as pltpu

F32 = jnp.float32
BF16 = jnp.bfloat16

NORM_EPS = 1e-6
NEG_INF = -1e30

SSM_GROUP = 16
N_HEADS = 8
HEAD_DIM = 128
ROT_DIM = HEAD_DIM // 4
ROPE_THETA = 500000.0
MOBA_BLOCK = 256
MOBA_TOPK = 3
N_EXPERTS = 8

Q_SCALE_LOG2 = HEAD_DIM ** -0.5 * math.log2(math.e)

KMEAN_PIECES = 3

LANES = 128
GROUPS_PER_TILE = LANES // SSM_GROUP

VMEM_LIMIT = 56 * 1024 * 1024

S5_CHUNK = 32
S5_ROW_PAD = 8
ROW_TILE = 512
QKV_TILE = 512
MOE_ROWS = 512
DISPATCH_TILE = 256


def _cparams(*sem):
    return pltpu.CompilerParams(dimension_semantics=sem, vmem_limit_bytes=VMEM_LIMIT)


def _rms(x, g):
    return x * lax.rsqrt(jnp.mean(x * x, axis=-1, keepdims=True) + NORM_EPS) * g


def _const_spec(shape):
    n = len(shape)
    return pl.BlockSpec(shape, lambda *_: (0,) * n, pipeline_mode=pl.Buffered(1))


def _s5_prep_kernel(lr_ref, li_ref, ldt_ref, br_ref, bi_ref,
                    ar_ref, ai_ref, bbr_ref, bbi_ref):
    lr = lr_ref[...]
    li = li_ref[...]
    dt = jnp.exp(ldt_ref[...])
    zr = lr * dt
    zi = li * dt
    mag = jnp.exp(zr)
    a_r = mag * jnp.cos(zr * 0 + zi)
    a_i = mag * jnp.sin(zi)
    den = lr * lr + li * li
    coef_r = ((a_r - 1.0) * lr + a_i * li) / den
    coef_i = (a_i * lr - (a_r - 1.0) * li) / den
    br = br_ref[...]
    bi = bi_ref[...]
    ar_ref[...] = a_r
    ai_ref[...] = a_i
    bbr_ref[...] = coef_r * br - coef_i * bi
    bbi_ref[...] = coef_r * bi + coef_i * br


def _s5_prep(lam_re, lam_im, log_dt, b_re, b_im):
    G, P = lam_re.shape
    Q = b_re.shape[-1]
    rep = lambda a: jnp.repeat(a.astype(F32), Q, axis=1)
    out = jax.ShapeDtypeStruct((G, P * Q), F32)
    a_r, a_i, bb_r, bb_i = pl.pallas_call(
        _s5_prep_kernel, out_shape=(out, out, out, out), name="s5_prep",
    )(rep(lam_re), rep(lam_im), log_dt.astype(F32).reshape(G, 1),
      b_re.astype(F32).reshape(G, P * Q), b_im.astype(F32).reshape(G, P * Q))
    a_r = a_r.reshape(G, P, Q)[:, :, 0]
    a_i = a_i.reshape(G, P, Q)[:, :, 0]
    return a_r, a_i, bb_r.reshape(G, P, Q), bb_i.reshape(G, P, Q)


def _s5_block_diag(a_r, a_i, bb_r, bb_i, c_re, c_im):
    G, P, Q = bb_r.shape
    gt = GROUPS_PER_TILE
    nt = G // gt
    eye = jnp.eye(gt, dtype=F32)
    def bmat(bb):
        return jnp.einsum('kgpi,gh->kgihp', bb.reshape(nt, gt, P, Q), eye).reshape(nt, gt * Q, gt * P)
    def cmat(c):
        return jnp.einsum('kgip,gh->kgphi', c.astype(F32).reshape(nt, gt, Q, P), eye).reshape(nt, gt * P, gt * Q)
    b_bd = jnp.concatenate([bmat(bb_r), bmat(bb_i)], axis=-1).astype(BF16)
    c_bd = jnp.concatenate([cmat(c_re), -cmat(c_im)], axis=1).astype(BF16)
    a_r_t = a_r.reshape(nt, 1, gt * P)
    a_i_t = a_i.reshape(nt, 1, gt * P)
    return a_r_t, a_i_t, b_bd, c_bd


def _s5_layer_kernel(x_ref, g0_ref, g1_ref, d_ref, ar_ref, ai_ref, bbd_ref, cbd_ref, wglu_ref,
                     o_ref, bm, tm, bu, y_tm, state, *, nb, lc, d, nt, hw):
    c = pl.program_id(0)
    pitch = lc + S5_ROW_PAD

    @pl.when(c == 0)
    def _():
        state[...] = jnp.zeros_like(state)

    for b in range(nb):
        u = _rms(x_ref[b], g0_ref[...])
        for k in range(nt):
            bm[k, b * pitch:b * pitch + lc, :] = u[:, k * LANES:(k + 1) * LANES]
    for t in range(lc):
        for k in range(nt):
            tm[k, t * nb:(t + 1) * nb, :] = bm[k, pl.ds(t, nb, stride=pitch), :]

    for k in range(nt):
        uk = tm[k]
        bu[...] = jnp.dot(uk.astype(BF16), bbd_ref[k], preferred_element_type=F32)
        a_r = jnp.broadcast_to(ar_ref[k], (nb, hw))
        a_i = jnp.broadcast_to(ai_ref[k], (nb, hw))

        def step(t, carry):
            s_r, s_i = carry
            r0 = pl.multiple_of(t * nb, nb)
            n_r = a_r * s_r - a_i * s_i + bu[pl.ds(r0, nb), 0:hw]
            n_i = a_r * s_i + a_i * s_r + bu[pl.ds(r0, nb), hw:2 * hw]
            bu[pl.ds(r0, nb), 0:hw] = n_r
            bu[pl.ds(r0, nb), hw:2 * hw] = n_i
            return n_r, n_i

        s_r, s_i = lax.fori_loop(0, lc, step, (state[k, :, 0:hw], state[k, :, hw:2 * hw]), unroll=4)
        state[k, :, 0:hw] = s_r
        state[k, :, hw:2 * hw] = s_i
        y_tm[:, k * LANES:(k + 1) * LANES] = (
            jnp.dot(bu[...].astype(BF16), cbd_ref[k], preferred_element_type=F32)
            + d_ref[:, k * LANES:(k + 1) * LANES] * uk)

    gl = jax.nn.gelu(y_tm[...]).astype(BF16)
    z = jnp.dot(gl, wglu_ref[...], preferred_element_type=F32)
    mix = z[:, :d] * jax.nn.sigmoid(z[:, d:])
    mixn = _rms(mix, g1_ref[...])
    for k in range(nt):
        tm[k] = mixn[:, k * LANES:(k + 1) * LANES]
    for t in range(lc):
        for k in range(nt):
            bm[k, pl.ds(t, nb, stride=pitch), :] = tm[k, t * nb:(t + 1) * nb, :]
    for b in range(nb):
        for k in range(nt):
            sl = slice(k * LANES, (k + 1) * LANES)
            o_ref[b, :, sl] = x_ref[b, :, sl] + bm[k, b * pitch:b * pitch + lc, :]


def _s5_layer(x, g0, g1, d_skip, a_r_t, a_i_t, b_bd, c_bd, w_glu):
    B, S, D = x.shape
    lc = S5_CHUNK
    nt = D // LANES
    hw = a_r_t.shape[-1]
    rows = B * lc
    kern = functools.partial(_s5_layer_kernel, nb=B, lc=lc, d=D, nt=nt, hw=hw)
    return pl.pallas_call(
        kern,
        grid=(S // lc,),
        in_specs=[
            pl.BlockSpec((B, lc, D), lambda c: (0, c, 0)),
            _const_spec((1, D)), _const_spec((1, D)), _const_spec((1, D)),
            _const_spec((nt, 1, hw)), _const_spec((nt, 1, hw)),
            _const_spec((nt, LANES, 2 * hw)), _const_spec((nt, 2 * hw, LANES)),
            _const_spec((D, 2 * D)),
        ],
        out_specs=pl.BlockSpec((B, lc, D), lambda c: (0, c, 0)),
        out_shape=jax.ShapeDtypeStruct((B, S, D), F32),
        scratch_shapes=[
            pltpu.VMEM((nt, B * (lc + S5_ROW_PAD), LANES), F32), pltpu.VMEM((nt, rows, LANES), F32),
            pltpu.VMEM((rows, 2 * hw), F32), pltpu.VMEM((rows, D), F32),
            pltpu.VMEM((nt, B, 2 * hw), F32),
        ],
        compiler_params=_cparams("arbitrary"),
        name="s5_layer",
    )(x, g0, g1, d_skip, a_r_t, a_i_t, b_bd, c_bd, w_glu)


def _ffn_kernel(h_ref, g2_ref, g3_ref, w1_ref, w3_ref, w2_ref, o_ref):
    h = h_ref[...]
    f_in = _rms(h, g2_ref[...]).astype(BF16)
    a = jnp.dot(f_in, w1_ref[...], preferred_element_type=F32)
    b = jnp.dot(f_in, w3_ref[...], preferred_element_type=F32)
    act = (jax.nn.silu(a) * b).astype(BF16)
    f = jnp.dot(act, w2_ref[...], preferred_element_type=F32)
    o_ref[...] = h + _rms(f, g3_ref[...])


def _dense_ffn(h, g2, g3, w1, w3, w2):
    N, D = h.shape
    Fd = w1.shape[1]
    tm = ROW_TILE
    return pl.pallas_call(
        _ffn_kernel,
        grid=(N // tm,),
        in_specs=[
            pl.BlockSpec((tm, D), lambda i: (i, 0)),
            _const_spec((1, D)), _const_spec((1, D)),
            _const_spec((D, Fd)), _const_spec((D, Fd)), _const_spec((Fd, D)),
        ],
        out_specs=pl.BlockSpec((tm, D), lambda i: (i, 0)),
        out_shape=jax.ShapeDtypeStruct((N, D), F32),
        compiler_params=_cparams("parallel"),
        name="dense_ffn",
    )(h, g2, g3, w1, w3, w2)


def _rotary(xh, cs, s_neg, s_pos):
    half = ROT_DIM // 2
    return xh * cs + pltpu.roll(xh, LANES - half, 1) * s_neg + pltpu.roll(xh, half, 1) * s_pos


def _qkv_kernel(h_ref, pos_ref, invf_ref, sneg_ref, spos_ref, gq_ref, gkv_ref,
                wq_ref, wk_ref, wvt_ref, q_ref, k_ref, vt_ref, km_ref, *, nblk_tile, nblk_seq):
    ti = pl.program_id(1)
    h = h_ref[0]
    r = lax.rsqrt(jnp.mean(h * h, axis=-1, keepdims=True) + NORM_EPS)
    hq = (h * r * gq_ref[...]).astype(BF16)
    hk = (h * r * gkv_ref[...]).astype(BF16)
    ang = pos_ref[0] * invf_ref[...]
    cs = jnp.cos(ang)
    sn = jnp.sin(ang)
    s_neg = sn * sneg_ref[...]
    s_pos = sn * spos_ref[...]
    q = jnp.dot(hq, wq_ref[...], preferred_element_type=F32)
    k = jnp.dot(hk, wk_ref[...], preferred_element_type=F32)
    vt = lax.dot_general(wvt_ref[...], hk, (((1,), (1,)), ((), ())), preferred_element_type=F32)
    for hh in range(N_HEADS):
        sl = slice(hh * HEAD_DIM, (hh + 1) * HEAD_DIM)
        q_ref[0, hh] = (_rotary(q[:, sl], cs, s_neg, s_pos) * Q_SCALE_LOG2).astype(BF16)
        kr = _rotary(k[:, sl], cs, s_neg, s_pos)
        for j in range(nblk_tile):
            rs = slice(j * MOBA_BLOCK, (j + 1) * MOBA_BLOCK)
            k_ref[0, hh, j] = kr[rs].astype(BF16)
            vt_ref[0, hh, j] = vt[sl, rs].astype(BF16)
            km = jnp.mean(kr[rs], axis=0, keepdims=True)
            for piece in range(KMEAN_PIECES):
                part = km.astype(BF16).astype(F32)
                km_ref[0, hh, pl.ds(piece * nblk_seq + ti * nblk_tile + j, 1), :] = part
                km = km - part


def _qkv(h2, posb, invf, sneg, spos, gq, gkv, wq, wk, wvt):
    B, S, D = h2.shape
    T = QKV_TILE
    H, Dh, blk = N_HEADS, HEAD_DIM, MOBA_BLOCK
    nb = S // blk
    nbt = T // blk
    kern = functools.partial(_qkv_kernel, nblk_tile=nbt, nblk_seq=nb)
    return pl.pallas_call(
        kern,
        grid=(B, S // T),
        in_specs=[
            pl.BlockSpec((1, T, D), lambda b, t: (b, t, 0)),
            pl.BlockSpec((1, T, LANES), lambda b, t: (b, t, 0)),
            _const_spec((1, LANES)), _const_spec((1, LANES)), _const_spec((1, LANES)),
            _const_spec((1, D)), _const_spec((1, D)),
            _const_spec((D, D)), _const_spec((D, D)), _const_spec((D, D)),
        ],
        out_specs=[
            pl.BlockSpec((1, H, T, Dh), lambda b, t: (b, 0, t, 0)),
            pl.BlockSpec((1, H, nbt, blk, Dh), lambda b, t: (b, 0, t, 0, 0)),
            pl.BlockSpec((1, H, nbt, Dh, blk), lambda b, t: (b, 0, t, 0, 0)),
            pl.BlockSpec((1, H, KMEAN_PIECES * nb, Dh), lambda b, t: (b, 0, 0, 0)),
        ],
        out_shape=[
            jax.ShapeDtypeStruct((B, H, S, Dh), BF16),
            jax.ShapeDtypeStruct((B, H, nb, blk, Dh), BF16),
            jax.ShapeDtypeStruct((B, H, nb, Dh, blk), BF16),
            jax.ShapeDtypeStruct((B, H, KMEAN_PIECES * nb, Dh), F32),
        ],
        compiler_params=_cparams("parallel", "arbitrary"),
        name="qkv_proj",
    )(h2, posb, invf, sneg, spos, gq, gkv, wq, wk, wvt)


def _moba_kernel(q_ref, k_ref, vt_ref, km_ref, h_ref, g1_ref, wo_ref, g2_ref, wrt_ref, br_ref,
                 o_ref, f_ref, e_ref, wcol_ref, sel_scr, st_scr, acc_scr, o_scr, *, nb, k_sel):
    j = pl.program_id(1)
    blk = MOBA_BLOCK
    nt_dims = (((1,), (1,)), ((), ()))
    blk_id = lax.broadcasted_iota(jnp.int32, (nb, blk), 0)
    past = blk_id < j
    causal = (lax.broadcasted_iota(jnp.int32, (blk, blk), 0)
              <= lax.broadcasted_iota(jnp.int32, (blk, blk), 1))

    for hh in range(N_HEADS):
        parts = lax.dot_general(km_ref[0, hh].astype(BF16), q_ref[0, hh], nt_dims,
                                preferred_element_type=F32)
        gate = parts[0:nb]
        for piece in range(1, KMEAN_PIECES):
            gate = gate + parts[piece * nb:(piece + 1) * nb]
        gate = jnp.where(past, gate, NEG_INF)
        rank = jnp.zeros((nb, blk), F32)
        for m in range(nb):
            gm = gate[m:m + 1, :]
            beats = jnp.where(gm > gate, 1.0, jnp.where(gm == gate, jnp.where(m < blk_id, 1.0, 0.0), 0.0))
            rank = rank + beats
        sel_scr[hh] = jnp.where(past, jnp.where(rank < k_sel, 1.0, 0.0), 0.0)

    def qk_pass(n, ms, own):
        out = []
        for hh in range(N_HEADS):
            st = lax.dot_general(k_ref[0, hh, n], q_ref[0, hh], nt_dims, preferred_element_type=F32)
            allowed = causal if own else sel_scr[hh, pl.ds(n, 1), :] > 0.5
            st = jnp.where(allowed, st, NEG_INF)
            st_scr[hh, n] = st
            out.append(jnp.maximum(ms[hh], jnp.max(st, axis=0, keepdims=True)))
        return tuple(out)

    m_init = tuple(jnp.full((1, blk), NEG_INF, F32) for _ in range(N_HEADS))
    m_own = qk_pass(j, m_init, True)
    m_fin = lax.fori_loop(0, j, lambda n, ms: qk_pass(n, ms, False), m_own)

    def pv_pass(n, ls, first):
        out = []
        for hh in range(N_HEADS):
            p = jnp.exp2(st_scr[hh, n] - m_fin[hh])
            out.append(ls[hh] + jnp.sum(p, axis=0, keepdims=True))
            pv = jnp.dot(vt_ref[0, hh, n], p.astype(BF16), preferred_element_type=F32)
            acc_scr[hh] = pv if first else acc_scr[hh] + pv
        return tuple(out)

    l_own = pv_pass(j, tuple(jnp.zeros((1, blk), F32) for _ in range(N_HEADS)), True)
    l_fin = lax.fori_loop(0, j, lambda n, ls: pv_pass(n, ls, False), l_own)
    for hh in range(N_HEADS):
        o_t = acc_scr[hh] / l_fin[hh]
        o_scr[:, hh * HEAD_DIM:(hh + 1) * HEAD_DIM] = o_t.T.astype(BF16)

    mix = jnp.dot(o_scr[...], wo_ref[...], preferred_element_type=F32)
    h3 = h_ref[0] + _rms(mix, g1_ref[...])
    o_ref[0] = h3
    f = _rms(h3, g2_ref[...])
    f_ref[0] = f.astype(BF16)
    e_ref[0], wcol_ref[0] = _route(f, wrt_ref, br_ref)


def _moba_layer(q, k5, vt5, kmean, h2, g1, wo, g2, wrt, br):
    B, S, D = h2.shape
    H, Dh, blk = N_HEADS, HEAD_DIM, MOBA_BLOCK
    nb = S // blk
    E = wrt.shape[0]
    kern = functools.partial(_moba_kernel, nb=nb, k_sel=min(MOBA_TOPK, nb))
    return pl.pallas_call(
        kern,
        grid=(B, nb),
        in_specs=[
            pl.BlockSpec((1, H, blk, Dh), lambda b, j: (b, 0, j, 0)),
            pl.BlockSpec((1, H, nb, blk, Dh), lambda b, j: (b, 0, 0, 0, 0)),
            pl.BlockSpec((1, H, nb, Dh, blk), lambda b, j: (b, 0, 0, 0, 0)),
            pl.BlockSpec((1, H, KMEAN_PIECES * nb, Dh), lambda b, j: (b, 0, 0, 0)),
            pl.BlockSpec((1, blk, D), lambda b, j: (b, j, 0)),
            _const_spec((1, D)), _const_spec((D, D)),
            _const_spec((1, D)), _const_spec((E, D)), _const_spec((E, 1)),
        ],
        out_specs=[
            pl.BlockSpec((1, blk, D), lambda b, j: (b, j, 0)),
            pl.BlockSpec((1, blk, D), lambda b, j: (b, j, 0)),
            pl.BlockSpec((1, 2, blk), lambda b, j: (b * nb + j, 0, 0)),
            pl.BlockSpec((1, blk, LANES), lambda b, j: (b * nb + j, 0, 0)),
        ],
        out_shape=[
            jax.ShapeDtypeStruct((B, S, D), F32),
            jax.ShapeDtypeStruct((B, S, D), BF16),
            jax.ShapeDtypeStruct((B * nb, 2, blk), jnp.int32),
            jax.ShapeDtypeStruct((B * nb, blk, LANES), F32),
        ],
        scratch_shapes=[
            pltpu.VMEM((H, nb, blk), F32), pltpu.VMEM((H, nb, blk, blk), F32),
            pltpu.VMEM((H, Dh, blk), F32), pltpu.VMEM((blk, D), BF16),
        ],
        compiler_params=_cparams("parallel", "arbitrary"),
        name="moba_attn",
    )(q, k5, vt5, kmean, h2, g1, wo, g2, wrt, br)


def _route(f, wrt_ref, br_ref):
    nt_dims = (((1,), (1,)), ((), ()))
    w = wrt_ref[...]
    ne = w.shape[0]
    w_hi = w.astype(BF16).astype(F32)
    f_hi = f.astype(BF16)
    f_lo = (f - f_hi.astype(F32)).astype(BF16)
    w2 = jnp.concatenate([w_hi, w - w_hi], axis=0).astype(BF16)
    a = lax.dot_general(w2, f_hi, nt_dims, preferred_element_type=F32)
    b = lax.dot_general(w2[:ne], f_lo, nt_dims, preferred_element_type=F32)
    logit = a[:ne] + a[ne:] + b + br_ref[...]
    t = logit.shape[1]
    eidx = lax.broadcasted_iota(jnp.int32, (ne, t), 0)
    m1 = jnp.max(logit, axis=0, keepdims=True)
    e1 = jnp.min(jnp.where(logit == m1, eidx, ne), axis=0, keepdims=True)
    rest = jnp.where(eidx == e1, -jnp.inf, logit)
    m2 = jnp.max(rest, axis=0, keepdims=True)
    e2 = jnp.min(jnp.where(rest == m2, eidx, ne), axis=0, keepdims=True)
    ex = jnp.exp(m2 - m1)
    den = 1.0 + ex
    wpad = jnp.concatenate([1.0 / den, ex / den, jnp.zeros((LANES - 2, t), F32)], axis=0)
    return jnp.concatenate([e1, e2], axis=0), wpad.T


META_ROWS = 32


def _rank_kernel(e_ref, lpos_ref, lcol_ref, meta_ref, pad_ref, blke_ref, nused_ref, *, ne, rows_log2, nblk_pad):
    ntile, _, t = e_ref.shape
    eidx = lax.broadcasted_iota(jnp.int32, (ne, t), 0)

    def onehots(i):
        e = e_ref[i]
        oh0 = jnp.where(eidx == e[0:1, :], 1.0, 0.0)
        oh1 = jnp.where(eidx == e[1:2, :], 1.0, 0.0)
        return oh0, oh1

    def excl_cumsum(col):
        run = jnp.zeros((1, 1), jnp.int32)
        parts = []
        for e in range(ne):
            parts.append(run)
            run = run + col[e:e + 1, :]
        return jnp.concatenate(parts, axis=0), run

    def count(i, cnt):
        oh0, oh1 = onehots(i)
        return cnt + jnp.sum(oh0 + oh1, axis=1, keepdims=True).astype(jnp.int32)

    cnt = lax.fori_loop(0, ntile, count, jnp.zeros((ne, 1), jnp.int32))
    pcnt = ((cnt + ((1 << rows_log2) - 1)) >> rows_log2) << rows_log2
    pstart, ptotal = excl_cumsum(pcnt)
    pend = pstart + pcnt

    upper = jnp.where(lax.broadcasted_iota(jnp.int32, (t, t), 0) < lax.broadcasted_iota(jnp.int32, (t, t), 1),
                      1.0, 0.0).astype(BF16)
    lane_b = lambda col: jnp.broadcast_to(col, (ne, LANES))

    def place(i, base):
        oh0, oh1 = onehots(i)
        both = oh0 + oh1
        before = jnp.dot(both.astype(BF16), upper, preferred_element_type=F32).astype(jnp.int32)
        cnt_i = jnp.sum(both, axis=1, keepdims=True).astype(jnp.int32)
        loff, _ = excl_cumsum(cnt_i)
        slot = loff + before
        l0 = jnp.sum(jnp.where(oh0 > 0.5, slot, 0), axis=0, keepdims=True)
        l1 = jnp.sum(jnp.where(oh1 > 0.5, slot, 0), axis=0, keepdims=True)
        lpos_ref[i] = jnp.concatenate([l0, l1], axis=0)
        lpad = jnp.concatenate([l0.astype(F32), l1.astype(F32), jnp.zeros((LANES - 2, t), F32)], axis=0)
        lcol_ref[i] = lpad.T
        meta_ref[i] = jnp.concatenate(
            [lane_b(cnt_i), lane_b(loff), lane_b(pstart + base),
             jnp.zeros((META_ROWS - 3 * ne, LANES), jnp.int32)], axis=0)
        return base + cnt_i

    lax.fori_loop(0, ntile, place, jnp.zeros((ne, 1), jnp.int32))

    pad_ref[...] = jnp.concatenate([lane_b(pstart + cnt), lane_b(pcnt - cnt),
                                    jnp.broadcast_to(ptotal, (ne, LANES))], axis=0)
    blk_row = lax.broadcasted_iota(jnp.int32, (ne, nblk_pad), 1) << rows_log2
    blke = jnp.sum(jnp.where(pend <= blk_row, 1, 0), axis=0, keepdims=True)
    blke_ref[...] = jnp.minimum(blke, ne - 1)
    nused_ref[...] = jnp.broadcast_to(ptotal >> rows_log2, (1, LANES))


def _rank(e3, ne, nblk_pad):
    ntile, _, t = e3.shape
    kern = functools.partial(_rank_kernel, ne=ne, rows_log2=int(math.log2(MOE_ROWS)), nblk_pad=nblk_pad)
    return pl.pallas_call(
        kern,
        out_shape=[
            jax.ShapeDtypeStruct((ntile, 2, t), jnp.int32),
            jax.ShapeDtypeStruct((ntile, t, LANES), F32),
            jax.ShapeDtypeStruct((ntile, META_ROWS, LANES), jnp.int32),
            jax.ShapeDtypeStruct((3 * ne, LANES), jnp.int32),
            jax.ShapeDtypeStruct((1, nblk_pad), jnp.int32),
            jax.ShapeDtypeStruct((1, LANES), jnp.int32),
        ],
        compiler_params=pltpu.CompilerParams(vmem_limit_bytes=VMEM_LIMIT),
        name="moe_rank",
    )(e3)


SUBROWS = 8


def _rows_to_interleaved(ref, x):
    n = x.shape[0]
    for s in range(SUBROWS):
        ref[pl.ds(s, n, stride=SUBROWS), :] = x[:, s * LANES:(s + 1) * LANES]


def _interleaved_to_rows(ref, n):
    return jnp.concatenate([ref[pl.ds(s, n, stride=SUBROWS), :] for s in range(SUBROWS)], axis=1)


def _segment_copies(count, src_ref, src_row, dst_ref, dst_row, sem, max_piece, wait=False, advance_src=True):
    p = max_piece
    while p >= 1:
        hit = (count & p) != 0

        @pl.when(hit)
        def _(p=p, src_row=src_row, dst_row=dst_row):
            cp = pltpu.make_async_copy(
                src_ref.at[pl.ds(pl.multiple_of(src_row * SUBROWS, SUBROWS), p * SUBROWS), :],
                dst_ref.at[pl.ds(pl.multiple_of(dst_row * SUBROWS, SUBROWS), p * SUBROWS), :], sem)
            if wait:
                cp.wait()
            else:
                cp.start()

        step = count & p
        if advance_src:
            src_row = src_row + step
        dst_row = dst_row + step
        p //= 2


def _dispatch_kernel(meta_ref, pad_ref, lpos_ref, f_ref, xs_ref, srt, zbuf, sems, zsem, *, td, ne):
    i = pl.program_id(0)

    @pl.when(i == 0)
    def _():
        zbuf[...] = jnp.zeros_like(zbuf)
        zrows = zbuf.shape[0] // SUBROWS
        total = xs_ref.shape[0] // SUBROWS
        for wait in (False, True):
            for e in range(ne):
                _segment_copies(pad_ref[ne + e], zbuf, 0, xs_ref, pad_ref[e], zsem,
                                zrows, wait=wait, advance_src=False)
            for r in range(0, ne * MOE_ROWS, zrows):
                row = pad_ref[2 * ne] + r

                @pl.when(row < total)
                def _(row=row):
                    cp = pltpu.make_async_copy(
                        zbuf, xs_ref.at[pl.ds(pl.multiple_of(row * SUBROWS, SUBROWS), zrows * SUBROWS), :], zsem)
                    if wait:
                        cp.wait()
                    else:
                        cp.start()

    slot = i % 2
    lp = lpos_ref[0]
    r = lax.broadcasted_iota(jnp.int32, (2 * td, td), 0)
    perm = (jnp.where(r == lp[0:1, :], 1.0, 0.0) + jnp.where(r == lp[1:2, :], 1.0, 0.0)).astype(BF16)
    _rows_to_interleaved(srt.at[slot], jnp.dot(perm, f_ref[...], preferred_element_type=F32))
    for e in range(ne):
        _segment_copies(meta_ref[0, 0, e], srt.at[slot], meta_ref[0, 0, ne + e], xs_ref,
                        meta_ref[0, 0, 2 * ne + e], sems.at[slot], td)

    def drain(s):
        for h in range(2):
            pltpu.make_async_copy(srt.at[s, pl.ds(h * td * SUBROWS, td * SUBROWS), :],
                                  xs_ref.at[pl.ds(0, td * SUBROWS), :], sems.at[s]).wait()

    @pl.when(i > 0)
    def _():
        drain(1 - slot)

    @pl.when(i + 1 == pl.num_programs(0))
    def _():
        drain(slot)


def _dispatch(meta, pad, lpos, f, total):
    N, D = f.shape
    ntile, _, td = lpos.shape
    ne = N_EXPERTS
    assert D == SUBROWS * LANES
    kern = functools.partial(_dispatch_kernel, td=td, ne=ne)
    return pl.pallas_call(
        kern,
        grid=(ntile,),
        in_specs=[
            pl.BlockSpec((1, 1, META_ROWS), lambda i: (i, 0, 0), memory_space=pltpu.SMEM),
            pl.BlockSpec(memory_space=pltpu.SMEM),
            pl.BlockSpec((1, 2, td), lambda i: (i, 0, 0)),
            pl.BlockSpec((td, D), lambda i: (i, 0)),
        ],
        out_specs=pl.BlockSpec(memory_space=pl.ANY),
        out_shape=jax.ShapeDtypeStruct((total * SUBROWS, LANES), F32),
        scratch_shapes=[pltpu.VMEM((2, 2 * td * SUBROWS, LANES), F32),
                        pltpu.VMEM((MOE_ROWS // 2 * SUBROWS, LANES), F32),
                        pltpu.SemaphoreType.DMA((2,)), pltpu.SemaphoreType.DMA(())],
        compiler_params=_cparams("arbitrary"),
        name="moe_dispatch",
    )(meta, pad, lpos, f)


def _expert_kernel(blke_ref, nused_ref, xs_ref, w1_ref, w3_ref, w2_ref, y_ref, *, rb):
    i = pl.program_id(0)

    @pl.when(i < nused_ref[0])
    def _():
        x = _interleaved_to_rows(xs_ref, rb).astype(BF16)
        a = jnp.dot(x, w1_ref[0], preferred_element_type=F32)
        b = jnp.dot(x, w3_ref[0], preferred_element_type=F32)
        act = (jax.nn.silu(a) * b).astype(BF16)
        _rows_to_interleaved(y_ref, jnp.dot(act, w2_ref[0], preferred_element_type=F32))

    @pl.when(i >= nused_ref[0])
    def _():
        y_ref[...] = jnp.zeros_like(y_ref)


def _experts(blke, nused, xs, w1, w3, w2):
    E, D, Fe = w1.shape
    total = xs.shape[0] // SUBROWS
    rb = MOE_ROWS
    nblk = total // rb
    wspec = lambda shape: pl.BlockSpec(shape, lambda i, be, nu: (be[i], 0, 0), pipeline_mode=pl.Buffered(1))
    return pl.pallas_call(
        functools.partial(_expert_kernel, rb=rb),
        grid_spec=pltpu.PrefetchScalarGridSpec(
            num_scalar_prefetch=2,
            grid=(nblk,),
            in_specs=[
                pl.BlockSpec((rb * SUBROWS, LANES), lambda i, be, nu: (jnp.minimum(i, nu[0] - 1), 0)),
                wspec((1, D, Fe)), wspec((1, D, Fe)), wspec((1, Fe, D)),
            ],
            out_specs=pl.BlockSpec((rb * SUBROWS, LANES), lambda i, be, nu: (i, 0)),
        ),
        out_shape=jax.ShapeDtypeStruct((total * SUBROWS, LANES), F32),
        compiler_params=_cparams("arbitrary"),
        name="moe_experts",
    )(blke, nused, xs, w1, w3, w2)


def _combine_kernel(meta_ref, nxt_ref, yb_ref, lcol_ref, wcol_ref, h_ref, g3_ref, o_ref, buf, sems, *, td, ne):
    i = pl.program_id(0)
    slot = i % 2

    def fetch(m_ref, s):
        for e in range(ne):
            _segment_copies(m_ref[0, 0, e], yb_ref, m_ref[0, 0, 2 * ne + e], buf.at[s], m_ref[0, 0, ne + e],
                            sems.at[s], td)

    @pl.when(i == 0)
    def _():
        fetch(meta_ref, 0)

    @pl.when(i + 1 < pl.num_programs(0))
    def _():
        fetch(nxt_ref, 1 - slot)

    for h in range(2):
        pltpu.make_async_copy(yb_ref.at[pl.ds(0, td * SUBROWS), :],
                              buf.at[slot, pl.ds(h * td * SUBROWS, td * SUBROWS), :], sems.at[slot]).wait()
    yb = _interleaved_to_rows(buf.at[slot], 2 * td).astype(BF16)
    lc = lcol_ref[0]
    w = wcol_ref[...]
    c = lax.broadcasted_iota(jnp.int32, (td, 2 * td), 1).astype(F32)
    pick0 = jnp.where(c == lc[:, 0:1], 1.0, 0.0).astype(BF16)
    pick1 = jnp.where(c == lc[:, 1:2], 1.0, 0.0).astype(BF16)
    f = (w[:, 0:1] * jnp.dot(pick0, yb, preferred_element_type=F32)
         + w[:, 1:2] * jnp.dot(pick1, yb, preferred_element_type=F32))
    o_ref[...] = h_ref[...] + _rms(f, g3_ref[...])


def _combine(meta, yb, lcol, wcol, h3, g3):
    N, D = h3.shape
    ntile, td, _ = lcol.shape
    ne = N_EXPERTS
    kern = functools.partial(_combine_kernel, td=td, ne=ne)
    return pl.pallas_call(
        kern,
        grid=(ntile,),
        in_specs=[
            pl.BlockSpec((1, 1, META_ROWS), lambda i: (i, 0, 0), memory_space=pltpu.SMEM),
            pl.BlockSpec((1, 1, META_ROWS), lambda i: (jnp.minimum(i + 1, ntile - 1), 0, 0),
                         memory_space=pltpu.SMEM),
            pl.BlockSpec(memory_space=pl.ANY),
            pl.BlockSpec((1, td, LANES), lambda i: (i, 0, 0)),
            pl.BlockSpec((td, LANES), lambda i: (i, 0)),
            pl.BlockSpec((td, D), lambda i: (i, 0)),
            _const_spec((1, D)),
        ],
        out_specs=pl.BlockSpec((td, D), lambda i: (i, 0)),
        out_shape=jax.ShapeDtypeStruct((N, D), F32),
        scratch_shapes=[pltpu.VMEM((2, 2 * td * SUBROWS, LANES), F32), pltpu.SemaphoreType.DMA((2,))],
        compiler_params=_cparams("arbitrary"),
        name="moe_combine",
    )(meta, meta, yb, lcol, wcol, h3, g3)


def _rope_lane_tables():
    half = ROT_DIM // 2
    inv_freq = 1.0 / (ROPE_THETA ** (np.arange(half, dtype=np.float32) / half))
    invf = np.zeros((1, LANES), np.float32)
    invf[0, :half] = inv_freq
    invf[0, half:ROT_DIM] = inv_freq
    sneg = np.zeros((1, LANES), np.float32)
    sneg[0, :half] = -1.0
    spos = np.zeros((1, LANES), np.float32)
    spos[0, half:ROT_DIM] = 1.0
    return jnp.asarray(invf), jnp.asarray(sneg), jnp.asarray(spos)


def kernel(x, positions, norms, s5_lam_re, s5_lam_im, s5_log_dt, s5_b_re, s5_b_im, s5_c_re, s5_c_im, s5_d, s5_w_glu, kv_norm, w_kv, w_q, w_o, ffn_w1, ffn_w3, ffn_w2, moe_router, moe_bias, moe_w1, moe_w3, moe_w2):
    B, S, D = x.shape
    N = B * S
    HD = N_HEADS * HEAD_DIM
    E = moe_router.shape[-1]
    row = lambda v: v.astype(F32).reshape(1, -1)

    a_r, a_i, bb_r, bb_i = _s5_prep(s5_lam_re[0], s5_lam_im[0], s5_log_dt[0], s5_b_re[0], s5_b_im[0])
    a_r_t, a_i_t, b_bd, c_bd = _s5_block_diag(a_r, a_i, bb_r, bb_i, s5_c_re[0], s5_c_im[0])
    h1 = _s5_layer(x, row(norms[0, 0]), row(norms[0, 1]), row(s5_d[0]), a_r_t, a_i_t, b_bd, c_bd,
                   s5_w_glu[0].astype(BF16))
    h2 = _dense_ffn(h1.reshape(N, D), row(norms[0, 2]), row(norms[0, 3]),
                    ffn_w1[0].astype(BF16), ffn_w3[0].astype(BF16), ffn_w2[0].astype(BF16))

    invf, sneg, spos = _rope_lane_tables()
    posb = jnp.broadcast_to(positions.astype(F32)[:, :, None], (B, S, LANES))
    q, k5, vt5, kmean = _qkv(h2.reshape(B, S, D), posb, invf, sneg, spos, row(norms[1, 0]), row(kv_norm),
                             w_q[0].astype(BF16), w_kv[:, :HD].astype(BF16), w_kv[:, HD:].T.astype(BF16))
    h3, f_in, e_tiles, wcol = _moba_layer(q, k5, vt5, kmean, h2.reshape(B, S, D), row(norms[1, 1]), w_o[0].astype(BF16),
                                          row(norms[1, 2]), moe_router[0].T.astype(F32),
                                          moe_bias[0].astype(F32).reshape(E, 1))
    h3 = h3.reshape(N, D)

    assert MOBA_BLOCK == DISPATCH_TILE
    total = 2 * N + E * MOE_ROWS
    nblk = total // MOE_ROWS
    nblk_pad = -(-nblk // LANES) * LANES
    lpos, lcol, meta, pad, blke, nused = _rank(e_tiles, E, nblk_pad)
    meta = meta[:, :, 0].reshape(-1, 1, META_ROWS)
    xs = _dispatch(meta, pad[:, 0], lpos, f_in.reshape(N, D), total)
    yb = _experts(blke[0, :nblk], nused[0, :1], xs,
                  moe_w1[0].astype(BF16), moe_w3[0].astype(BF16), moe_w2[0].astype(BF16))
    out = _combine(meta, yb, lcol, wcol.reshape(N, LANES), h3, row(norms[1, 3]))
    return out.reshape(B, S, D)
```

```python
import functools
import math

import numpy as np
import jax
import jax.numpy as jnp
from jax import lax
from jax.experimental import pallas as pl
from jax.experimental.pallas import tpu as pltpu

F32 = jnp.float32
BF16 = jnp.bfloat16

NORM_EPS = 1e-6
NEG_INF = -1e30

SSM_GROUP = 16
N_HEADS = 8
HEAD_DIM = 128
ROT_DIM = HEAD_DIM // 4
ROPE_THETA = 500000.0
MOBA_BLOCK = 256
MOBA_TOPK = 3
N_EXPERTS = 8

Q_SCALE_LOG2 = HEAD_DIM ** -0.5 * math.log2(math.e)

KMEAN_PIECES = 3

LANES = 128
GROUPS_PER_TILE = LANES // SSM_GROUP

VMEM_LIMIT = 56 * 1024 * 1024

S5_CHUNK = 32
S5_ROW_PAD = 8
ROW_TILE = 512
QKV_TILE = 512
MOE_ROWS = 512
DISPATCH_TILE = 256


def _cparams(*sem):
    return pltpu.CompilerParams(dimension_semantics=sem, vmem_limit_bytes=VMEM_LIMIT)


def _rms(x, g):
    return x * lax.rsqrt(jnp.mean(x * x, axis=-1, keepdims=True) + NORM_EPS) * g


def _const_spec(shape):
    n = len(shape)
    return pl.BlockSpec(shape, lambda *_: (0,) * n, pipeline_mode=pl.Buffered(1))


def _zoh(lr, li, dt):
    zr = lr * dt
    zi = li * dt
    mag = jnp.exp(zr)
    return mag * jnp.cos(zr * 0 + zi), mag * jnp.sin(zi)


def _cmul(xr, xi, yr, yi):
    return xr * yr - xi * yi, xr * yi + xi * yr


def _s5_prep_kernel(lrb_ref, lib_ref, lrc_ref, lic_ref, ldt_ref, br_ref, bi_ref, cr_ref, ci_ref,
                    a2r_ref, a2i_ref, bbr_ref, bbi_ref, abr_ref, abi_ref, car_ref, cai_ref, c2r_ref, c2i_ref):
    dt = jnp.exp(ldt_ref[...])
    lr = lrb_ref[...]
    li = lib_ref[...]
    a_r, a_i = _zoh(lr, li, dt)
    den = lr * lr + li * li
    coef_r = ((a_r - 1.0) * lr + a_i * li) / den
    coef_i = (a_i * lr - (a_r - 1.0) * li) / den
    bb_r, bb_i = _cmul(coef_r, coef_i, br_ref[...], bi_ref[...])
    bbr_ref[...] = bb_r
    bbi_ref[...] = bb_i
    abr_ref[...], abi_ref[...] = _cmul(a_r, a_i, bb_r, bb_i)
    a2r_ref[...], a2i_ref[...] = _cmul(a_r, a_i, a_r, a_i)
    ac_r, ac_i = _zoh(lrc_ref[...], lic_ref[...], dt)
    ca_r, ca_i = _cmul(cr_ref[...], ci_ref[...], ac_r, ac_i)
    car_ref[...] = ca_r
    cai_ref[...] = ca_i
    c2r_ref[...], c2i_ref[...] = _cmul(ca_r, ca_i, ac_r, ac_i)


def _s5_fold_kernel(b_ref, ab_ref, c_ref, d0_ref, d1_ref):
    for k in range(b_ref.shape[0]):
        d0_ref[k] = jnp.dot(b_ref[k], c_ref[k], precision=lax.Precision.HIGHEST, preferred_element_type=F32)
        d1_ref[k] = jnp.dot(ab_ref[k], c_ref[k], precision=lax.Precision.HIGHEST, preferred_element_type=F32)


def _s5_params(lam_re, lam_im, log_dt, b_re, b_im, c_re, c_im):
    G, P = lam_re.shape
    Q = b_re.shape[-1]
    gt = GROUPS_PER_TILE
    nt = G // gt
    f = lambda a: a.astype(F32)
    rep = lambda a: jnp.repeat(f(a), Q, axis=1)
    til = lambda a: jnp.tile(f(a), (1, Q))
    out = jax.ShapeDtypeStruct((G, P * Q), F32)
    a2_r, a2_i, bb_r, bb_i, ab_r, ab_i, ca_r, ca_i, c2_r, c2_i = pl.pallas_call(
        _s5_prep_kernel, out_shape=(out,) * 10, name="s5_prep",
    )(rep(lam_re), rep(lam_im), til(lam_re), til(lam_im), f(log_dt).reshape(G, 1),
      f(b_re).reshape(G, P * Q), f(b_im).reshape(G, P * Q), f(c_re).reshape(G, Q * P), f(c_im).reshape(G, Q * P))

    eye = jnp.eye(gt, dtype=F32)
    def bmat(bb):
        return jnp.einsum('kgpi,gh->kgihp', bb.reshape(nt, gt, P, Q), eye).reshape(nt, gt * Q, gt * P)
    def cmat(c):
        return jnp.einsum('kgip,gh->kgphi', c.reshape(nt, gt, Q, P), eye).reshape(nt, gt * P, gt * Q)
    bmat2 = lambda r, i: jnp.concatenate([bmat(r), bmat(i)], axis=-1)
    cmat2 = lambda r, i: jnp.concatenate([cmat(r), -cmat(i)], axis=1)
    b_bd, ab_bd = bmat2(bb_r, bb_i), bmat2(ab_r, ab_i)
    c_bd = cmat2(f(c_re).reshape(G, Q * P), f(c_im).reshape(G, Q * P))
    dshape = jax.ShapeDtypeStruct((nt, LANES, LANES), F32)
    d0, d1 = pl.pallas_call(_s5_fold_kernel, out_shape=(dshape, dshape), name="s5_fold",
                            compiler_params=pltpu.CompilerParams(vmem_limit_bytes=VMEM_LIMIT))(b_bd, ab_bd, c_bd)
    b2 = jnp.concatenate([ab_bd, b_bd], axis=1).astype(BF16)
    c2 = jnp.concatenate([cmat2(ca_r, ca_i), cmat2(c2_r, c2_i)], axis=2).astype(BF16)
    d2 = jnp.concatenate([jnp.concatenate([d0, d1], axis=2),
                          jnp.concatenate([jnp.zeros_like(d0), d0], axis=2)], axis=1).astype(BF16)
    sel = lambda a: a.reshape(G, P, Q)[:, :, 0].reshape(nt, 1, gt * P)
    return sel(a2_r), sel(a2_i), b2, c2, d2


def _s5_layer_kernel(x_ref, g0_ref, g1_ref, d_ref, ar_ref, ai_ref, b2_ref, c2_ref, d2_ref, wglu_ref,
                     o_ref, bm, tm, bu, y_tm, state, *, nb, lc, d, nt, hw):
    c = pl.program_id(0)
    pitch = lc + S5_ROW_PAD
    npair = lc // 2

    @pl.when(c == 0)
    def _():
        state[...] = jnp.zeros_like(state)

    for b in range(nb):
        u = _rms(x_ref[b], g0_ref[...])
        for k in range(nt):
            bm[k, b * pitch:b * pitch + lc, :] = u[:, k * LANES:(k + 1) * LANES]
    for t in range(lc):
        for k in range(nt):
            tm[k, t * nb:(t + 1) * nb, :] = bm[k, pl.ds(t, nb, stride=pitch), :]

    for k in range(nt):
        uk = tm[k]
        u4 = uk.reshape(npair, 2, nb, LANES)
        u2 = jnp.concatenate([u4[:, 0].reshape(npair * nb, LANES), u4[:, 1].reshape(npair * nb, LANES)],
                             axis=1).astype(BF16)
        bu[...] = jnp.dot(u2, b2_ref[k], preferred_element_type=F32)
        a_r = jnp.broadcast_to(ar_ref[k], (nb, hw))
        a_i = jnp.broadcast_to(ai_ref[k], (nb, hw))

        def step(t, carry):
            s_r, s_i = carry
            r0 = pl.multiple_of(t * nb, nb)
            n_r = a_r * s_r - a_i * s_i + bu[pl.ds(r0, nb), 0:hw]
            n_i = a_r * s_i + a_i * s_r + bu[pl.ds(r0, nb), hw:2 * hw]
            bu[pl.ds(r0, nb), 0:hw] = s_r
            bu[pl.ds(r0, nb), hw:2 * hw] = s_i
            return n_r, n_i

        s_r, s_i = lax.fori_loop(0, npair, step, (state[k, :, 0:hw], state[k, :, hw:2 * hw]), unroll=4)
        state[k, :, 0:hw] = s_r
        state[k, :, hw:2 * hw] = s_i
        y2 = (jnp.dot(bu[...].astype(BF16), c2_ref[k], preferred_element_type=F32)
              + jnp.dot(u2, d2_ref[k], preferred_element_type=F32))
        yk = jnp.concatenate([y2[:, :LANES].reshape(npair, 1, nb, LANES),
                              y2[:, LANES:].reshape(npair, 1, nb, LANES)], axis=1).reshape(lc * nb, LANES)
        y_tm[:, k * LANES:(k + 1) * LANES] = yk + d_ref[:, k * LANES:(k + 1) * LANES] * uk

    gl = jax.nn.gelu(y_tm[...]).astype(BF16)
    z = jnp.dot(gl, wglu_ref[...], preferred_element_type=F32)
    mix = z[:, :d] * jax.nn.sigmoid(z[:, d:])
    mixn = _rms(mix, g1_ref[...])
    for k in range(nt):
        tm[k] = mixn[:, k * LANES:(k + 1) * LANES]
    for t in range(lc):
        for k in range(nt):
            bm[k, pl.ds(t, nb, stride=pitch), :] = tm[k, t * nb:(t + 1) * nb, :]
    for b in range(nb):
        for k in range(nt):
            sl = slice(k * LANES, (k + 1) * LANES)
            o_ref[b, :, sl] = x_ref[b, :, sl] + bm[k, b * pitch:b * pitch + lc, :]


def _s5_layer(x, g0, g1, d_skip, a2_r_t, a2_i_t, b2, c2, d2, w_glu):
    B, S, D = x.shape
    lc = S5_CHUNK
    nt = D // LANES
    hw = a2_r_t.shape[-1]
    rows = B * lc
    kern = functools.partial(_s5_layer_kernel, nb=B, lc=lc, d=D, nt=nt, hw=hw)
    return pl.pallas_call(
        kern,
        grid=(S // lc,),
        in_specs=[
            pl.BlockSpec((B, lc, D), lambda c: (0, c, 0)),
            _const_spec((1, D)), _const_spec((1, D)), _const_spec((1, D)),
            _const_spec((nt, 1, hw)), _const_spec((nt, 1, hw)),
            _const_spec((nt, 2 * LANES, 2 * hw)), _const_spec((nt, 2 * hw, 2 * LANES)),
            _const_spec((nt, 2 * LANES, 2 * LANES)),
            _const_spec((D, 2 * D)),
        ],
        out_specs=pl.BlockSpec((B, lc, D), lambda c: (0, c, 0)),
        out_shape=jax.ShapeDtypeStruct((B, S, D), F32),
        scratch_shapes=[
            pltpu.VMEM((nt, B * (lc + S5_ROW_PAD), LANES), F32), pltpu.VMEM((nt, rows, LANES), F32),
            pltpu.VMEM((rows // 2, 2 * hw), F32), pltpu.VMEM((rows, D), F32),
            pltpu.VMEM((nt, B, 2 * hw), F32),
        ],
        compiler_params=_cparams("arbitrary"),
        name="s5_layer",
    )(x, g0, g1, d_skip, a2_r_t, a2_i_t, b2, c2, d2, w_glu)


def _ffn_kernel(h_ref, g2_ref, g3_ref, w1_ref, w3_ref, w2_ref, o_ref):
    h = h_ref[...]
    f_in = _rms(h, g2_ref[...]).astype(BF16)
    a = jnp.dot(f_in, w1_ref[...], preferred_element_type=F32)
    b = jnp.dot(f_in, w3_ref[...], preferred_element_type=F32)
    act = (jax.nn.silu(a) * b).astype(BF16)
    f = jnp.dot(act, w2_ref[...], preferred_element_type=F32)
    o_ref[...] = h + _rms(f, g3_ref[...])


def _dense_ffn(h, g2, g3, w1, w3, w2):
    N, D = h.shape
    Fd = w1.shape[1]
    tm = ROW_TILE
    return pl.pallas_call(
        _ffn_kernel,
        grid=(N // tm,),
        in_specs=[
            pl.BlockSpec((tm, D), lambda i: (i, 0)),
            _const_spec((1, D)), _const_spec((1, D)),
            _const_spec((D, Fd)), _const_spec((D, Fd)), _const_spec((Fd, D)),
        ],
        out_specs=pl.BlockSpec((tm, D), lambda i: (i, 0)),
        out_shape=jax.ShapeDtypeStruct((N, D), F32),
        compiler_params=_cparams("parallel"),
        name="dense_ffn",
    )(h, g2, g3, w1, w3, w2)


def _rotary(xh, cs, s_neg, s_pos):
    half = ROT_DIM // 2
    return xh * cs + pltpu.roll(xh, LANES - half, 1) * s_neg + pltpu.roll(xh, half, 1) * s_pos


def _qkv_kernel(h_ref, pos_ref, invf_ref, sneg_ref, spos_ref, gq_ref, gkv_ref,
                wq_ref, wk_ref, wvt_ref, q_ref, k_ref, vt_ref, km_ref, *, nblk_tile, nblk_seq):
    ti = pl.program_id(1)
    h = h_ref[0]
    r = lax.rsqrt(jnp.mean(h * h, axis=-1, keepdims=True) + NORM_EPS)
    hq = (h * r * gq_ref[...]).astype(BF16)
    hk = (h * r * gkv_ref[...]).astype(BF16)
    ang = pos_ref[0] * invf_ref[...]
    cs = jnp.cos(ang)
    sn = jnp.sin(ang)
    s_neg = sn * sneg_ref[...]
    s_pos = sn * spos_ref[...]
    q = jnp.dot(hq, wq_ref[...], preferred_element_type=F32)
    k = jnp.dot(hk, wk_ref[...], preferred_element_type=F32)
    vt = lax.dot_general(wvt_ref[...], hk, (((1,), (1,)), ((), ())), preferred_element_type=F32)
    for hh in range(N_HEADS):
        sl = slice(hh * HEAD_DIM, (hh + 1) * HEAD_DIM)
        q_ref[0, hh] = (_rotary(q[:, sl], cs, s_neg, s_pos) * Q_SCALE_LOG2).astype(BF16)
        kr = _rotary(k[:, sl], cs, s_neg, s_pos)
        for j in range(nblk_tile):
            rs = slice(j * MOBA_BLOCK, (j + 1) * MOBA_BLOCK)
            k_ref[0, hh, j] = kr[rs].astype(BF16)
            vt_ref[0, hh, j] = vt[sl, rs].astype(BF16)
            km = jnp.mean(kr[rs], axis=0, keepdims=True)
            for piece in range(KMEAN_PIECES):
                part = km.astype(BF16).astype(F32)
                km_ref[0, hh, pl.ds(piece * nblk_seq + ti * nblk_tile + j, 1), :] = part
                km = km - part


def _qkv(h2, posb, invf, sneg, spos, gq, gkv, wq, wk, wvt):
    B, S, D = h2.shape
    T = QKV_TILE
    H, Dh, blk = N_HEADS, HEAD_DIM, MOBA_BLOCK
    nb = S // blk
    nbt = T // blk
    kern = functools.partial(_qkv_kernel, nblk_tile=nbt, nblk_seq=nb)
    return pl.pallas_call(
        kern,
        grid=(B, S // T),
        in_specs=[
            pl.BlockSpec((1, T, D), lambda b, t: (b, t, 0)),
            pl.BlockSpec((1, T, LANES), lambda b, t: (b, t, 0)),
            _const_spec((1, LANES)), _const_spec((1, LANES)), _const_spec((1, LANES)),
            _const_spec((1, D)), _const_spec((1, D)),
            _const_spec((D, D)), _const_spec((D, D)), _const_spec((D, D)),
        ],
        out_specs=[
            pl.BlockSpec((1, H, T, Dh), lambda b, t: (b, 0, t, 0)),
            pl.BlockSpec((1, H, nbt, blk, Dh), lambda b, t: (b, 0, t, 0, 0)),
            pl.BlockSpec((1, H, nbt, Dh, blk), lambda b, t: (b, 0, t, 0, 0)),
            pl.BlockSpec((1, H, KMEAN_PIECES * nb, Dh), lambda b, t: (b, 0, 0, 0)),
        ],
        out_shape=[
            jax.ShapeDtypeStruct((B, H, S, Dh), BF16),
            jax.ShapeDtypeStruct((B, H, nb, blk, Dh), BF16),
            jax.ShapeDtypeStruct((B, H, nb, Dh, blk), BF16),
            jax.ShapeDtypeStruct((B, H, KMEAN_PIECES * nb, Dh), F32),
        ],
        compiler_params=_cparams("parallel", "arbitrary"),
        name="qkv_proj",
    )(h2, posb, invf, sneg, spos, gq, gkv, wq, wk, wvt)


def _moba_kernel(q_ref, k_ref, vt_ref, km_ref, h_ref, g1_ref, wo_ref, g2_ref, wrt_ref, br_ref,
                 o_ref, f_ref, e_ref, wcol_ref, sel_scr, st_scr, acc_scr, o_scr, *, nb, k_sel):
    j = pl.program_id(1)
    blk = MOBA_BLOCK
    nt_dims = (((1,), (1,)), ((), ()))
    blk_id = lax.broadcasted_iota(jnp.int32, (nb, blk), 0)
    past = blk_id < j
    causal = (lax.broadcasted_iota(jnp.int32, (blk, blk), 0)
              <= lax.broadcasted_iota(jnp.int32, (blk, blk), 1))

    for hh in range(N_HEADS):
        parts = lax.dot_general(km_ref[0, hh].astype(BF16), q_ref[0, hh], nt_dims,
                                preferred_element_type=F32)
        gate = parts[0:nb]
        for piece in range(1, KMEAN_PIECES):
            gate = gate + parts[piece * nb:(piece + 1) * nb]
        gate = jnp.where(past, gate, NEG_INF)
        rank = jnp.zeros((nb, blk), F32)
        for m in range(nb):
            gm = gate[m:m + 1, :]
            beats = jnp.where(gm > gate, 1.0, jnp.where(gm == gate, jnp.where(m < blk_id, 1.0, 0.0), 0.0))
            rank = rank + beats
        sel_scr[hh] = jnp.where(past, jnp.where(rank < k_sel, 1.0, 0.0), 0.0)

    def qk_pass(n, ms, own):
        out = []
        for hh in range(N_HEADS):
            st = lax.dot_general(k_ref[0, hh, n], q_ref[0, hh], nt_dims, preferred_element_type=F32)
            allowed = causal if own else sel_scr[hh, pl.ds(n, 1), :] > 0.5
            st = jnp.where(allowed, st, NEG_INF)
            st_scr[hh, n] = st
            out.append(jnp.maximum(ms[hh], jnp.max(st, axis=0, keepdims=True)))
        return tuple(out)

    m_init = tuple(jnp.full((1, blk), NEG_INF, F32) for _ in range(N_HEADS))
    m_own = qk_pass(j, m_init, True)
    m_fin = lax.fori_loop(0, j, lambda n, ms: qk_pass(n, ms, False), m_own)

    def pv_pass(n, ls, first):
        out = []
        for hh in range(N_HEADS):
            p = jnp.exp2(st_scr[hh, n] - m_fin[hh])
            out.append(ls[hh] + jnp.sum(p, axis=0, keepdims=True))
            pv = jnp.dot(vt_ref[0, hh, n], p.astype(BF16), preferred_element_type=F32)
            acc_scr[hh] = pv if first else acc_scr[hh] + pv
        return tuple(out)

    l_own = pv_pass(j, tuple(jnp.zeros((1, blk), F32) for _ in range(N_HEADS)), True)
    l_fin = lax.fori_loop(0, j, lambda n, ls: pv_pass(n, ls, False), l_own)
    for hh in range(N_HEADS):
        o_t = acc_scr[hh] / l_fin[hh]
        o_scr[:, hh * HEAD_DIM:(hh + 1) * HEAD_DIM] = o_t.T.astype(BF16)

    mix = jnp.dot(o_scr[...], wo_ref[...], preferred_element_type=F32)
    h3 = h_ref[0] + _rms(mix, g1_ref[...])
    o_ref[0] = h3
    f = _rms(h3, g2_ref[...])
    f_ref[0] = f.astype(BF16)
    e_ref[0], wcol_ref[0] = _route(f, wrt_ref, br_ref)


def _moba_layer(q, k5, vt5, kmean, h2, g1, wo, g2, wrt, br):
    B, S, D = h2.shape
    H, Dh, blk = N_HEADS, HEAD_DIM, MOBA_BLOCK
    nb = S // blk
    E = wrt.shape[0]
    kern = functools.partial(_moba_kernel, nb=nb, k_sel=min(MOBA_TOPK, nb))
    return pl.pallas_call(
        kern,
        grid=(B, nb),
        in_specs=[
            pl.BlockSpec((1, H, blk, Dh), lambda b, j: (b, 0, j, 0)),
            pl.BlockSpec((1, H, nb, blk, Dh), lambda b, j: (b, 0, 0, 0, 0)),
            pl.BlockSpec((1, H, nb, Dh, blk), lambda b, j: (b, 0, 0, 0, 0)),
            pl.BlockSpec((1, H, KMEAN_PIECES * nb, Dh), lambda b, j: (b, 0, 0, 0)),
            pl.BlockSpec((1, blk, D), lambda b, j: (b, j, 0)),
            _const_spec((1, D)), _const_spec((D, D)),
            _const_spec((1, D)), _const_spec((E, D)), _const_spec((E, 1)),
        ],
        out_specs=[
            pl.BlockSpec((1, blk, D), lambda b, j: (b, j, 0)),
            pl.BlockSpec((1, blk, D), lambda b, j: (b, j, 0)),
            pl.BlockSpec((1, 2, blk), lambda b, j: (b * nb + j, 0, 0)),
            pl.BlockSpec((1, blk, LANES), lambda b, j: (b * nb + j, 0, 0)),
        ],
        out_shape=[
            jax.ShapeDtypeStruct((B, S, D), F32),
            jax.ShapeDtypeStruct((B, S, D), BF16),
            jax.ShapeDtypeStruct((B * nb, 2, blk), jnp.int32),
            jax.ShapeDtypeStruct((B * nb, blk, LANES), F32),
        ],
        scratch_shapes=[
            pltpu.VMEM((H, nb, blk), F32), pltpu.VMEM((H, nb, blk, blk), F32),
            pltpu.VMEM((H, Dh, blk), F32), pltpu.VMEM((blk, D), BF16),
        ],
        compiler_params=_cparams("parallel", "arbitrary"),
        name="moba_attn",
    )(q, k5, vt5, kmean, h2, g1, wo, g2, wrt, br)


def _route(f, wrt_ref, br_ref):
    nt_dims = (((1,), (1,)), ((), ()))
    w = wrt_ref[...]
    ne = w.shape[0]
    w_hi = w.astype(BF16).astype(F32)
    f_hi = f.astype(BF16)
    f_lo = (f - f_hi.astype(F32)).astype(BF16)
    w2 = jnp.concatenate([w_hi, w - w_hi], axis=0).astype(BF16)
    a = lax.dot_general(w2, f_hi, nt_dims, preferred_element_type=F32)
    b = lax.dot_general(w2[:ne], f_lo, nt_dims, preferred_element_type=F32)
    logit = a[:ne] + a[ne:] + b + br_ref[...]
    t = logit.shape[1]
    eidx = lax.broadcasted_iota(jnp.int32, (ne, t), 0)
    m1 = jnp.max(logit, axis=0, keepdims=True)
    e1 = jnp.min(jnp.where(logit == m1, eidx, ne), axis=0, keepdims=True)
    rest = jnp.where(eidx == e1, -jnp.inf, logit)
    m2 = jnp.max(rest, axis=0, keepdims=True)
    e2 = jnp.min(jnp.where(rest == m2, eidx, ne), axis=0, keepdims=True)
    ex = jnp.exp(m2 - m1)
    den = 1.0 + ex
    wpad = jnp.concatenate([1.0 / den, ex / den, jnp.zeros((LANES - 2, t), F32)], axis=0)
    return jnp.concatenate([e1, e2], axis=0), wpad.T


META_ROWS = 32


def _rank_kernel(e_ref, lpos_ref, lcol_ref, meta_ref, pad_ref, blke_ref, nused_ref, *, ne, rows_log2, nblk_pad):
    ntile, _, t = e_ref.shape
    eidx = lax.broadcasted_iota(jnp.int32, (ne, t), 0)

    def onehots(i):
        e = e_ref[i]
        oh0 = jnp.where(eidx == e[0:1, :], 1.0, 0.0)
        oh1 = jnp.where(eidx == e[1:2, :], 1.0, 0.0)
        return oh0, oh1

    def excl_cumsum(col):
        run = jnp.zeros((1, 1), jnp.int32)
        parts = []
        for e in range(ne):
            parts.append(run)
            run = run + col[e:e + 1, :]
        return jnp.concatenate(parts, axis=0), run

    def count(i, cnt):
        oh0, oh1 = onehots(i)
        return cnt + jnp.sum(oh0 + oh1, axis=1, keepdims=True).astype(jnp.int32)

    cnt = lax.fori_loop(0, ntile, count, jnp.zeros((ne, 1), jnp.int32))
    pcnt = ((cnt + ((1 << rows_log2) - 1)) >> rows_log2) << rows_log2
    pstart, ptotal = excl_cumsum(pcnt)
    pend = pstart + pcnt

    upper = jnp.where(lax.broadcasted_iota(jnp.int32, (t, t), 0) < lax.broadcasted_iota(jnp.int32, (t, t), 1),
                      1.0, 0.0).astype(BF16)
    lane_b = lambda col: jnp.broadcast_to(col, (ne, LANES))

    def place(i, base):
        oh0, oh1 = onehots(i)
        both = oh0 + oh1
        before = jnp.dot(both.astype(BF16), upper, preferred_element_type=F32).astype(jnp.int32)
        cnt_i = jnp.sum(both, axis=1, keepdims=True).astype(jnp.int32)
        loff, _ = excl_cumsum(cnt_i)
        slot = loff + before
        l0 = jnp.sum(jnp.where(oh0 > 0.5, slot, 0), axis=0, keepdims=True)
        l1 = jnp.sum(jnp.where(oh1 > 0.5, slot, 0), axis=0, keepdims=True)
        lpos_ref[i] = jnp.concatenate([l0, l1], axis=0)
        lpad = jnp.concatenate([l0.astype(F32), l1.astype(F32), jnp.zeros((LANES - 2, t), F32)], axis=0)
        lcol_ref[i] = lpad.T
        meta_ref[i] = jnp.concatenate(
            [lane_b(cnt_i), lane_b(loff), lane_b(pstart + base),
             jnp.zeros((META_ROWS - 3 * ne, LANES), jnp.int32)], axis=0)
        return base + cnt_i

    lax.fori_loop(0, ntile, place, jnp.zeros((ne, 1), jnp.int32))

    pad_ref[...] = jnp.concatenate([lane_b(pstart + cnt), lane_b(pcnt - cnt),
                                    jnp.broadcast_to(ptotal, (ne, LANES))], axis=0)
    blk_row = lax.broadcasted_iota(jnp.int32, (ne, nblk_pad), 1) << rows_log2
    blke = jnp.sum(jnp.where(pend <= blk_row, 1, 0), axis=0, keepdims=True)
    blke_ref[...] = jnp.minimum(blke, ne - 1)
    nused_ref[...] = jnp.broadcast_to(ptotal >> rows_log2, (1, LANES))


def _rank(e3, ne, nblk_pad):
    ntile, _, t = e3.shape
    kern = functools.partial(_rank_kernel, ne=ne, rows_log2=int(math.log2(MOE_ROWS)), nblk_pad=nblk_pad)
    return pl.pallas_call(
        kern,
        out_shape=[
            jax.ShapeDtypeStruct((ntile, 2, t), jnp.int32),
            jax.ShapeDtypeStruct((ntile, t, LANES), F32),
            jax.ShapeDtypeStruct((ntile, META_ROWS, LANES), jnp.int32),
            jax.ShapeDtypeStruct((3 * ne, LANES), jnp.int32),
            jax.ShapeDtypeStruct((1, nblk_pad), jnp.int32),
            jax.ShapeDtypeStruct((1, LANES), jnp.int32),
        ],
        compiler_params=pltpu.CompilerParams(vmem_limit_bytes=VMEM_LIMIT),
        name="moe_rank",
    )(e3)


SUBROWS = 8


def _rows_to_interleaved(ref, x):
    n = x.shape[0]
    for s in range(SUBROWS):
        ref[pl.ds(s, n, stride=SUBROWS), :] = x[:, s * LANES:(s + 1) * LANES]


def _interleaved_to_rows(ref, n):
    return jnp.concatenate([ref[pl.ds(s, n, stride=SUBROWS), :] for s in range(SUBROWS)], axis=1)


def _segment_copies(count, src_ref, src_row, dst_ref, dst_row, sem, max_piece, wait=False, advance_src=True):
    p = max_piece
    while p >= 1:
        hit = (count & p) != 0

        @pl.when(hit)
        def _(p=p, src_row=src_row, dst_row=dst_row):
            cp = pltpu.make_async_copy(
                src_ref.at[pl.ds(pl.multiple_of(src_row * SUBROWS, SUBROWS), p * SUBROWS), :],
                dst_ref.at[pl.ds(pl.multiple_of(dst_row * SUBROWS, SUBROWS), p * SUBROWS), :], sem)
            if wait:
                cp.wait()
            else:
                cp.start()

        step = count & p
        if advance_src:
            src_row = src_row + step
        dst_row = dst_row + step
        p //= 2


def _dispatch_kernel(meta_ref, pad_ref, lpos_ref, f_ref, xs_ref, srt, zbuf, sems, zsem, *, td, ne):
    i = pl.program_id(0)

    @pl.when(i == 0)
    def _():
        zbuf[...] = jnp.zeros_like(zbuf)
        zrows = zbuf.shape[0] // SUBROWS
        total = xs_ref.shape[0] // SUBROWS
        for wait in (False, True):
            for e in range(ne):
                _segment_copies(pad_ref[ne + e], zbuf, 0, xs_ref, pad_ref[e], zsem,
                                zrows, wait=wait, advance_src=False)
            for r in range(0, ne * MOE_ROWS, zrows):
                row = pad_ref[2 * ne] + r

                @pl.when(row < total)
                def _(row=row):
                    cp = pltpu.make_async_copy(
                        zbuf, xs_ref.at[pl.ds(pl.multiple_of(row * SUBROWS, SUBROWS), zrows * SUBROWS), :], zsem)
                    if wait:
                        cp.wait()
                    else:
                        cp.start()

    slot = i % 2
    lp = lpos_ref[0]
    r = lax.broadcasted_iota(jnp.int32, (2 * td, td), 0)
    perm = (jnp.where(r == lp[0:1, :], 1.0, 0.0) + jnp.where(r == lp[1:2, :], 1.0, 0.0)).astype(BF16)
    _rows_to_interleaved(srt.at[slot], jnp.dot(perm, f_ref[...], preferred_element_type=F32))
    for e in range(ne):
        _segment_copies(meta_ref[0, 0, e], srt.at[slot], meta_ref[0, 0, ne + e], xs_ref,
                        meta_ref[0, 0, 2 * ne + e], sems.at[slot], td)

    def drain(s):
        for h in range(2):
            pltpu.make_async_copy(srt.at[s, pl.ds(h * td * SUBROWS, td * SUBROWS), :],
                                  xs_ref.at[pl.ds(0, td * SUBROWS), :], sems.at[s]).wait()

    @pl.when(i > 0)
    def _():
        drain(1 - slot)

    @pl.when(i + 1 == pl.num_programs(0))
    def _():
        drain(slot)


def _dispatch(meta, pad, lpos, f, total):
    N, D = f.shape
    ntile, _, td = lpos.shape
    ne = N_EXPERTS
    assert D == SUBROWS * LANES
    kern = functools.partial(_dispatch_kernel, td=td, ne=ne)
    return pl.pallas_call(
        kern,
        grid=(ntile,),
        in_specs=[
            pl.BlockSpec((1, 1, META_ROWS), lambda i: (i, 0, 0), memory_space=pltpu.SMEM),
            pl.BlockSpec(memory_space=pltpu.SMEM),
            pl.BlockSpec((1, 2, td), lambda i: (i, 0, 0)),
            pl.BlockSpec((td, D), lambda i: (i, 0)),
        ],
        out_specs=pl.BlockSpec(memory_space=pl.ANY),
        out_shape=jax.ShapeDtypeStruct((total * SUBROWS, LANES), F32),
        scratch_shapes=[pltpu.VMEM((2, 2 * td * SUBROWS, LANES), F32),
                        pltpu.VMEM((MOE_ROWS // 2 * SUBROWS, LANES), F32),
                        pltpu.SemaphoreType.DMA((2,)), pltpu.SemaphoreType.DMA(())],
        compiler_params=_cparams("arbitrary"),
        name="moe_dispatch",
    )(meta, pad, lpos, f)


def _expert_kernel(blke_ref, nused_ref, xs_ref, w1_ref, w3_ref, w2_ref, y_ref, *, rb):
    i = pl.program_id(0)

    @pl.when(i < nused_ref[0])
    def _():
        x = _interleaved_to_rows(xs_ref, rb).astype(BF16)
        a = jnp.dot(x, w1_ref[0], preferred_element_type=F32)
        b = jnp.dot(x, w3_ref[0], preferred_element_type=F32)
        act = (jax.nn.silu(a) * b).astype(BF16)
        _rows_to_interleaved(y_ref, jnp.dot(act, w2_ref[0], preferred_element_type=F32))

    @pl.when(i >= nused_ref[0])
    def _():
        y_ref[...] = jnp.zeros_like(y_ref)


def _experts(blke, nused, xs, w1, w3, w2):
    E, D, Fe = w1.shape
    total = xs.shape[0] // SUBROWS
    rb = MOE_ROWS
    nblk = total // rb
    wspec = lambda shape: pl.BlockSpec(shape, lambda i, be, nu: (be[i], 0, 0), pipeline_mode=pl.Buffered(1))
    return pl.pallas_call(
        functools.partial(_expert_kernel, rb=rb),
        grid_spec=pltpu.PrefetchScalarGridSpec(
            num_scalar_prefetch=2,
            grid=(nblk,),
            in_specs=[
                pl.BlockSpec((rb * SUBROWS, LANES), lambda i, be, nu: (jnp.minimum(i, nu[0] - 1), 0)),
                wspec((1, D, Fe)), wspec((1, D, Fe)), wspec((1, Fe, D)),
            ],
            out_specs=pl.BlockSpec((rb * SUBROWS, LANES), lambda i, be, nu: (i, 0)),
        ),
        out_shape=jax.ShapeDtypeStruct((total * SUBROWS, LANES), F32),
        compiler_params=_cparams("arbitrary"),
        name="moe_experts",
    )(blke, nused, xs, w1, w3, w2)


def _combine_kernel(meta_ref, nxt_ref, yb_ref, lcol_ref, wcol_ref, h_ref, g3_ref, o_ref, buf, sems, *, td, ne):
    i = pl.program_id(0)
    slot = i % 2

    def fetch(m_ref, s):
        for e in range(ne):
            _segment_copies(m_ref[0, 0, e], yb_ref, m_ref[0, 0, 2 * ne + e], buf.at[s], m_ref[0, 0, ne + e],
                            sems.at[s], td)

    @pl.when(i == 0)
    def _():
        fetch(meta_ref, 0)

    @pl.when(i + 1 < pl.num_programs(0))
    def _():
        fetch(nxt_ref, 1 - slot)

    for h in range(2):
        pltpu.make_async_copy(yb_ref.at[pl.ds(0, td * SUBROWS), :],
                              buf.at[slot, pl.ds(h * td * SUBROWS, td * SUBROWS), :], sems.at[slot]).wait()
    yb = _interleaved_to_rows(buf.at[slot], 2 * td).astype(BF16)
    lc = lcol_ref[0]
    w = wcol_ref[...]
    c = lax.broadcasted_iota(jnp.int32, (td, 2 * td), 1).astype(F32)
    pick0 = jnp.where(c == lc[:, 0:1], 1.0, 0.0).astype(BF16)
    pick1 = jnp.where(c == lc[:, 1:2], 1.0, 0.0).astype(BF16)
    f = (w[:, 0:1] * jnp.dot(pick0, yb, preferred_element_type=F32)
         + w[:, 1:2] * jnp.dot(pick1, yb, preferred_element_type=F32))
    o_ref[...] = h_ref[...] + _rms(f, g3_ref[...])


def _combine(meta, yb, lcol, wcol, h3, g3):
    N, D = h3.shape
    ntile, td, _ = lcol.shape
    ne = N_EXPERTS
    kern = functools.partial(_combine_kernel, td=td, ne=ne)
    return pl.pallas_call(
        kern,
        grid=(ntile,),
        in_specs=[
            pl.BlockSpec((1, 1, META_ROWS), lambda i: (i, 0, 0), memory_space=pltpu.SMEM),
            pl.BlockSpec((1, 1, META_ROWS), lambda i: (jnp.minimum(i + 1, ntile - 1), 0, 0),
                         memory_space=pltpu.SMEM),
            pl.BlockSpec(memory_space=pl.ANY),
            pl.BlockSpec((1, td, LANES), lambda i: (i, 0, 0)),
            pl.BlockSpec((td, LANES), lambda i: (i, 0)),
            pl.BlockSpec((td, D), lambda i: (i, 0)),
            _const_spec((1, D)),
        ],
        out_specs=pl.BlockSpec((td, D), lambda i: (i, 0)),
        out_shape=jax.ShapeDtypeStruct((N, D), F32),
        scratch_shapes=[pltpu.VMEM((2, 2 * td * SUBROWS, LANES), F32), pltpu.SemaphoreType.DMA((2,))],
        compiler_params=_cparams("arbitrary"),
        name="moe_combine",
    )(meta, meta, yb, lcol, wcol, h3, g3)


def _rope_lane_tables():
    half = ROT_DIM // 2
    inv_freq = 1.0 / (ROPE_THETA ** (np.arange(half, dtype=np.float32) / half))
    invf = np.zeros((1, LANES), np.float32)
    invf[0, :half] = inv_freq
    invf[0, half:ROT_DIM] = inv_freq
    sneg = np.zeros((1, LANES), np.float32)
    sneg[0, :half] = -1.0
    spos = np.zeros((1, LANES), np.float32)
    spos[0, half:ROT_DIM] = 1.0
    return jnp.asarray(invf), jnp.asarray(sneg), jnp.asarray(spos)


def kernel(x, positions, norms, s5_lam_re, s5_lam_im, s5_log_dt, s5_b_re, s5_b_im, s5_c_re, s5_c_im, s5_d, s5_w_glu, kv_norm, w_kv, w_q, w_o, ffn_w1, ffn_w3, ffn_w2, moe_router, moe_bias, moe_w1, moe_w3, moe_w2):
    B, S, D = x.shape
    N = B * S
    HD = N_HEADS * HEAD_DIM
    E = moe_router.shape[-1]
    row = lambda v: v.astype(F32).reshape(1, -1)

    a2_r, a2_i, b2, c2, d2 = _s5_params(s5_lam_re[0], s5_lam_im[0], s5_log_dt[0], s5_b_re[0], s5_b_im[0],
                                        s5_c_re[0], s5_c_im[0])
    h1 = _s5_layer(x, row(norms[0, 0]), row(norms[0, 1]), row(s5_d[0]), a2_r, a2_i, b2, c2, d2,
                   s5_w_glu[0].astype(BF16))
    h2 = _dense_ffn(h1.reshape(N, D), row(norms[0, 2]), row(norms[0, 3]),
                    ffn_w1[0].astype(BF16), ffn_w3[0].astype(BF16), ffn_w2[0].astype(BF16))

    invf, sneg, spos = _rope_lane_tables()
    posb = jnp.broadcast_to(positions.astype(F32)[:, :, None], (B, S, LANES))
    q, k5, vt5, kmean = _qkv(h2.reshape(B, S, D), posb, invf, sneg, spos, row(norms[1, 0]), row(kv_norm),
                             w_q[0].astype(BF16), w_kv[:, :HD].astype(BF16), w_kv[:, HD:].T.astype(BF16))
    h3, f_in, e_tiles, wcol = _moba_layer(q, k5, vt5, kmean, h2.reshape(B, S, D), row(norms[1, 1]), w_o[0].astype(BF16),
                                          row(norms[1, 2]), moe_router[0].T.astype(F32),
                                          moe_bias[0].astype(F32).reshape(E, 1))
    h3 = h3.reshape(N, D)

    assert MOBA_BLOCK == DISPATCH_TILE
    total = 2 * N + E * MOE_ROWS
    nblk = total // MOE_ROWS
    nblk_pad = -(-nblk // LANES) * LANES
    lpos, lcol, meta, pad, blke, nused = _rank(e_tiles, E, nblk_pad)
    meta = meta[:, :, 0].reshape(-1, 1, META_ROWS)
    xs = _dispatch(meta, pad[:, 0], lpos, f_in.reshape(N, D), total)
    yb = _experts(blke[0, :nblk], nused[0, :1], xs,
                  moe_w1[0].astype(BF16), moe_w3[0].astype(BF16), moe_w2[0].astype(BF16))
    out = _combine(meta, yb, lcol, wcol.reshape(N, LANES), h3, row(norms[1, 3]))
    return out.reshape(B, S, D)
```

```python
import functools
import math

import numpy as np
import jax
import jax.numpy as jnp
from jax import lax
from jax.experimental import pallas as pl
from jax.experimental.pallas import tpu as pltpu

F32 = jnp.float32
BF16 = jnp.bfloat16

NORM_EPS = 1e-6
NEG_INF = -1e30

SSM_GROUP = 16
N_HEADS = 8
HEAD_DIM = 128
ROT_DIM = HEAD_DIM // 4
ROPE_THETA = 500000.0
MOBA_BLOCK = 256
MOBA_TOPK = 3
N_EXPERTS = 8

Q_SCALE_LOG2 = HEAD_DIM ** -0.5 * math.log2(math.e)

KMEAN_PIECES = 3

LANES = 128
GROUPS_PER_TILE = LANES // SSM_GROUP

VMEM_LIMIT = 56 * 1024 * 1024

S5_CHUNK = 64
S5_ROW_PAD = 8
ROW_TILE = 512
QKV_TILE = 512
MOE_ROWS = 512
DISPATCH_TILE = 256


def _cparams(*sem):
    return pltpu.CompilerParams(dimension_semantics=sem, vmem_limit_bytes=VMEM_LIMIT)


def _rms(x, g):
    return x * lax.rsqrt(jnp.mean(x * x, axis=-1, keepdims=True) + NORM_EPS) * g


def _const_spec(shape):
    n = len(shape)
    return pl.BlockSpec(shape, lambda *_: (0,) * n, pipeline_mode=pl.Buffered(1))


def _zoh(lr, li, dt):
    zr = lr * dt
    zi = li * dt
    mag = jnp.exp(zr)
    return mag * jnp.cos(zr * 0 + zi), mag * jnp.sin(zi)


def _cmul(xr, xi, yr, yi):
    return xr * yr - xi * yi, xr * yi + xi * yr


def _s5_prep_kernel(lrb_ref, lib_ref, lrc_ref, lic_ref, ldt_ref, br_ref, bi_ref, cr_ref, ci_ref,
                    a2r_ref, a2i_ref, bbr_ref, bbi_ref, abr_ref, abi_ref, car_ref, cai_ref, c2r_ref, c2i_ref):
    dt = jnp.exp(ldt_ref[...])
    lr = lrb_ref[...]
    li = lib_ref[...]
    a_r, a_i = _zoh(lr, li, dt)
    den = lr * lr + li * li
    coef_r = ((a_r - 1.0) * lr + a_i * li) / den
    coef_i = (a_i * lr - (a_r - 1.0) * li) / den
    bb_r, bb_i = _cmul(coef_r, coef_i, br_ref[...], bi_ref[...])
    bbr_ref[...] = bb_r
    bbi_ref[...] = bb_i
    abr_ref[...], abi_ref[...] = _cmul(a_r, a_i, bb_r, bb_i)
    a2r_ref[...], a2i_ref[...] = _cmul(a_r, a_i, a_r, a_i)
    ac_r, ac_i = _zoh(lrc_ref[...], lic_ref[...], dt)
    ca_r, ca_i = _cmul(cr_ref[...], ci_ref[...], ac_r, ac_i)
    car_ref[...] = ca_r
    cai_ref[...] = ca_i
    c2r_ref[...], c2i_ref[...] = _cmul(ca_r, ca_i, ac_r, ac_i)


def _s5_fold_kernel(b_ref, ab_ref, c_ref, d0_ref, d1_ref):
    for k in range(b_ref.shape[0]):
        d0_ref[k] = jnp.dot(b_ref[k], c_ref[k], precision=lax.Precision.HIGHEST, preferred_element_type=F32)
        d1_ref[k] = jnp.dot(ab_ref[k], c_ref[k], precision=lax.Precision.HIGHEST, preferred_element_type=F32)


def _s5_params(lam_re, lam_im, log_dt, b_re, b_im, c_re, c_im):
    G, P = lam_re.shape
    Q = b_re.shape[-1]
    gt = GROUPS_PER_TILE
    nt = G // gt
    f = lambda a: a.astype(F32)
    rep = lambda a: jnp.repeat(f(a), Q, axis=1)
    til = lambda a: jnp.tile(f(a), (1, Q))
    out = jax.ShapeDtypeStruct((G, P * Q), F32)
    a2_r, a2_i, bb_r, bb_i, ab_r, ab_i, ca_r, ca_i, c2_r, c2_i = pl.pallas_call(
        _s5_prep_kernel, out_shape=(out,) * 10, name="s5_prep",
    )(rep(lam_re), rep(lam_im), til(lam_re), til(lam_im), f(log_dt).reshape(G, 1),
      f(b_re).reshape(G, P * Q), f(b_im).reshape(G, P * Q), f(c_re).reshape(G, Q * P), f(c_im).reshape(G, Q * P))

    eye = jnp.eye(gt, dtype=F32)
    def bmat(bb):
        return jnp.einsum('kgpi,gh->kgihp', bb.reshape(nt, gt, P, Q), eye).reshape(nt, gt * Q, gt * P)
    def cmat(c):
        return jnp.einsum('kgip,gh->kgphi', c.reshape(nt, gt, Q, P), eye).reshape(nt, gt * P, gt * Q)
    bmat2 = lambda r, i: jnp.concatenate([bmat(r), bmat(i)], axis=-1)
    cmat2 = lambda r, i: jnp.concatenate([cmat(r), -cmat(i)], axis=1)
    b_bd, ab_bd = bmat2(bb_r, bb_i), bmat2(ab_r, ab_i)
    c_bd = cmat2(f(c_re).reshape(G, Q * P), f(c_im).reshape(G, Q * P))
    dshape = jax.ShapeDtypeStruct((nt, LANES, LANES), F32)
    d0, d1 = pl.pallas_call(_s5_fold_kernel, out_shape=(dshape, dshape), name="s5_fold",
                            compiler_params=pltpu.CompilerParams(vmem_limit_bytes=VMEM_LIMIT))(b_bd, ab_bd, c_bd)
    b2 = jnp.concatenate([ab_bd, b_bd], axis=1).astype(BF16)
    c2 = jnp.concatenate([cmat2(ca_r, ca_i), cmat2(c2_r, c2_i)], axis=2).astype(BF16)
    d2 = jnp.concatenate([jnp.concatenate([d0, d1], axis=2),
                          jnp.concatenate([jnp.zeros_like(d0), d0], axis=2)], axis=1).astype(BF16)
    sel = lambda a: a.reshape(G, P, Q)[:, :, 0].reshape(nt, 1, gt * P)
    return sel(a2_r), sel(a2_i), b2, c2, d2


def _s5_layer_kernel(x_ref, g0_ref, g1_ref, d_ref, ar_ref, ai_ref, b2_ref, c2_ref, d2_ref, wglu_ref,
                     o_ref, bm, tm, bu, y_tm, state, *, nb, lc, d, nt, hw):
    c = pl.program_id(0)
    pitch = lc + S5_ROW_PAD
    npair = lc // 2

    @pl.when(c == 0)
    def _():
        state[...] = jnp.zeros_like(state)

    for b in range(nb):
        u = _rms(x_ref[b], g0_ref[...])
        for k in range(nt):
            bm[k, b * pitch:b * pitch + lc, :] = u[:, k * LANES:(k + 1) * LANES]
    for t in range(lc):
        for k in range(nt):
            tm[k, t * nb:(t + 1) * nb, :] = bm[k, pl.ds(t, nb, stride=pitch), :]

    for k in range(nt):
        uk = tm[k]
        u4 = uk.reshape(npair, 2, nb, LANES)
        u2 = jnp.concatenate([u4[:, 0].reshape(npair * nb, LANES), u4[:, 1].reshape(npair * nb, LANES)],
                             axis=1).astype(BF16)
        bu[...] = jnp.dot(u2, b2_ref[k], preferred_element_type=F32)
        a_r = jnp.broadcast_to(ar_ref[k], (nb, hw))
        a_i = jnp.broadcast_to(ai_ref[k], (nb, hw))

        def step(t, carry):
            s_r, s_i = carry
            r0 = pl.multiple_of(t * nb, nb)
            n_r = a_r * s_r - a_i * s_i + bu[pl.ds(r0, nb), 0:hw]
            n_i = a_r * s_i + a_i * s_r + bu[pl.ds(r0, nb), hw:2 * hw]
            bu[pl.ds(r0, nb), 0:hw] = s_r
            bu[pl.ds(r0, nb), hw:2 * hw] = s_i
            return n_r, n_i

        s_r, s_i = lax.fori_loop(0, npair, step, (state[k, :, 0:hw], state[k, :, hw:2 * hw]), unroll=True)
        state[k, :, 0:hw] = s_r
        state[k, :, hw:2 * hw] = s_i
        y2 = (jnp.dot(bu[...].astype(BF16), c2_ref[k], preferred_element_type=F32)
              + jnp.dot(u2, d2_ref[k], preferred_element_type=F32))
        yk = jnp.concatenate([y2[:, :LANES].reshape(npair, 1, nb, LANES),
                              y2[:, LANES:].reshape(npair, 1, nb, LANES)], axis=1).reshape(lc * nb, LANES)
        y_tm[:, k * LANES:(k + 1) * LANES] = yk + d_ref[:, k * LANES:(k + 1) * LANES] * uk

    gl = jax.nn.gelu(y_tm[...]).astype(BF16)
    z = jnp.dot(gl, wglu_ref[...], preferred_element_type=F32)
    mix = z[:, :d] * jax.nn.sigmoid(z[:, d:])
    mixn = _rms(mix, g1_ref[...])
    for k in range(nt):
        tm[k] = mixn[:, k * LANES:(k + 1) * LANES]
    for t in range(lc):
        for k in range(nt):
            bm[k, pl.ds(t, nb, stride=pitch), :] = tm[k, t * nb:(t + 1) * nb, :]
    for b in range(nb):
        for k in range(nt):
            sl = slice(k * LANES, (k + 1) * LANES)
            o_ref[b, :, sl] = x_ref[b, :, sl] + bm[k, b * pitch:b * pitch + lc, :]


def _s5_layer(x, g0, g1, d_skip, a2_r_t, a2_i_t, b2, c2, d2, w_glu):
    B, S, D = x.shape
    lc = S5_CHUNK
    nt = D // LANES
    hw = a2_r_t.shape[-1]
    rows = B * lc
    kern = functools.partial(_s5_layer_kernel, nb=B, lc=lc, d=D, nt=nt, hw=hw)
    return pl.pallas_call(
        kern,
        grid=(S // lc,),
        in_specs=[
            pl.BlockSpec((B, lc, D), lambda c: (0, c, 0)),
            _const_spec((1, D)), _const_spec((1, D)), _const_spec((1, D)),
            _const_spec((nt, 1, hw)), _const_spec((nt, 1, hw)),
            _const_spec((nt, 2 * LANES, 2 * hw)), _const_spec((nt, 2 * hw, 2 * LANES)),
            _const_spec((nt, 2 * LANES, 2 * LANES)),
            _const_spec((D, 2 * D)),
        ],
        out_specs=pl.BlockSpec((B, lc, D), lambda c: (0, c, 0)),
        out_shape=jax.ShapeDtypeStruct((B, S, D), F32),
        scratch_shapes=[
            pltpu.VMEM((nt, B * (lc + S5_ROW_PAD), LANES), F32), pltpu.VMEM((nt, rows, LANES), F32),
            pltpu.VMEM((rows // 2, 2 * hw), F32), pltpu.VMEM((rows, D), F32),
            pltpu.VMEM((nt, B, 2 * hw), F32),
        ],
        compiler_params=_cparams("arbitrary"),
        name="s5_layer",
    )(x, g0, g1, d_skip, a2_r_t, a2_i_t, b2, c2, d2, w_glu)


def _ffn_kernel(h_ref, g2_ref, g3_ref, w1_ref, w3_ref, w2_ref, o_ref):
    h = h_ref[...]
    f_in = _rms(h, g2_ref[...]).astype(BF16)
    a = jnp.dot(f_in, w1_ref[...], preferred_element_type=F32)
    b = jnp.dot(f_in, w3_ref[...], preferred_element_type=F32)
    act = (jax.nn.silu(a) * b).astype(BF16)
    f = jnp.dot(act, w2_ref[...], preferred_element_type=F32)
    o_ref[...] = h + _rms(f, g3_ref[...])


def _dense_ffn(h, g2, g3, w1, w3, w2):
    N, D = h.shape
    Fd = w1.shape[1]
    tm = ROW_TILE
    return pl.pallas_call(
        _ffn_kernel,
        grid=(N // tm,),
        in_specs=[
            pl.BlockSpec((tm, D), lambda i: (i, 0)),
            _const_spec((1, D)), _const_spec((1, D)),
            _const_spec((D, Fd)), _const_spec((D, Fd)), _const_spec((Fd, D)),
        ],
        out_specs=pl.BlockSpec((tm, D), lambda i: (i, 0)),
        out_shape=jax.ShapeDtypeStruct((N, D), F32),
        compiler_params=_cparams("parallel"),
        name="dense_ffn",
    )(h, g2, g3, w1, w3, w2)


def _rotary(xh, cs, s_neg, s_pos):
    half = ROT_DIM // 2
    return xh * cs + pltpu.roll(xh, LANES - half, 1) * s_neg + pltpu.roll(xh, half, 1) * s_pos


def _qkv_kernel(h_ref, pos_ref, invf_ref, sneg_ref, spos_ref, gq_ref, gkv_ref,
                wq_ref, wk_ref, wvt_ref, q_ref, k_ref, vt_ref, km_ref, *, nblk_tile, nblk_seq):
    ti = pl.program_id(1)
    h = h_ref[0]
    r = lax.rsqrt(jnp.mean(h * h, axis=-1, keepdims=True) + NORM_EPS)
    hq = (h * r * gq_ref[...]).astype(BF16)
    hk = (h * r * gkv_ref[...]).astype(BF16)
    ang = pos_ref[0] * invf_ref[...]
    cs = jnp.cos(ang)
    sn = jnp.sin(ang)
    s_neg = sn * sneg_ref[...]
    s_pos = sn * spos_ref[...]
    q = jnp.dot(hq, wq_ref[...], preferred_element_type=F32)
    k = jnp.dot(hk, wk_ref[...], preferred_element_type=F32)
    vt = lax.dot_general(wvt_ref[...], hk, (((1,), (1,)), ((), ())), preferred_element_type=F32)
    for hh in range(N_HEADS):
        sl = slice(hh * HEAD_DIM, (hh + 1) * HEAD_DIM)
        q_ref[0, hh] = (_rotary(q[:, sl], cs, s_neg, s_pos) * Q_SCALE_LOG2).astype(BF16)
        kr = _rotary(k[:, sl], cs, s_neg, s_pos)
        for j in range(nblk_tile):
            rs = slice(j * MOBA_BLOCK, (j + 1) * MOBA_BLOCK)
            k_ref[0, hh, j] = kr[rs].astype(BF16)
            vt_ref[0, hh, j] = vt[sl, rs].astype(BF16)
            km = jnp.mean(kr[rs], axis=0, keepdims=True)
            for piece in range(KMEAN_PIECES):
                part = km.astype(BF16).astype(F32)
                km_ref[0, hh, pl.ds(piece * nblk_seq + ti * nblk_tile + j, 1), :] = part
                km = km - part


def _qkv(h2, posb, invf, sneg, spos, gq, gkv, wq, wk, wvt):
    B, S, D = h2.shape
    T = QKV_TILE
    H, Dh, blk = N_HEADS, HEAD_DIM, MOBA_BLOCK
    nb = S // blk
    nbt = T // blk
    kern = functools.partial(_qkv_kernel, nblk_tile=nbt, nblk_seq=nb)
    return pl.pallas_call(
        kern,
        grid=(B, S // T),
        in_specs=[
            pl.BlockSpec((1, T, D), lambda b, t: (b, t, 0)),
            pl.BlockSpec((1, T, LANES), lambda b, t: (b, t, 0)),
            _const_spec((1, LANES)), _const_spec((1, LANES)), _const_spec((1, LANES)),
            _const_spec((1, D)), _const_spec((1, D)),
            _const_spec((D, D)), _const_spec((D, D)), _const_spec((D, D)),
        ],
        out_specs=[
            pl.BlockSpec((1, H, T, Dh), lambda b, t: (b, 0, t, 0)),
            pl.BlockSpec((1, H, nbt, blk, Dh), lambda b, t: (b, 0, t, 0, 0)),
            pl.BlockSpec((1, H, nbt, Dh, blk), lambda b, t: (b, 0, t, 0, 0)),
            pl.BlockSpec((1, H, KMEAN_PIECES * nb, Dh), lambda b, t: (b, 0, 0, 0)),
        ],
        out_shape=[
            jax.ShapeDtypeStruct((B, H, S, Dh), BF16),
            jax.ShapeDtypeStruct((B, H, nb, blk, Dh), BF16),
            jax.ShapeDtypeStruct((B, H, nb, Dh, blk), BF16),
            jax.ShapeDtypeStruct((B, H, KMEAN_PIECES * nb, Dh), F32),
        ],
        compiler_params=_cparams("parallel", "arbitrary"),
        name="qkv_proj",
    )(h2, posb, invf, sneg, spos, gq, gkv, wq, wk, wvt)


def _moba_kernel(q_ref, k_ref, vt_ref, km_ref, h_ref, g1_ref, wo_ref, g2_ref, wrt_ref, br_ref,
                 o_ref, f_ref, e_ref, wcol_ref, sel_scr, st_scr, acc_scr, o_scr, *, nb, k_sel):
    j = pl.program_id(1)
    blk = MOBA_BLOCK
    nt_dims = (((1,), (1,)), ((), ()))
    blk_id = lax.broadcasted_iota(jnp.int32, (nb, blk), 0)
    past = blk_id < j
    causal = (lax.broadcasted_iota(jnp.int32, (blk, blk), 0)
              <= lax.broadcasted_iota(jnp.int32, (blk, blk), 1))

    for hh in range(N_HEADS):
        parts = lax.dot_general(km_ref[0, hh].astype(BF16), q_ref[0, hh], nt_dims,
                                preferred_element_type=F32)
        gate = parts[0:nb]
        for piece in range(1, KMEAN_PIECES):
            gate = gate + parts[piece * nb:(piece + 1) * nb]
        gate = jnp.where(past, gate, NEG_INF)
        rank = jnp.zeros((nb, blk), F32)
        for m in range(nb):
            gm = gate[m:m + 1, :]
            beats = jnp.where(gm > gate, 1.0, jnp.where(gm == gate, jnp.where(m < blk_id, 1.0, 0.0), 0.0))
            rank = rank + beats
        sel_scr[hh] = jnp.where(past, jnp.where(rank < k_sel, 1.0, 0.0), 0.0)

    def qk_pass(n, ms, own):
        out = []
        for hh in range(N_HEADS):
            st = lax.dot_general(k_ref[0, hh, n], q_ref[0, hh], nt_dims, preferred_element_type=F32)
            allowed = causal if own else sel_scr[hh, pl.ds(n, 1), :] > 0.5
            st = jnp.where(allowed, st, NEG_INF)
            st_scr[hh, n] = st
            out.append(jnp.maximum(ms[hh], jnp.max(st, axis=0, keepdims=True)))
        return tuple(out)

    m_init = tuple(jnp.full((1, blk), NEG_INF, F32) for _ in range(N_HEADS))
    m_own = qk_pass(j, m_init, True)
    m_fin = lax.fori_loop(0, j, lambda n, ms: qk_pass(n, ms, False), m_own)

    def pv_pass(n, ls, first):
        out = []
        for hh in range(N_HEADS):
            p = jnp.exp2(st_scr[hh, n] - m_fin[hh])
            out.append(ls[hh] + jnp.sum(p, axis=0, keepdims=True))
            pv = jnp.dot(vt_ref[0, hh, n], p.astype(BF16), preferred_element_type=F32)
            acc_scr[hh] = pv if first else acc_scr[hh] + pv
        return tuple(out)

    l_own = pv_pass(j, tuple(jnp.zeros((1, blk), F32) for _ in range(N_HEADS)), True)
    l_fin = lax.fori_loop(0, j, lambda n, ls: pv_pass(n, ls, False), l_own)
    for hh in range(N_HEADS):
        o_t = acc_scr[hh] / l_fin[hh]
        o_scr[:, hh * HEAD_DIM:(hh + 1) * HEAD_DIM] = o_t.T.astype(BF16)

    mix = jnp.dot(o_scr[...], wo_ref[...], preferred_element_type=F32)
    h3 = h_ref[0] + _rms(mix, g1_ref[...])
    o_ref[0] = h3
    f = _rms(h3, g2_ref[...])
    f_ref[0] = f.astype(BF16)
    e_ref[0], wcol_ref[0] = _route(f, wrt_ref, br_ref)


def _moba_layer(q, k5, vt5, kmean, h2, g1, wo, g2, wrt, br):
    B, S, D = h2.shape
    H, Dh, blk = N_HEADS, HEAD_DIM, MOBA_BLOCK
    nb = S // blk
    E = wrt.shape[0]
    kern = functools.partial(_moba_kernel, nb=nb, k_sel=min(MOBA_TOPK, nb))
    return pl.pallas_call(
        kern,
        grid=(B, nb),
        in_specs=[
            pl.BlockSpec((1, H, blk, Dh), lambda b, j: (b, 0, j, 0)),
            pl.BlockSpec((1, H, nb, blk, Dh), lambda b, j: (b, 0, 0, 0, 0)),
            pl.BlockSpec((1, H, nb, Dh, blk), lambda b, j: (b, 0, 0, 0, 0)),
            pl.BlockSpec((1, H, KMEAN_PIECES * nb, Dh), lambda b, j: (b, 0, 0, 0)),
            pl.BlockSpec((1, blk, D), lambda b, j: (b, j, 0)),
            _const_spec((1, D)), _const_spec((D, D)),
            _const_spec((1, D)), _const_spec((E, D)), _const_spec((E, 1)),
        ],
        out_specs=[
            pl.BlockSpec((1, blk, D), lambda b, j: (b, j, 0)),
            pl.BlockSpec((1, blk, D), lambda b, j: (b, j, 0)),
            pl.BlockSpec((1, 2, blk), lambda b, j: (b * nb + j, 0, 0)),
            pl.BlockSpec((1, blk, LANES), lambda b, j: (b * nb + j, 0, 0)),
        ],
        out_shape=[
            jax.ShapeDtypeStruct((B, S, D), F32),
            jax.ShapeDtypeStruct((B, S, D), BF16),
            jax.ShapeDtypeStruct((B * nb, 2, blk), jnp.int32),
            jax.ShapeDtypeStruct((B * nb, blk, LANES), F32),
        ],
        scratch_shapes=[
            pltpu.VMEM((H, nb, blk), F32), pltpu.VMEM((H, nb, blk, blk), F32),
            pltpu.VMEM((H, Dh, blk), F32), pltpu.VMEM((blk, D), BF16),
        ],
        compiler_params=_cparams("parallel", "arbitrary"),
        name="moba_attn",
    )(q, k5, vt5, kmean, h2, g1, wo, g2, wrt, br)


def _route(f, wrt_ref, br_ref):
    nt_dims = (((1,), (1,)), ((), ()))
    w = wrt_ref[...]
    ne = w.shape[0]
    w_hi = w.astype(BF16).astype(F32)
    f_hi = f.astype(BF16)
    f_lo = (f - f_hi.astype(F32)).astype(BF16)
    w2 = jnp.concatenate([w_hi, w - w_hi], axis=0).astype(BF16)
    a = lax.dot_general(w2, f_hi, nt_dims, preferred_element_type=F32)
    b = lax.dot_general(w2[:ne], f_lo, nt_dims, preferred_element_type=F32)
    logit = a[:ne] + a[ne:] + b + br_ref[...]
    t = logit.shape[1]
    eidx = lax.broadcasted_iota(jnp.int32, (ne, t), 0)
    m1 = jnp.max(logit, axis=0, keepdims=True)
    e1 = jnp.min(jnp.where(logit == m1, eidx, ne), axis=0, keepdims=True)
    rest = jnp.where(eidx == e1, -jnp.inf, logit)
    m2 = jnp.max(rest, axis=0, keepdims=True)
    e2 = jnp.min(jnp.where(rest == m2, eidx, ne), axis=0, keepdims=True)
    ex = jnp.exp(m2 - m1)
    den = 1.0 + ex
    wpad = jnp.concatenate([1.0 / den, ex / den, jnp.zeros((LANES - 2, t), F32)], axis=0)
    return jnp.concatenate([e1, e2], axis=0), wpad.T


META_ROWS = 32


def _rank_kernel(e_ref, lpos_ref, lcol_ref, meta_ref, pad_ref, blke_ref, nused_ref, *, ne, rows_log2, nblk_pad):
    ntile, _, t = e_ref.shape
    eidx = lax.broadcasted_iota(jnp.int32, (ne, t), 0)

    def onehots(i):
        e = e_ref[i]
        oh0 = jnp.where(eidx == e[0:1, :], 1.0, 0.0)
        oh1 = jnp.where(eidx == e[1:2, :], 1.0, 0.0)
        return oh0, oh1

    def excl_cumsum(col):
        run = jnp.zeros((1, 1), jnp.int32)
        parts = []
        for e in range(ne):
            parts.append(run)
            run = run + col[e:e + 1, :]
        return jnp.concatenate(parts, axis=0), run

    def count(i, cnt):
        oh0, oh1 = onehots(i)
        return cnt + jnp.sum(oh0 + oh1, axis=1, keepdims=True).astype(jnp.int32)

    cnt = lax.fori_loop(0, ntile, count, jnp.zeros((ne, 1), jnp.int32), unroll=8)
    pcnt = ((cnt + ((1 << rows_log2) - 1)) >> rows_log2) << rows_log2
    pstart, ptotal = excl_cumsum(pcnt)
    pend = pstart + pcnt

    upper = jnp.where(lax.broadcasted_iota(jnp.int32, (t, t), 0) < lax.broadcasted_iota(jnp.int32, (t, t), 1),
                      1.0, 0.0).astype(BF16)
    lane_b = lambda col: jnp.broadcast_to(col, (ne, LANES))

    def place(i, base):
        oh0, oh1 = onehots(i)
        both = oh0 + oh1
        before = jnp.dot(both.astype(BF16), upper, preferred_element_type=F32).astype(jnp.int32)
        cnt_i = jnp.sum(both, axis=1, keepdims=True).astype(jnp.int32)
        loff, _ = excl_cumsum(cnt_i)
        slot = loff + before
        l0 = jnp.sum(jnp.where(oh0 > 0.5, slot, 0), axis=0, keepdims=True)
        l1 = jnp.sum(jnp.where(oh1 > 0.5, slot, 0), axis=0, keepdims=True)
        lpos_ref[i] = jnp.concatenate([l0, l1], axis=0)
        lpad = jnp.concatenate([l0.astype(F32), l1.astype(F32), jnp.zeros((LANES - 2, t), F32)], axis=0)
        lcol_ref[i] = lpad.T
        meta_ref[i] = jnp.concatenate(
            [lane_b(cnt_i), lane_b(loff), lane_b(pstart + base),
             jnp.zeros((META_ROWS - 3 * ne, LANES), jnp.int32)], axis=0)
        return base + cnt_i

    lax.fori_loop(0, ntile, place, jnp.zeros((ne, 1), jnp.int32), unroll=4)

    pad_ref[...] = jnp.concatenate([lane_b(pstart + cnt), lane_b(pcnt - cnt),
                                    jnp.broadcast_to(ptotal, (ne, LANES))], axis=0)
    blk_row = lax.broadcasted_iota(jnp.int32, (ne, nblk_pad), 1) << rows_log2
    blke = jnp.sum(jnp.where(pend <= blk_row, 1, 0), axis=0, keepdims=True)
    blke_ref[...] = jnp.minimum(blke, ne - 1)
    nused_ref[...] = jnp.broadcast_to(ptotal >> rows_log2, (1, LANES))


def _rank(e3, ne, nblk_pad):
    ntile, _, t = e3.shape
    kern = functools.partial(_rank_kernel, ne=ne, rows_log2=int(math.log2(MOE_ROWS)), nblk_pad=nblk_pad)
    return pl.pallas_call(
        kern,
        out_shape=[
            jax.ShapeDtypeStruct((ntile, 2, t), jnp.int32),
            jax.ShapeDtypeStruct((ntile, t, LANES), F32),
            jax.ShapeDtypeStruct((ntile, META_ROWS, LANES), jnp.int32),
            jax.ShapeDtypeStruct((3 * ne, LANES), jnp.int32),
            jax.ShapeDtypeStruct((1, nblk_pad), jnp.int32),
            jax.ShapeDtypeStruct((1, LANES), jnp.int32),
        ],
        compiler_params=pltpu.CompilerParams(vmem_limit_bytes=VMEM_LIMIT),
        name="moe_rank",
    )(e3)


SUBROWS = 8


def _rows_to_interleaved(ref, x):
    n = x.shape[0]
    for s in range(SUBROWS):
        ref[pl.ds(s, n, stride=SUBROWS), :] = x[:, s * LANES:(s + 1) * LANES]


def _interleaved_to_rows(ref, n):
    return jnp.concatenate([ref[pl.ds(s, n, stride=SUBROWS), :] for s in range(SUBROWS)], axis=1)


def _segment_copies(count, src_ref, src_row, dst_ref, dst_row, sem, max_piece, wait=False, advance_src=True):
    p = max_piece
    while p >= 1:
        hit = (count & p) != 0

        @pl.when(hit)
        def _(p=p, src_row=src_row, dst_row=dst_row):
            cp = pltpu.make_async_copy(
                src_ref.at[pl.ds(pl.multiple_of(src_row * SUBROWS, SUBROWS), p * SUBROWS), :],
                dst_ref.at[pl.ds(pl.multiple_of(dst_row * SUBROWS, SUBROWS), p * SUBROWS), :], sem)
            if wait:
                cp.wait()
            else:
                cp.start()

        step = count & p
        if advance_src:
            src_row = src_row + step
        dst_row = dst_row + step
        p //= 2


def _dispatch_kernel(meta_ref, pad_ref, lpos_ref, f_ref, xs_ref, srt, zbuf, sems, zsem, *, td, ne):
    i = pl.program_id(0)

    @pl.when(i == 0)
    def _():
        zbuf[...] = jnp.zeros_like(zbuf)
        zrows = zbuf.shape[0] // SUBROWS
        total = xs_ref.shape[0] // SUBROWS
        for wait in (False, True):
            for e in range(ne):
                _segment_copies(pad_ref[ne + e], zbuf, 0, xs_ref, pad_ref[e], zsem,
                                zrows, wait=wait, advance_src=False)
            for r in range(0, ne * MOE_ROWS, zrows):
                row = pad_ref[2 * ne] + r

                @pl.when(row < total)
                def _(row=row):
                    cp = pltpu.make_async_copy(
                        zbuf, xs_ref.at[pl.ds(pl.multiple_of(row * SUBROWS, SUBROWS), zrows * SUBROWS), :], zsem)
                    if wait:
                        cp.wait()
                    else:
                        cp.start()

    slot = i % 2
    lp = lpos_ref[0]
    r = lax.broadcasted_iota(jnp.int32, (2 * td, td), 0)
    perm = (jnp.where(r == lp[0:1, :], 1.0, 0.0) + jnp.where(r == lp[1:2, :], 1.0, 0.0)).astype(BF16)
    _rows_to_interleaved(srt.at[slot], jnp.dot(perm, f_ref[...], preferred_element_type=F32))
    for e in range(ne):
        _segment_copies(meta_ref[0, 0, e], srt.at[slot], meta_ref[0, 0, ne + e], xs_ref,
                        meta_ref[0, 0, 2 * ne + e], sems.at[slot], td)

    def drain(s):
        for h in range(2):
            pltpu.make_async_copy(srt.at[s, pl.ds(h * td * SUBROWS, td * SUBROWS), :],
                                  xs_ref.at[pl.ds(0, td * SUBROWS), :], sems.at[s]).wait()

    @pl.when(i > 0)
    def _():
        drain(1 - slot)

    @pl.when(i + 1 == pl.num_programs(0))
    def _():
        drain(slot)


def _dispatch(meta, pad, lpos, f, total):
    N, D = f.shape
    ntile, _, td = lpos.shape
    ne = N_EXPERTS
    assert D == SUBROWS * LANES
    kern = functools.partial(_dispatch_kernel, td=td, ne=ne)
    return pl.pallas_call(
        kern,
        grid=(ntile,),
        in_specs=[
            pl.BlockSpec((1, 1, META_ROWS), lambda i: (i, 0, 0), memory_space=pltpu.SMEM),
            pl.BlockSpec(memory_space=pltpu.SMEM),
            pl.BlockSpec((1, 2, td), lambda i: (i, 0, 0)),
            pl.BlockSpec((td, D), lambda i: (i, 0)),
        ],
        out_specs=pl.BlockSpec(memory_space=pl.ANY),
        out_shape=jax.ShapeDtypeStruct((total * SUBROWS, LANES), F32),
        scratch_shapes=[pltpu.VMEM((2, 2 * td * SUBROWS, LANES), F32),
                        pltpu.VMEM((MOE_ROWS // 2 * SUBROWS, LANES), F32),
                        pltpu.SemaphoreType.DMA((2,)), pltpu.SemaphoreType.DMA(())],
        compiler_params=_cparams("arbitrary"),
        name="moe_dispatch",
    )(meta, pad, lpos, f)


def _expert_kernel(blke_ref, nused_ref, xs_ref, w1_ref, w3_ref, w2_ref, y_ref, *, rb):
    i = pl.program_id(0)

    @pl.when(i < nused_ref[0])
    def _():
        x = _interleaved_to_rows(xs_ref, rb).astype(BF16)
        a = jnp.dot(x, w1_ref[0], preferred_element_type=F32)
        b = jnp.dot(x, w3_ref[0], preferred_element_type=F32)
        act = (jax.nn.silu(a) * b).astype(BF16)
        _rows_to_interleaved(y_ref, jnp.dot(act, w2_ref[0], preferred_element_type=F32))

    @pl.when(i >= nused_ref[0])
    def _():
        y_ref[...] = jnp.zeros_like(y_ref)


def _experts(blke, nused, xs, w1, w3, w2):
    E, D, Fe = w1.shape
    total = xs.shape[0] // SUBROWS
    rb = MOE_ROWS
    nblk = total // rb
    wspec = lambda shape: pl.BlockSpec(shape, lambda i, be, nu: (be[i], 0, 0), pipeline_mode=pl.Buffered(1))
    return pl.pallas_call(
        functools.partial(_expert_kernel, rb=rb),
        grid_spec=pltpu.PrefetchScalarGridSpec(
            num_scalar_prefetch=2,
            grid=(nblk,),
            in_specs=[
                pl.BlockSpec((rb * SUBROWS, LANES), lambda i, be, nu: (jnp.minimum(i, nu[0] - 1), 0)),
                wspec((1, D, Fe)), wspec((1, D, Fe)), wspec((1, Fe, D)),
            ],
            out_specs=pl.BlockSpec((rb * SUBROWS, LANES), lambda i, be, nu: (i, 0)),
        ),
        out_shape=jax.ShapeDtypeStruct((total * SUBROWS, LANES), F32),
        compiler_params=_cparams("arbitrary"),
        name="moe_experts",
    )(blke, nused, xs, w1, w3, w2)


def _combine_kernel(meta_ref, nxt_ref, yb_ref, lcol_ref, wcol_ref, h_ref, g3_ref, o_ref, buf, sems, *, td, ne):
    i = pl.program_id(0)
    slot = i % 2

    def fetch(m_ref, s):
        for e in range(ne):
            _segment_copies(m_ref[0, 0, e], yb_ref, m_ref[0, 0, 2 * ne + e], buf.at[s], m_ref[0, 0, ne + e],
                            sems.at[s], td)

    @pl.when(i == 0)
    def _():
        fetch(meta_ref, 0)

    @pl.when(i + 1 < pl.num_programs(0))
    def _():
        fetch(nxt_ref, 1 - slot)

    for h in range(2):
        pltpu.make_async_copy(yb_ref.at[pl.ds(0, td * SUBROWS), :],
                              buf.at[slot, pl.ds(h * td * SUBROWS, td * SUBROWS), :], sems.at[slot]).wait()
    yb = _interleaved_to_rows(buf.at[slot], 2 * td).astype(BF16)
    lc = lcol_ref[0]
    w = wcol_ref[...]
    c = lax.broadcasted_iota(jnp.int32, (td, 2 * td), 1).astype(F32)
    pick0 = jnp.where(c == lc[:, 0:1], 1.0, 0.0).astype(BF16)
    pick1 = jnp.where(c == lc[:, 1:2], 1.0, 0.0).astype(BF16)
    f = (w[:, 0:1] * jnp.dot(pick0, yb, preferred_element_type=F32)
         + w[:, 1:2] * jnp.dot(pick1, yb, preferred_element_type=F32))
    o_ref[...] = h_ref[...] + _rms(f, g3_ref[...])


def _combine(meta, yb, lcol, wcol, h3, g3):
    N, D = h3.shape
    ntile, td, _ = lcol.shape
    ne = N_EXPERTS
    kern = functools.partial(_combine_kernel, td=td, ne=ne)
    return pl.pallas_call(
        kern,
        grid=(ntile,),
        in_specs=[
            pl.BlockSpec((1, 1, META_ROWS), lambda i: (i, 0, 0), memory_space=pltpu.SMEM),
            pl.BlockSpec((1, 1, META_ROWS), lambda i: (jnp.minimum(i + 1, ntile - 1), 0, 0),
                         memory_space=pltpu.SMEM),
            pl.BlockSpec(memory_space=pl.ANY),
            pl.BlockSpec((1, td, LANES), lambda i: (i, 0, 0)),
            pl.BlockSpec((td, LANES), lambda i: (i, 0)),
            pl.BlockSpec((td, D), lambda i: (i, 0)),
            _const_spec((1, D)),
        ],
        out_specs=pl.BlockSpec((td, D), lambda i: (i, 0)),
        out_shape=jax.ShapeDtypeStruct((N, D), F32),
        scratch_shapes=[pltpu.VMEM((2, 2 * td * SUBROWS, LANES), F32), pltpu.SemaphoreType.DMA((2,))],
        compiler_params=_cparams("arbitrary"),
        name="moe_combine",
    )(meta, meta, yb, lcol, wcol, h3, g3)


def _rope_lane_tables():
    half = ROT_DIM // 2
    inv_freq = 1.0 / (ROPE_THETA ** (np.arange(half, dtype=np.float32) / half))
    invf = np.zeros((1, LANES), np.float32)
    invf[0, :half] = inv_freq
    invf[0, half:ROT_DIM] = inv_freq
    sneg = np.zeros((1, LANES), np.float32)
    sneg[0, :half] = -1.0
    spos = np.zeros((1, LANES), np.float32)
    spos[0, half:ROT_DIM] = 1.0
    return jnp.asarray(invf), jnp.asarray(sneg), jnp.asarray(spos)


def kernel(x, positions, norms, s5_lam_re, s5_lam_im, s5_log_dt, s5_b_re, s5_b_im, s5_c_re, s5_c_im, s5_d, s5_w_glu, kv_norm, w_kv, w_q, w_o, ffn_w1, ffn_w3, ffn_w2, moe_router, moe_bias, moe_w1, moe_w3, moe_w2):
    B, S, D = x.shape
    N = B * S
    HD = N_HEADS * HEAD_DIM
    E = moe_router.shape[-1]
    row = lambda v: v.astype(F32).reshape(1, -1)

    a2_r, a2_i, b2, c2, d2 = _s5_params(s5_lam_re[0], s5_lam_im[0], s5_log_dt[0], s5_b_re[0], s5_b_im[0],
                                        s5_c_re[0], s5_c_im[0])
    h1 = _s5_layer(x, row(norms[0, 0]), row(norms[0, 1]), row(s5_d[0]), a2_r, a2_i, b2, c2, d2,
                   s5_w_glu[0].astype(BF16))
    h2 = _dense_ffn(h1.reshape(N, D), row(norms[0, 2]), row(norms[0, 3]),
                    ffn_w1[0].astype(BF16), ffn_w3[0].astype(BF16), ffn_w2[0].astype(BF16))

    invf, sneg, spos = _rope_lane_tables()
    posb = jnp.broadcast_to(positions.astype(F32)[:, :, None], (B, S, LANES))
    q, k5, vt5, kmean = _qkv(h2.reshape(B, S, D), posb, invf, sneg, spos, row(norms[1, 0]), row(kv_norm),
                             w_q[0].astype(BF16), w_kv[:, :HD].astype(BF16), w_kv[:, HD:].T.astype(BF16))
    h3, f_in, e_tiles, wcol = _moba_layer(q, k5, vt5, kmean, h2.reshape(B, S, D), row(norms[1, 1]), w_o[0].astype(BF16),
                                          row(norms[1, 2]), moe_router[0].T.astype(F32),
                                          moe_bias[0].astype(F32).reshape(E, 1))
    h3 = h3.reshape(N, D)

    assert MOBA_BLOCK == DISPATCH_TILE
    total = 2 * N + E * MOE_ROWS
    nblk = total // MOE_ROWS
    nblk_pad = -(-nblk // LANES) * LANES
    lpos, lcol, meta, pad, blke, nused = _rank(e_tiles, E, nblk_pad)
    meta = meta[:, :, 0].reshape(-1, 1, META_ROWS)
    xs = _dispatch(meta, pad[:, 0], lpos, f_in.reshape(N, D), total)
    yb = _experts(blke[0, :nblk], nused[0, :1], xs,
                  moe_w1[0].astype(BF16), moe_w3[0].astype(BF16), moe_w2[0].astype(BF16))
    out = _combine(meta, yb, lcol, wcol.reshape(N, LANES), h3, row(norms[1, 3]))
    return out.reshape(B, S, D)
```

```python
import functools
import math

import numpy as np
import jax
import jax.numpy as jnp
from jax import lax
from jax.experimental import pallas as pl
from jax.experimental.pallas import tpu as pltpu

F32 = jnp.float32
BF16 = jnp.bfloat16

NORM_EPS = 1e-6
NEG_INF = -1e30

SSM_GROUP = 16
N_HEADS = 8
HEAD_DIM = 128
ROT_DIM = HEAD_DIM // 4
ROPE_THETA = 500000.0
MOBA_BLOCK = 256
MOBA_TOPK = 3
N_EXPERTS = 8

Q_SCALE_LOG2 = HEAD_DIM ** -0.5 * math.log2(math.e)

KMEAN_PIECES = 3

LANES = 128
GROUPS_PER_TILE = LANES // SSM_GROUP

VMEM_LIMIT = 56 * 1024 * 1024

S5_CHUNK = 64
S5_ROW_PAD = 8
ROW_TILE = 512
QKV_TILE = 512
MOE_ROWS = 512
DISPATCH_TILE = 256


def _cparams(*sem):
    return pltpu.CompilerParams(dimension_semantics=sem, vmem_limit_bytes=VMEM_LIMIT)


def _rms(x, g):
    return x * lax.rsqrt(jnp.mean(x * x, axis=-1, keepdims=True) + NORM_EPS) * g


def _const_spec(shape):
    n = len(shape)
    return pl.BlockSpec(shape, lambda *_: (0,) * n, pipeline_mode=pl.Buffered(1))


def _zoh(lr, li, dt):
    zr = lr * dt
    zi = li * dt
    mag = jnp.exp(zr)
    return mag * jnp.cos(zr * 0 + zi), mag * jnp.sin(zi)


def _cmul(xr, xi, yr, yi):
    return xr * yr - xi * yi, xr * yi + xi * yr


def _s5_prep_kernel(lrb_ref, lib_ref, lrc_ref, lic_ref, ldt_ref, br_ref, bi_ref, cr_ref, ci_ref,
                    a2r_ref, a2i_ref, bbr_ref, bbi_ref, abr_ref, abi_ref, car_ref, cai_ref, c2r_ref, c2i_ref):
    dt = jnp.exp(ldt_ref[...])
    lr = lrb_ref[...]
    li = lib_ref[...]
    a_r, a_i = _zoh(lr, li, dt)
    den = lr * lr + li * li
    coef_r = ((a_r - 1.0) * lr + a_i * li) / den
    coef_i = (a_i * lr - (a_r - 1.0) * li) / den
    bb_r, bb_i = _cmul(coef_r, coef_i, br_ref[...], bi_ref[...])
    bbr_ref[...] = bb_r
    bbi_ref[...] = bb_i
    abr_ref[...], abi_ref[...] = _cmul(a_r, a_i, bb_r, bb_i)
    a2r_ref[...], a2i_ref[...] = _cmul(a_r, a_i, a_r, a_i)
    ac_r, ac_i = _zoh(lrc_ref[...], lic_ref[...], dt)
    ca_r, ca_i = _cmul(cr_ref[...], ci_ref[...], ac_r, ac_i)
    car_ref[...] = ca_r
    cai_ref[...] = ca_i
    c2r_ref[...], c2i_ref[...] = _cmul(ca_r, ca_i, ac_r, ac_i)


def _s5_fold_kernel(b_ref, ab_ref, c_ref, d0_ref, d1_ref):
    for k in range(b_ref.shape[0]):
        d0_ref[k] = jnp.dot(b_ref[k], c_ref[k], precision=lax.Precision.HIGHEST, preferred_element_type=F32)
        d1_ref[k] = jnp.dot(ab_ref[k], c_ref[k], precision=lax.Precision.HIGHEST, preferred_element_type=F32)


def _s5_params(lam_re, lam_im, log_dt, b_re, b_im, c_re, c_im):
    G, P = lam_re.shape
    Q = b_re.shape[-1]
    gt = GROUPS_PER_TILE
    nt = G // gt
    f = lambda a: a.astype(F32)
    rep = lambda a: jnp.repeat(f(a), Q, axis=1)
    til = lambda a: jnp.tile(f(a), (1, Q))
    out = jax.ShapeDtypeStruct((G, P * Q), F32)
    a2_r, a2_i, bb_r, bb_i, ab_r, ab_i, ca_r, ca_i, c2_r, c2_i = pl.pallas_call(
        _s5_prep_kernel, out_shape=(out,) * 10, name="s5_prep",
    )(rep(lam_re), rep(lam_im), til(lam_re), til(lam_im), f(log_dt).reshape(G, 1),
      f(b_re).reshape(G, P * Q), f(b_im).reshape(G, P * Q), f(c_re).reshape(G, Q * P), f(c_im).reshape(G, Q * P))

    eye = jnp.eye(gt, dtype=F32)
    def bmat(bb):
        return jnp.einsum('kgpi,gh->kgihp', bb.reshape(nt, gt, P, Q), eye).reshape(nt, gt * Q, gt * P)
    def cmat(c):
        return jnp.einsum('kgip,gh->kgphi', c.reshape(nt, gt, Q, P), eye).reshape(nt, gt * P, gt * Q)
    bmat2 = lambda r, i: jnp.concatenate([bmat(r), bmat(i)], axis=-1)
    cmat2 = lambda r, i: jnp.concatenate([cmat(r), -cmat(i)], axis=1)
    b_bd, ab_bd = bmat2(bb_r, bb_i), bmat2(ab_r, ab_i)
    c_bd = cmat2(f(c_re).reshape(G, Q * P), f(c_im).reshape(G, Q * P))
    dshape = jax.ShapeDtypeStruct((nt, LANES, LANES), F32)
    d0, d1 = pl.pallas_call(_s5_fold_kernel, out_shape=(dshape, dshape), name="s5_fold",
                            compiler_params=pltpu.CompilerParams(vmem_limit_bytes=VMEM_LIMIT))(b_bd, ab_bd, c_bd)
    b2 = jnp.concatenate([ab_bd, b_bd], axis=1).astype(BF16)
    c2 = jnp.concatenate([cmat2(ca_r, ca_i), cmat2(c2_r, c2_i)], axis=2).astype(BF16)
    d2 = jnp.concatenate([jnp.concatenate([d0, d1], axis=2),
                          jnp.concatenate([jnp.zeros_like(d0), d0], axis=2)], axis=1).astype(BF16)
    sel = lambda a: a.reshape(G, P, Q)[:, :, 0].reshape(nt, 1, gt * P)
    return sel(a2_r), sel(a2_i), b2, c2, d2


def _s5_layer_kernel(x_ref, g0_ref, g1_ref, d_ref, ar_ref, ai_ref, b2_ref, c2_ref, d2_ref, wglu_ref,
                     o_ref, bm, tm, bu, y_tm, state, *, nb, lc, d, nt, hw):
    c = pl.program_id(0)
    pitch = lc + S5_ROW_PAD
    npair = lc // 2

    @pl.when(c == 0)
    def _():
        state[...] = jnp.zeros_like(state)

    for b in range(nb):
        u = _rms(x_ref[b], g0_ref[...])
        for k in range(nt):
            bm[k, b * pitch:b * pitch + lc, :] = u[:, k * LANES:(k + 1) * LANES]
    for t in range(lc):
        for k in range(nt):
            tm[k, t * nb:(t + 1) * nb, :] = bm[k, pl.ds(t, nb, stride=pitch), :]

    for k in range(nt):
        uk = tm[k]
        u4 = uk.reshape(npair, 2, nb, LANES)
        u2 = jnp.concatenate([u4[:, 0].reshape(npair * nb, LANES), u4[:, 1].reshape(npair * nb, LANES)],
                             axis=1).astype(BF16)
        bu[...] = jnp.dot(u2, b2_ref[k], preferred_element_type=F32)
        a_r = jnp.broadcast_to(ar_ref[k], (nb, hw))
        a_i = jnp.broadcast_to(ai_ref[k], (nb, hw))

        def step(t, carry):
            s_r, s_i = carry
            r0 = pl.multiple_of(t * nb, nb)
            n_r = a_r * s_r - a_i * s_i + bu[pl.ds(r0, nb), 0:hw]
            n_i = a_r * s_i + a_i * s_r + bu[pl.ds(r0, nb), hw:2 * hw]
            bu[pl.ds(r0, nb), 0:hw] = s_r
            bu[pl.ds(r0, nb), hw:2 * hw] = s_i
            return n_r, n_i

        s_r, s_i = lax.fori_loop(0, npair, step, (state[k, :, 0:hw], state[k, :, hw:2 * hw]), unroll=True)
        state[k, :, 0:hw] = s_r
        state[k, :, hw:2 * hw] = s_i
        y2 = (jnp.dot(bu[...].astype(BF16), c2_ref[k], preferred_element_type=F32)
              + jnp.dot(u2, d2_ref[k], preferred_element_type=F32))
        yk = jnp.concatenate([y2[:, :LANES].reshape(npair, 1, nb, LANES),
                              y2[:, LANES:].reshape(npair, 1, nb, LANES)], axis=1).reshape(lc * nb, LANES)
        y_tm[:, k * LANES:(k + 1) * LANES] = yk + d_ref[:, k * LANES:(k + 1) * LANES] * uk

    gl = jax.nn.gelu(y_tm[...]).astype(BF16)
    z = jnp.dot(gl, wglu_ref[...], preferred_element_type=F32)
    mix = z[:, :d] * jax.nn.sigmoid(z[:, d:])
    mixn = _rms(mix, g1_ref[...])
    for k in range(nt):
        tm[k] = mixn[:, k * LANES:(k + 1) * LANES]
    for t in range(lc):
        for k in range(nt):
            bm[k, pl.ds(t, nb, stride=pitch), :] = tm[k, t * nb:(t + 1) * nb, :]
    for b in range(nb):
        for k in range(nt):
            sl = slice(k * LANES, (k + 1) * LANES)
            o_ref[b, :, sl] = x_ref[b, :, sl] + bm[k, b * pitch:b * pitch + lc, :]


def _s5_layer(x, g0, g1, d_skip, a2_r_t, a2_i_t, b2, c2, d2, w_glu):
    B, S, D = x.shape
    lc = S5_CHUNK
    nt = D // LANES
    hw = a2_r_t.shape[-1]
    rows = B * lc
    kern = functools.partial(_s5_layer_kernel, nb=B, lc=lc, d=D, nt=nt, hw=hw)
    return pl.pallas_call(
        kern,
        grid=(S // lc,),
        in_specs=[
            pl.BlockSpec((B, lc, D), lambda c: (0, c, 0)),
            _const_spec((1, D)), _const_spec((1, D)), _const_spec((1, D)),
            _const_spec((nt, 1, hw)), _const_spec((nt, 1, hw)),
            _const_spec((nt, 2 * LANES, 2 * hw)), _const_spec((nt, 2 * hw, 2 * LANES)),
            _const_spec((nt, 2 * LANES, 2 * LANES)),
            _const_spec((D, 2 * D)),
        ],
        out_specs=pl.BlockSpec((B, lc, D), lambda c: (0, c, 0)),
        out_shape=jax.ShapeDtypeStruct((B, S, D), F32),
        scratch_shapes=[
            pltpu.VMEM((nt, B * (lc + S5_ROW_PAD), LANES), F32), pltpu.VMEM((nt, rows, LANES), F32),
            pltpu.VMEM((rows // 2, 2 * hw), F32), pltpu.VMEM((rows, D), F32),
            pltpu.VMEM((nt, B, 2 * hw), F32),
        ],
        compiler_params=_cparams("arbitrary"),
        name="s5_layer",
    )(x, g0, g1, d_skip, a2_r_t, a2_i_t, b2, c2, d2, w_glu)


def _ffn_kernel(h_ref, g2_ref, g3_ref, w1_ref, w3_ref, w2_ref, o_ref):
    h = h_ref[...]
    f_in = _rms(h, g2_ref[...]).astype(BF16)
    a = jnp.dot(f_in, w1_ref[...], preferred_element_type=F32)
    b = jnp.dot(f_in, w3_ref[...], preferred_element_type=F32)
    act = (jax.nn.silu(a) * b).astype(BF16)
    f = jnp.dot(act, w2_ref[...], preferred_element_type=F32)
    o_ref[...] = h + _rms(f, g3_ref[...])


def _dense_ffn(h, g2, g3, w1, w3, w2):
    N, D = h.shape
    Fd = w1.shape[1]
    tm = ROW_TILE
    return pl.pallas_call(
        _ffn_kernel,
        grid=(N // tm,),
        in_specs=[
            pl.BlockSpec((tm, D), lambda i: (i, 0)),
            _const_spec((1, D)), _const_spec((1, D)),
            _const_spec((D, Fd)), _const_spec((D, Fd)), _const_spec((Fd, D)),
        ],
        out_specs=pl.BlockSpec((tm, D), lambda i: (i, 0)),
        out_shape=jax.ShapeDtypeStruct((N, D), F32),
        compiler_params=_cparams("parallel"),
        name="dense_ffn",
    )(h, g2, g3, w1, w3, w2)


def _rotary(xh, cs, s_neg, s_pos):
    half = ROT_DIM // 2
    return xh * cs + pltpu.roll(xh, LANES - half, 1) * s_neg + pltpu.roll(xh, half, 1) * s_pos


def _qkv_kernel(h_ref, pos_ref, invf_ref, sneg_ref, spos_ref, gq_ref, gkv_ref,
                wq_ref, wk_ref, wvt_ref, q_ref, k_ref, vt_ref, km_ref, *, nblk_tile, nblk_seq):
    ti = pl.program_id(1)
    h = h_ref[0]
    r = lax.rsqrt(jnp.mean(h * h, axis=-1, keepdims=True) + NORM_EPS)
    hq = (h * r * gq_ref[...]).astype(BF16)
    hk = (h * r * gkv_ref[...]).astype(BF16)
    ang = pos_ref[0] * invf_ref[...]
    cs = jnp.cos(ang)
    sn = jnp.sin(ang)
    s_neg = sn * sneg_ref[...]
    s_pos = sn * spos_ref[...]
    q = jnp.dot(hq, wq_ref[...], preferred_element_type=F32)
    k = jnp.dot(hk, wk_ref[...], preferred_element_type=F32)
    vt = lax.dot_general(wvt_ref[...], hk, (((1,), (1,)), ((), ())), preferred_element_type=F32)
    for hh in range(N_HEADS):
        sl = slice(hh * HEAD_DIM, (hh + 1) * HEAD_DIM)
        q_ref[0, hh] = (_rotary(q[:, sl], cs, s_neg, s_pos) * Q_SCALE_LOG2).astype(BF16)
        kr = _rotary(k[:, sl], cs, s_neg, s_pos)
        for j in range(nblk_tile):
            rs = slice(j * MOBA_BLOCK, (j + 1) * MOBA_BLOCK)
            k_ref[0, hh, j] = kr[rs].astype(BF16)
            vt_ref[0, hh, j] = vt[sl, rs].astype(BF16)
            km = jnp.mean(kr[rs], axis=0, keepdims=True)
            for piece in range(KMEAN_PIECES):
                part = km.astype(BF16).astype(F32)
                km_ref[0, hh, pl.ds(piece * nblk_seq + ti * nblk_tile + j, 1), :] = part
                km = km - part


def _qkv(h2, posb, invf, sneg, spos, gq, gkv, wq, wk, wvt):
    B, S, D = h2.shape
    T = QKV_TILE
    H, Dh, blk = N_HEADS, HEAD_DIM, MOBA_BLOCK
    nb = S // blk
    nbt = T // blk
    kern = functools.partial(_qkv_kernel, nblk_tile=nbt, nblk_seq=nb)
    return pl.pallas_call(
        kern,
        grid=(B, S // T),
        in_specs=[
            pl.BlockSpec((1, T, D), lambda b, t: (b, t, 0)),
            pl.BlockSpec((1, T, LANES), lambda b, t: (b, t, 0)),
            _const_spec((1, LANES)), _const_spec((1, LANES)), _const_spec((1, LANES)),
            _const_spec((1, D)), _const_spec((1, D)),
            _const_spec((D, D)), _const_spec((D, D)), _const_spec((D, D)),
        ],
        out_specs=[
            pl.BlockSpec((1, H, T, Dh), lambda b, t: (b, 0, t, 0)),
            pl.BlockSpec((1, H, nbt, blk, Dh), lambda b, t: (b, 0, t, 0, 0)),
            pl.BlockSpec((1, H, nbt, Dh, blk), lambda b, t: (b, 0, t, 0, 0)),
            pl.BlockSpec((1, H, KMEAN_PIECES * nb, Dh), lambda b, t: (b, 0, 0, 0)),
        ],
        out_shape=[
            jax.ShapeDtypeStruct((B, H, S, Dh), BF16),
            jax.ShapeDtypeStruct((B, H, nb, blk, Dh), BF16),
            jax.ShapeDtypeStruct((B, H, nb, Dh, blk), BF16),
            jax.ShapeDtypeStruct((B, H, KMEAN_PIECES * nb, Dh), F32),
        ],
        compiler_params=_cparams("parallel", "arbitrary"),
        name="qkv_proj",
    )(h2, posb, invf, sneg, spos, gq, gkv, wq, wk, wvt)


def _moba_kernel(q_ref, k_ref, vt_ref, km_ref, h_ref, g1_ref, wo_ref, g2_ref, wrt_ref, br_ref,
                 o_ref, f_ref, e_ref, wcol_ref, sel_scr, st_scr, acc_scr, o_scr, *, nb, k_sel):
    j = pl.program_id(1)
    blk = MOBA_BLOCK
    nt_dims = (((1,), (1,)), ((), ()))
    blk_id = lax.broadcasted_iota(jnp.int32, (nb, blk), 0)
    past = blk_id < j
    causal = (lax.broadcasted_iota(jnp.int32, (blk, blk), 0)
              <= lax.broadcasted_iota(jnp.int32, (blk, blk), 1))

    for hh in range(N_HEADS):
        parts = lax.dot_general(km_ref[0, hh].astype(BF16), q_ref[0, hh], nt_dims,
                                preferred_element_type=F32)
        gate = parts[0:nb]
        for piece in range(1, KMEAN_PIECES):
            gate = gate + parts[piece * nb:(piece + 1) * nb]
        gate = jnp.where(past, gate, NEG_INF)
        rank = jnp.zeros((nb, blk), F32)
        for m in range(nb):
            gm = gate[m:m + 1, :]
            beats = jnp.where(gm > gate, 1.0, jnp.where(gm == gate, jnp.where(m < blk_id, 1.0, 0.0), 0.0))
            rank = rank + beats
        sel_scr[hh] = jnp.where(past, jnp.where(rank < k_sel, 1.0, 0.0), 0.0)

    def qk_pass(n, ms, own):
        out = []
        for hh in range(N_HEADS):
            st = lax.dot_general(k_ref[0, hh, n], q_ref[0, hh], nt_dims, preferred_element_type=F32)
            allowed = causal if own else sel_scr[hh, pl.ds(n, 1), :] > 0.5
            st = jnp.where(allowed, st, NEG_INF)
            st_scr[hh, n] = st
            out.append(jnp.maximum(ms[hh], jnp.max(st, axis=0, keepdims=True)))
        return tuple(out)

    m_init = tuple(jnp.full((1, blk), NEG_INF, F32) for _ in range(N_HEADS))
    m_own = qk_pass(j, m_init, True)
    def past_blocks(step, carry):
        done = 0
        for group in (4, 2, 1):
            trips = (j - done) // group

            def body(i, c, group=group, done=done):
                for g in range(group):
                    c = step(done + i * group + g, c)
                return c

            carry = lax.fori_loop(0, trips, body, carry)
            done = done + trips * group
        return carry

    m_fin = past_blocks(lambda n, ms: qk_pass(n, ms, False), m_own)

    def pv_pass(n, ls, first):
        out = []
        for hh in range(N_HEADS):
            p = jnp.exp2(st_scr[hh, n] - m_fin[hh])
            out.append(ls[hh] + jnp.sum(p, axis=0, keepdims=True))
            pv = jnp.dot(vt_ref[0, hh, n], p.astype(BF16), preferred_element_type=F32)
            acc_scr[hh] = pv if first else acc_scr[hh] + pv
        return tuple(out)

    l_own = pv_pass(j, tuple(jnp.zeros((1, blk), F32) for _ in range(N_HEADS)), True)
    l_fin = past_blocks(lambda n, ls: pv_pass(n, ls, False), l_own)
    for hh in range(N_HEADS):
        o_t = acc_scr[hh] / l_fin[hh]
        o_scr[:, hh * HEAD_DIM:(hh + 1) * HEAD_DIM] = o_t.T.astype(BF16)

    mix = jnp.dot(o_scr[...], wo_ref[...], preferred_element_type=F32)
    h3 = h_ref[0] + _rms(mix, g1_ref[...])
    o_ref[0] = h3
    f = _rms(h3, g2_ref[...])
    f_ref[0] = f.astype(BF16)
    e_ref[0], wcol_ref[0] = _route(f, wrt_ref, br_ref)


def _moba_layer(q, k5, vt5, kmean, h2, g1, wo, g2, wrt, br):
    B, S, D = h2.shape
    H, Dh, blk = N_HEADS, HEAD_DIM, MOBA_BLOCK
    nb = S // blk
    E = wrt.shape[0]
    kern = functools.partial(_moba_kernel, nb=nb, k_sel=min(MOBA_TOPK, nb))
    return pl.pallas_call(
        kern,
        grid=(B, nb),
        in_specs=[
            pl.BlockSpec((1, H, blk, Dh), lambda b, j: (b, 0, j, 0)),
            pl.BlockSpec((1, H, nb, blk, Dh), lambda b, j: (b, 0, 0, 0, 0)),
            pl.BlockSpec((1, H, nb, Dh, blk), lambda b, j: (b, 0, 0, 0, 0)),
            pl.BlockSpec((1, H, KMEAN_PIECES * nb, Dh), lambda b, j: (b, 0, 0, 0)),
            pl.BlockSpec((1, blk, D), lambda b, j: (b, j, 0)),
            _const_spec((1, D)), _const_spec((D, D)),
            _const_spec((1, D)), _const_spec((E, D)), _const_spec((E, 1)),
        ],
        out_specs=[
            pl.BlockSpec((1, blk, D), lambda b, j: (b, j, 0)),
            pl.BlockSpec((1, blk, D), lambda b, j: (b, j, 0)),
            pl.BlockSpec((1, 2, blk), lambda b, j: (b * nb + j, 0, 0)),
            pl.BlockSpec((1, blk, LANES), lambda b, j: (b * nb + j, 0, 0)),
        ],
        out_shape=[
            jax.ShapeDtypeStruct((B, S, D), F32),
            jax.ShapeDtypeStruct((B, S, D), BF16),
            jax.ShapeDtypeStruct((B * nb, 2, blk), jnp.int32),
            jax.ShapeDtypeStruct((B * nb, blk, LANES), F32),
        ],
        scratch_shapes=[
            pltpu.VMEM((H, nb, blk), F32), pltpu.VMEM((H, nb, blk, blk), F32),
            pltpu.VMEM((H, Dh, blk), F32), pltpu.VMEM((blk, D), BF16),
        ],
        compiler_params=_cparams("parallel", "arbitrary"),
        name="moba_attn",
    )(q, k5, vt5, kmean, h2, g1, wo, g2, wrt, br)


def _route(f, wrt_ref, br_ref):
    nt_dims = (((1,), (1,)), ((), ()))
    w = wrt_ref[...]
    ne = w.shape[0]
    w_hi = w.astype(BF16).astype(F32)
    f_hi = f.astype(BF16)
    f_lo = (f - f_hi.astype(F32)).astype(BF16)
    w2 = jnp.concatenate([w_hi, w - w_hi], axis=0).astype(BF16)
    a = lax.dot_general(w2, f_hi, nt_dims, preferred_element_type=F32)
    b = lax.dot_general(w2[:ne], f_lo, nt_dims, preferred_element_type=F32)
    logit = a[:ne] + a[ne:] + b + br_ref[...]
    t = logit.shape[1]
    eidx = lax.broadcasted_iota(jnp.int32, (ne, t), 0)
    m1 = jnp.max(logit, axis=0, keepdims=True)
    e1 = jnp.min(jnp.where(logit == m1, eidx, ne), axis=0, keepdims=True)
    rest = jnp.where(eidx == e1, -jnp.inf, logit)
    m2 = jnp.max(rest, axis=0, keepdims=True)
    e2 = jnp.min(jnp.where(rest == m2, eidx, ne), axis=0, keepdims=True)
    ex = jnp.exp(m2 - m1)
    den = 1.0 + ex
    wpad = jnp.concatenate([1.0 / den, ex / den, jnp.zeros((LANES - 2, t), F32)], axis=0)
    return jnp.concatenate([e1, e2], axis=0), wpad.T


META_ROWS = 32


def _rank_kernel(e_ref, lpos_ref, lcol_ref, meta_ref, pad_ref, blke_ref, nused_ref, *, ne, rows_log2, nblk_pad):
    ntile, _, t = e_ref.shape
    eidx = lax.broadcasted_iota(jnp.int32, (ne, t), 0)

    def onehots(i):
        e = e_ref[i]
        oh0 = jnp.where(eidx == e[0:1, :], 1.0, 0.0)
        oh1 = jnp.where(eidx == e[1:2, :], 1.0, 0.0)
        return oh0, oh1

    def excl_cumsum(col):
        run = jnp.zeros((1, 1), jnp.int32)
        parts = []
        for e in range(ne):
            parts.append(run)
            run = run + col[e:e + 1, :]
        return jnp.concatenate(parts, axis=0), run

    def count(i, cnt):
        oh0, oh1 = onehots(i)
        return cnt + jnp.sum(oh0 + oh1, axis=1, keepdims=True).astype(jnp.int32)

    cnt = lax.fori_loop(0, ntile, count, jnp.zeros((ne, 1), jnp.int32), unroll=8)
    pcnt = ((cnt + ((1 << rows_log2) - 1)) >> rows_log2) << rows_log2
    pstart, ptotal = excl_cumsum(pcnt)
    pend = pstart + pcnt

    upper = jnp.where(lax.broadcasted_iota(jnp.int32, (t, t), 0) < lax.broadcasted_iota(jnp.int32, (t, t), 1),
                      1.0, 0.0).astype(BF16)
    lane_b = lambda col: jnp.broadcast_to(col, (ne, LANES))

    def place(i, base):
        oh0, oh1 = onehots(i)
        both = oh0 + oh1
        before = jnp.dot(both.astype(BF16), upper, preferred_element_type=F32).astype(jnp.int32)
        cnt_i = jnp.sum(both, axis=1, keepdims=True).astype(jnp.int32)
        loff, _ = excl_cumsum(cnt_i)
        slot = loff + before
        l0 = jnp.sum(jnp.where(oh0 > 0.5, slot, 0), axis=0, keepdims=True)
        l1 = jnp.sum(jnp.where(oh1 > 0.5, slot, 0), axis=0, keepdims=True)
        lpos_ref[i] = jnp.concatenate([l0, l1], axis=0)
        lpad = jnp.concatenate([l0.astype(F32), l1.astype(F32), jnp.zeros((LANES - 2, t), F32)], axis=0)
        lcol_ref[i] = lpad.T
        meta_ref[i] = jnp.concatenate(
            [lane_b(cnt_i), lane_b(loff), lane_b(pstart + base),
             jnp.zeros((META_ROWS - 3 * ne, LANES), jnp.int32)], axis=0)
        return base + cnt_i

    lax.fori_loop(0, ntile, place, jnp.zeros((ne, 1), jnp.int32), unroll=4)

    pad_ref[...] = jnp.concatenate([lane_b(pstart + cnt), lane_b(pcnt - cnt),
                                    jnp.broadcast_to(ptotal, (ne, LANES))], axis=0)
    blk_row = lax.broadcasted_iota(jnp.int32, (ne, nblk_pad), 1) << rows_log2
    blke = jnp.sum(jnp.where(pend <= blk_row, 1, 0), axis=0, keepdims=True)
    blke_ref[...] = jnp.minimum(blke, ne - 1)
    nused_ref[...] = jnp.broadcast_to(ptotal >> rows_log2, (1, LANES))


def _rank(e3, ne, nblk_pad):
    ntile, _, t = e3.shape
    kern = functools.partial(_rank_kernel, ne=ne, rows_log2=int(math.log2(MOE_ROWS)), nblk_pad=nblk_pad)
    return pl.pallas_call(
        kern,
        out_shape=[
            jax.ShapeDtypeStruct((ntile, 2, t), jnp.int32),
            jax.ShapeDtypeStruct((ntile, t, LANES), F32),
            jax.ShapeDtypeStruct((ntile, META_ROWS, LANES), jnp.int32),
            jax.ShapeDtypeStruct((3 * ne, LANES), jnp.int32),
            jax.ShapeDtypeStruct((1, nblk_pad), jnp.int32),
            jax.ShapeDtypeStruct((1, LANES), jnp.int32),
        ],
        compiler_params=pltpu.CompilerParams(vmem_limit_bytes=VMEM_LIMIT),
        name="moe_rank",
    )(e3)


SUBROWS = 8


def _rows_to_interleaved(ref, x):
    n = x.shape[0]
    for s in range(SUBROWS):
        ref[pl.ds(s, n, stride=SUBROWS), :] = x[:, s * LANES:(s + 1) * LANES]


def _interleaved_to_rows(ref, n):
    return jnp.concatenate([ref[pl.ds(s, n, stride=SUBROWS), :] for s in range(SUBROWS)], axis=1)


def _segment_copies(count, src_ref, src_row, dst_ref, dst_row, sem, max_piece, wait=False, advance_src=True):
    p = max_piece
    while p >= 1:
        hit = (count & p) != 0

        @pl.when(hit)
        def _(p=p, src_row=src_row, dst_row=dst_row):
            cp = pltpu.make_async_copy(
                src_ref.at[pl.ds(pl.multiple_of(src_row * SUBROWS, SUBROWS), p * SUBROWS), :],
                dst_ref.at[pl.ds(pl.multiple_of(dst_row * SUBROWS, SUBROWS), p * SUBROWS), :], sem)
            if wait:
                cp.wait()
            else:
                cp.start()

        step = count & p
        if advance_src:
            src_row = src_row + step
        dst_row = dst_row + step
        p //= 2


def _dispatch_kernel(meta_ref, pad_ref, lpos_ref, f_ref, xs_ref, srt, zbuf, sems, zsem, *, td, ne):
    i = pl.program_id(0)

    @pl.when(i == 0)
    def _():
        zbuf[...] = jnp.zeros_like(zbuf)
        zrows = zbuf.shape[0] // SUBROWS
        total = xs_ref.shape[0] // SUBROWS
        for wait in (False, True):
            for e in range(ne):
                _segment_copies(pad_ref[ne + e], zbuf, 0, xs_ref, pad_ref[e], zsem,
                                zrows, wait=wait, advance_src=False)
            for r in range(0, ne * MOE_ROWS, zrows):
                row = pad_ref[2 * ne] + r

                @pl.when(row < total)
                def _(row=row):
                    cp = pltpu.make_async_copy(
                        zbuf, xs_ref.at[pl.ds(pl.multiple_of(row * SUBROWS, SUBROWS), zrows * SUBROWS), :], zsem)
                    if wait:
                        cp.wait()
                    else:
                        cp.start()

    slot = i % 2
    lp = lpos_ref[0]
    r = lax.broadcasted_iota(jnp.int32, (2 * td, td), 0)
    perm = (jnp.where(r == lp[0:1, :], 1.0, 0.0) + jnp.where(r == lp[1:2, :], 1.0, 0.0)).astype(BF16)
    _rows_to_interleaved(srt.at[slot], jnp.dot(perm, f_ref[...], preferred_element_type=F32))
    for e in range(ne):
        _segment_copies(meta_ref[0, 0, e], srt.at[slot], meta_ref[0, 0, ne + e], xs_ref,
                        meta_ref[0, 0, 2 * ne + e], sems.at[slot], td)

    def drain(s):
        for h in range(2):
            pltpu.make_async_copy(srt.at[s, pl.ds(h * td * SUBROWS, td * SUBROWS), :],
                                  xs_ref.at[pl.ds(0, td * SUBROWS), :], sems.at[s]).wait()

    @pl.when(i > 0)
    def _():
        drain(1 - slot)

    @pl.when(i + 1 == pl.num_programs(0))
    def _():
        drain(slot)


def _dispatch(meta, pad, lpos, f, total):
    N, D = f.shape
    ntile, _, td = lpos.shape
    ne = N_EXPERTS
    assert D == SUBROWS * LANES
    kern = functools.partial(_dispatch_kernel, td=td, ne=ne)
    return pl.pallas_call(
        kern,
        grid=(ntile,),
        in_specs=[
            pl.BlockSpec((1, 1, META_ROWS), lambda i: (i, 0, 0), memory_space=pltpu.SMEM),
            pl.BlockSpec(memory_space=pltpu.SMEM),
            pl.BlockSpec((1, 2, td), lambda i: (i, 0, 0)),
            pl.BlockSpec((td, D), lambda i: (i, 0)),
        ],
        out_specs=pl.BlockSpec(memory_space=pl.ANY),
        out_shape=jax.ShapeDtypeStruct((total * SUBROWS, LANES), F32),
        scratch_shapes=[pltpu.VMEM((2, 2 * td * SUBROWS, LANES), F32),
                        pltpu.VMEM((MOE_ROWS // 2 * SUBROWS, LANES), F32),
                        pltpu.SemaphoreType.DMA((2,)), pltpu.SemaphoreType.DMA(())],
        compiler_params=_cparams("arbitrary"),
        name="moe_dispatch",
    )(meta, pad, lpos, f)


def _expert_kernel(blke_ref, nused_ref, xs_ref, w1_ref, w3_ref, w2_ref, y_ref, *, rb):
    i = pl.program_id(0)

    @pl.when(i < nused_ref[0])
    def _():
        x = _interleaved_to_rows(xs_ref, rb).astype(BF16)
        a = jnp.dot(x, w1_ref[0], preferred_element_type=F32)
        b = jnp.dot(x, w3_ref[0], preferred_element_type=F32)
        act = (jax.nn.silu(a) * b).astype(BF16)
        _rows_to_interleaved(y_ref, jnp.dot(act, w2_ref[0], preferred_element_type=F32))

    @pl.when(i >= nused_ref[0])
    def _():
        y_ref[...] = jnp.zeros_like(y_ref)


def _experts(blke, nused, xs, w1, w3, w2):
    E, D, Fe = w1.shape
    total = xs.shape[0] // SUBROWS
    rb = MOE_ROWS
    nblk = total // rb
    wspec = lambda shape: pl.BlockSpec(shape, lambda i, be, nu: (be[i], 0, 0), pipeline_mode=pl.Buffered(1))
    return pl.pallas_call(
        functools.partial(_expert_kernel, rb=rb),
        grid_spec=pltpu.PrefetchScalarGridSpec(
            num_scalar_prefetch=2,
            grid=(nblk,),
            in_specs=[
                pl.BlockSpec((rb * SUBROWS, LANES), lambda i, be, nu: (jnp.minimum(i, nu[0] - 1), 0)),
                wspec((1, D, Fe)), wspec((1, D, Fe)), wspec((1, Fe, D)),
            ],
            out_specs=pl.BlockSpec((rb * SUBROWS, LANES), lambda i, be, nu: (i, 0)),
        ),
        out_shape=jax.ShapeDtypeStruct((total * SUBROWS, LANES), F32),
        compiler_params=_cparams("arbitrary"),
        name="moe_experts",
    )(blke, nused, xs, w1, w3, w2)


def _combine_kernel(meta_ref, nxt_ref, yb_ref, lcol_ref, wcol_ref, h_ref, g3_ref, o_ref, buf, sems, *, td, ne):
    i = pl.program_id(0)
    slot = i % 2

    def fetch(m_ref, s):
        for e in range(ne):
            _segment_copies(m_ref[0, 0, e], yb_ref, m_ref[0, 0, 2 * ne + e], buf.at[s], m_ref[0, 0, ne + e],
                            sems.at[s], td)

    @pl.when(i == 0)
    def _():
        fetch(meta_ref, 0)

    @pl.when(i + 1 < pl.num_programs(0))
    def _():
        fetch(nxt_ref, 1 - slot)

    for h in range(2):
        pltpu.make_async_copy(yb_ref.at[pl.ds(0, td * SUBROWS), :],
                              buf.at[slot, pl.ds(h * td * SUBROWS, td * SUBROWS), :], sems.at[slot]).wait()
    yb = _interleaved_to_rows(buf.at[slot], 2 * td).astype(BF16)
    lc = lcol_ref[0]
    w = wcol_ref[...]
    c = lax.broadcasted_iota(jnp.int32, (td, 2 * td), 1).astype(F32)
    pick0 = jnp.where(c == lc[:, 0:1], 1.0, 0.0).astype(BF16)
    pick1 = jnp.where(c == lc[:, 1:2], 1.0, 0.0).astype(BF16)
    f = (w[:, 0:1] * jnp.dot(pick0, yb, preferred_element_type=F32)
         + w[:, 1:2] * jnp.dot(pick1, yb, preferred_element_type=F32))
    o_ref[...] = h_ref[...] + _rms(f, g3_ref[...])


def _combine(meta, yb, lcol, wcol, h3, g3):
    N, D = h3.shape
    ntile, td, _ = lcol.shape
    ne = N_EXPERTS
    kern = functools.partial(_combine_kernel, td=td, ne=ne)
    return pl.pallas_call(
        kern,
        grid=(ntile,),
        in_specs=[
            pl.BlockSpec((1, 1, META_ROWS), lambda i: (i, 0, 0), memory_space=pltpu.SMEM),
            pl.BlockSpec((1, 1, META_ROWS), lambda i: (jnp.minimum(i + 1, ntile - 1), 0, 0),
                         memory_space=pltpu.SMEM),
            pl.BlockSpec(memory_space=pl.ANY),
            pl.BlockSpec((1, td, LANES), lambda i: (i, 0, 0)),
            pl.BlockSpec((td, LANES), lambda i: (i, 0)),
            pl.BlockSpec((td, D), lambda i: (i, 0)),
            _const_spec((1, D)),
        ],
        out_specs=pl.BlockSpec((td, D), lambda i: (i, 0)),
        out_shape=jax.ShapeDtypeStruct((N, D), F32),
        scratch_shapes=[pltpu.VMEM((2, 2 * td * SUBROWS, LANES), F32), pltpu.SemaphoreType.DMA((2,))],
        compiler_params=_cparams("arbitrary"),
        name="moe_combine",
    )(meta, meta, yb, lcol, wcol, h3, g3)


def _rope_lane_tables():
    half = ROT_DIM // 2
    inv_freq = 1.0 / (ROPE_THETA ** (np.arange(half, dtype=np.float32) / half))
    invf = np.zeros((1, LANES), np.float32)
    invf[0, :half] = inv_freq
    invf[0, half:ROT_DIM] = inv_freq
    sneg = np.zeros((1, LANES), np.float32)
    sneg[0, :half] = -1.0
    spos = np.zeros((1, LANES), np.float32)
    spos[0, half:ROT_DIM] = 1.0
    return jnp.asarray(invf), jnp.asarray(sneg), jnp.asarray(spos)


def kernel(x, positions, norms, s5_lam_re, s5_lam_im, s5_log_dt, s5_b_re, s5_b_im, s5_c_re, s5_c_im, s5_d, s5_w_glu, kv_norm, w_kv, w_q, w_o, ffn_w1, ffn_w3, ffn_w2, moe_router, moe_bias, moe_w1, moe_w3, moe_w2):
    B, S, D = x.shape
    N = B * S
    HD = N_HEADS * HEAD_DIM
    E = moe_router.shape[-1]
    row = lambda v: v.astype(F32).reshape(1, -1)

    a2_r, a2_i, b2, c2, d2 = _s5_params(s5_lam_re[0], s5_lam_im[0], s5_log_dt[0], s5_b_re[0], s5_b_im[0],
                                        s5_c_re[0], s5_c_im[0])
    h1 = _s5_layer(x, row(norms[0, 0]), row(norms[0, 1]), row(s5_d[0]), a2_r, a2_i, b2, c2, d2,
                   s5_w_glu[0].astype(BF16))
    h2 = _dense_ffn(h1.reshape(N, D), row(norms[0, 2]), row(norms[0, 3]),
                    ffn_w1[0].astype(BF16), ffn_w3[0].astype(BF16), ffn_w2[0].astype(BF16))

    invf, sneg, spos = _rope_lane_tables()
    posb = jnp.broadcast_to(positions.astype(F32)[:, :, None], (B, S, LANES))
    q, k5, vt5, kmean = _qkv(h2.reshape(B, S, D), posb, invf, sneg, spos, row(norms[1, 0]), row(kv_norm),
                             w_q[0].astype(BF16), w_kv[:, :HD].astype(BF16), w_kv[:, HD:].T.astype(BF16))
    h3, f_in, e_tiles, wcol = _moba_layer(q, k5, vt5, kmean, h2.reshape(B, S, D), row(norms[1, 1]), w_o[0].astype(BF16),
                                          row(norms[1, 2]), moe_router[0].T.astype(F32),
                                          moe_bias[0].astype(F32).reshape(E, 1))
    h3 = h3.reshape(N, D)

    assert MOBA_BLOCK == DISPATCH_TILE
    total = 2 * N + E * MOE_ROWS
    nblk = total // MOE_ROWS
    nblk_pad = -(-nblk // LANES) * LANES
    lpos, lcol, meta, pad, blke, nused = _rank(e_tiles, E, nblk_pad)
    meta = meta[:, :, 0].reshape(-1, 1, META_ROWS)
    xs = _dispatch(meta, pad[:, 0], lpos, f_in.reshape(N, D), total)
    yb = _experts(blke[0, :nblk], nused[0, :1], xs,
                  moe_w1[0].astype(BF16), moe_w3[0].astype(BF16), moe_w2[0].astype(BF16))
    out = _combine(meta, yb, lcol, wcol.reshape(N, LANES), h3, row(norms[1, 3]))
    return out.reshape(B, S, D)
```

```python
import functools
import math

import numpy as np
import jax
import jax.numpy as jnp
from jax import lax
from jax.experimental import pallas as pl
from jax.experimental.pallas import tpu as pltpu

F32 = jnp.float32
BF16 = jnp.bfloat16

NORM_EPS = 1e-6
NEG_INF = -1e30

SSM_GROUP = 16
N_HEADS = 8
HEAD_DIM = 128
ROT_DIM = HEAD_DIM // 4
ROPE_THETA = 500000.0
MOBA_BLOCK = 256
MOBA_TOPK = 3
N_EXPERTS = 8

Q_SCALE_LOG2 = HEAD_DIM ** -0.5 * math.log2(math.e)

KMEAN_PIECES = 3

LANES = 128
GROUPS_PER_TILE = LANES // SSM_GROUP

VMEM_LIMIT = 56 * 1024 * 1024

S5_CHUNK = 64
S5_ROW_PAD = 8
ROW_TILE = 512
MOE_ROWS = 512
DISPATCH_TILE = 256


def _cparams(*sem):
    return pltpu.CompilerParams(dimension_semantics=sem, vmem_limit_bytes=VMEM_LIMIT)


def _rms(x, g):
    return x * lax.rsqrt(jnp.mean(x * x, axis=-1, keepdims=True) + NORM_EPS) * g


def _const_spec(shape):
    n = len(shape)
    return pl.BlockSpec(shape, lambda *_: (0,) * n, pipeline_mode=pl.Buffered(1))


def _zoh(lr, li, dt):
    zr = lr * dt
    zi = li * dt
    mag = jnp.exp(zr)
    return mag * jnp.cos(zr * 0 + zi), mag * jnp.sin(zi)


def _cmul(xr, xi, yr, yi):
    return xr * yr - xi * yi, xr * yi + xi * yr


def _s5_prep_kernel(lrb_ref, lib_ref, lrc_ref, lic_ref, ldt_ref, br_ref, bi_ref, cr_ref, ci_ref,
                    a2r_ref, a2i_ref, bbr_ref, bbi_ref, abr_ref, abi_ref, car_ref, cai_ref, c2r_ref, c2i_ref):
    dt = jnp.exp(ldt_ref[...])
    lr = lrb_ref[...]
    li = lib_ref[...]
    a_r, a_i = _zoh(lr, li, dt)
    den = lr * lr + li * li
    coef_r = ((a_r - 1.0) * lr + a_i * li) / den
    coef_i = (a_i * lr - (a_r - 1.0) * li) / den
    bb_r, bb_i = _cmul(coef_r, coef_i, br_ref[...], bi_ref[...])
    bbr_ref[...] = bb_r
    bbi_ref[...] = bb_i
    abr_ref[...], abi_ref[...] = _cmul(a_r, a_i, bb_r, bb_i)
    a2r_ref[...], a2i_ref[...] = _cmul(a_r, a_i, a_r, a_i)
    ac_r, ac_i = _zoh(lrc_ref[...], lic_ref[...], dt)
    ca_r, ca_i = _cmul(cr_ref[...], ci_ref[...], ac_r, ac_i)
    car_ref[...] = ca_r
    cai_ref[...] = ca_i
    c2r_ref[...], c2i_ref[...] = _cmul(ca_r, ca_i, ac_r, ac_i)


def _s5_fold_kernel(b_ref, ab_ref, c_ref, d0_ref, d1_ref):
    for k in range(b_ref.shape[0]):
        d0_ref[k] = jnp.dot(b_ref[k], c_ref[k], precision=lax.Precision.HIGHEST, preferred_element_type=F32)
        d1_ref[k] = jnp.dot(ab_ref[k], c_ref[k], precision=lax.Precision.HIGHEST, preferred_element_type=F32)


def _s5_params(lam_re, lam_im, log_dt, b_re, b_im, c_re, c_im):
    G, P = lam_re.shape
    Q = b_re.shape[-1]
    gt = GROUPS_PER_TILE
    nt = G // gt
    f = lambda a: a.astype(F32)
    rep = lambda a: jnp.repeat(f(a), Q, axis=1)
    til = lambda a: jnp.tile(f(a), (1, Q))
    out = jax.ShapeDtypeStruct((G, P * Q), F32)
    a2_r, a2_i, bb_r, bb_i, ab_r, ab_i, ca_r, ca_i, c2_r, c2_i = pl.pallas_call(
        _s5_prep_kernel, out_shape=(out,) * 10, name="s5_prep",
    )(rep(lam_re), rep(lam_im), til(lam_re), til(lam_im), f(log_dt).reshape(G, 1),
      f(b_re).reshape(G, P * Q), f(b_im).reshape(G, P * Q), f(c_re).reshape(G, Q * P), f(c_im).reshape(G, Q * P))

    eye = jnp.eye(gt, dtype=F32)
    def bmat(bb):
        return jnp.einsum('kgpi,gh->kgihp', bb.reshape(nt, gt, P, Q), eye).reshape(nt, gt * Q, gt * P)
    def cmat(c):
        return jnp.einsum('kgip,gh->kgphi', c.reshape(nt, gt, Q, P), eye).reshape(nt, gt * P, gt * Q)
    bmat2 = lambda r, i: jnp.concatenate([bmat(r), bmat(i)], axis=-1)
    cmat2 = lambda r, i: jnp.concatenate([cmat(r), -cmat(i)], axis=1)
    b_bd, ab_bd = bmat2(bb_r, bb_i), bmat2(ab_r, ab_i)
    c_bd = cmat2(f(c_re).reshape(G, Q * P), f(c_im).reshape(G, Q * P))
    dshape = jax.ShapeDtypeStruct((nt, LANES, LANES), F32)
    d0, d1 = pl.pallas_call(_s5_fold_kernel, out_shape=(dshape, dshape), name="s5_fold",
                            compiler_params=pltpu.CompilerParams(vmem_limit_bytes=VMEM_LIMIT))(b_bd, ab_bd, c_bd)
    b2 = jnp.concatenate([ab_bd, b_bd], axis=1).astype(BF16)
    c2 = jnp.concatenate([cmat2(ca_r, ca_i), cmat2(c2_r, c2_i)], axis=2).astype(BF16)
    d2 = jnp.concatenate([jnp.concatenate([d0, d1], axis=2),
                          jnp.concatenate([jnp.zeros_like(d0), d0], axis=2)], axis=1).astype(BF16)
    sel = lambda a: a.reshape(G, P, Q)[:, :, 0].reshape(nt, 1, gt * P)
    return sel(a2_r), sel(a2_i), b2, c2, d2


def _s5_layer_kernel(x_ref, g0_ref, g1_ref, d_ref, ar_ref, ai_ref, b2_ref, c2_ref, d2_ref, wglu_ref,
                     o_ref, bm, tm, bu, y_tm, state, *, nb, lc, d, nt, hw):
    c = pl.program_id(0)
    pitch = lc + S5_ROW_PAD
    npair = lc // 2

    @pl.when(c == 0)
    def _():
        state[...] = jnp.zeros_like(state)

    for b in range(nb):
        u = _rms(x_ref[b], g0_ref[...])
        for k in range(nt):
            bm[k, b * pitch:b * pitch + lc, :] = u[:, k * LANES:(k + 1) * LANES]
    for t in range(lc):
        for k in range(nt):
            tm[k, t * nb:(t + 1) * nb, :] = bm[k, pl.ds(t, nb, stride=pitch), :]

    for k in range(nt):
        uk = tm[k]
        u4 = uk.reshape(npair, 2, nb, LANES)
        u2 = jnp.concatenate([u4[:, 0].reshape(npair * nb, LANES), u4[:, 1].reshape(npair * nb, LANES)],
                             axis=1).astype(BF16)
        bu[...] = jnp.dot(u2, b2_ref[k], preferred_element_type=F32)
        a_r = jnp.broadcast_to(ar_ref[k], (nb, hw))
        a_i = jnp.broadcast_to(ai_ref[k], (nb, hw))

        def step(t, carry):
            s_r, s_i = carry
            r0 = pl.multiple_of(t * nb, nb)
            n_r = a_r * s_r - a_i * s_i + bu[pl.ds(r0, nb), 0:hw]
            n_i = a_r * s_i + a_i * s_r + bu[pl.ds(r0, nb), hw:2 * hw]
            bu[pl.ds(r0, nb), 0:hw] = s_r
            bu[pl.ds(r0, nb), hw:2 * hw] = s_i
            return n_r, n_i

        s_r, s_i = lax.fori_loop(0, npair, step, (state[k, :, 0:hw], state[k, :, hw:2 * hw]), unroll=True)
        state[k, :, 0:hw] = s_r
        state[k, :, hw:2 * hw] = s_i
        y2 = (jnp.dot(bu[...].astype(BF16), c2_ref[k], preferred_element_type=F32)
              + jnp.dot(u2, d2_ref[k], preferred_element_type=F32))
        yk = jnp.concatenate([y2[:, :LANES].reshape(npair, 1, nb, LANES),
                              y2[:, LANES:].reshape(npair, 1, nb, LANES)], axis=1).reshape(lc * nb, LANES)
        y_tm[:, k * LANES:(k + 1) * LANES] = yk + d_ref[:, k * LANES:(k + 1) * LANES] * uk

    gl = jax.nn.gelu(y_tm[...]).astype(BF16)
    z = jnp.dot(gl, wglu_ref[...], preferred_element_type=F32)
    mix = z[:, :d] * jax.nn.sigmoid(z[:, d:])
    mixn = _rms(mix, g1_ref[...])
    for k in range(nt):
        tm[k] = mixn[:, k * LANES:(k + 1) * LANES]
    for t in range(lc):
        for k in range(nt):
            bm[k, pl.ds(t, nb, stride=pitch), :] = tm[k, t * nb:(t + 1) * nb, :]
    for b in range(nb):
        for k in range(nt):
            sl = slice(k * LANES, (k + 1) * LANES)
            o_ref[b, :, sl] = x_ref[b, :, sl] + bm[k, b * pitch:b * pitch + lc, :]


def _s5_layer(x, g0, g1, d_skip, a2_r_t, a2_i_t, b2, c2, d2, w_glu):
    B, S, D = x.shape
    lc = S5_CHUNK
    nt = D // LANES
    hw = a2_r_t.shape[-1]
    rows = B * lc
    kern = functools.partial(_s5_layer_kernel, nb=B, lc=lc, d=D, nt=nt, hw=hw)
    return pl.pallas_call(
        kern,
        grid=(S // lc,),
        in_specs=[
            pl.BlockSpec((B, lc, D), lambda c: (0, c, 0)),
            _const_spec((1, D)), _const_spec((1, D)), _const_spec((1, D)),
            _const_spec((nt, 1, hw)), _const_spec((nt, 1, hw)),
            _const_spec((nt, 2 * LANES, 2 * hw)), _const_spec((nt, 2 * hw, 2 * LANES)),
            _const_spec((nt, 2 * LANES, 2 * LANES)),
            _const_spec((D, 2 * D)),
        ],
        out_specs=pl.BlockSpec((B, lc, D), lambda c: (0, c, 0)),
        out_shape=jax.ShapeDtypeStruct((B, S, D), F32),
        scratch_shapes=[
            pltpu.VMEM((nt, B * (lc + S5_ROW_PAD), LANES), F32), pltpu.VMEM((nt, rows, LANES), F32),
            pltpu.VMEM((rows // 2, 2 * hw), F32), pltpu.VMEM((rows, D), F32),
            pltpu.VMEM((nt, B, 2 * hw), F32),
        ],
        compiler_params=_cparams("arbitrary"),
        name="s5_layer",
    )(x, g0, g1, d_skip, a2_r_t, a2_i_t, b2, c2, d2, w_glu)


def _rotary(xh, cs, s_neg, s_pos):
    half = ROT_DIM // 2
    return xh * cs + pltpu.roll(xh, LANES - half, 1) * s_neg + pltpu.roll(xh, half, 1) * s_pos


def _ffn_qkv_kernel(h_ref, g2_ref, g3_ref, w1_ref, w3_ref, w2_ref,
                    pos_ref, invf_ref, sneg_ref, spos_ref, gq_ref, gkv_ref, wq_ref, wk_ref, wvt_ref,
                    o_ref, q_ref, k_ref, vt_ref, km_ref, *, nblk_tile, nblk_seq, tiles_per_seq):
    ti = pl.program_id(0) % tiles_per_seq
    h1 = h_ref[...]
    f_in = _rms(h1, g2_ref[...]).astype(BF16)
    a = jnp.dot(f_in, w1_ref[...], preferred_element_type=F32)
    b = jnp.dot(f_in, w3_ref[...], preferred_element_type=F32)
    act = (jax.nn.silu(a) * b).astype(BF16)
    f = jnp.dot(act, w2_ref[...], preferred_element_type=F32)
    h = h1 + _rms(f, g3_ref[...])
    o_ref[...] = h

    r = lax.rsqrt(jnp.mean(h * h, axis=-1, keepdims=True) + NORM_EPS)
    hq = (h * r * gq_ref[...]).astype(BF16)
    hk = (h * r * gkv_ref[...]).astype(BF16)
    ang = pos_ref[0] * invf_ref[...]
    cs = jnp.cos(ang)
    sn = jnp.sin(ang)
    s_neg = sn * sneg_ref[...]
    s_pos = sn * spos_ref[...]
    q = jnp.dot(hq, wq_ref[...], preferred_element_type=F32)
    k = jnp.dot(hk, wk_ref[...], preferred_element_type=F32)
    vt = lax.dot_general(wvt_ref[...], hk, (((1,), (1,)), ((), ())), preferred_element_type=F32)
    for hh in range(N_HEADS):
        sl = slice(hh * HEAD_DIM, (hh + 1) * HEAD_DIM)
        q_ref[0, hh] = (_rotary(q[:, sl], cs, s_neg, s_pos) * Q_SCALE_LOG2).astype(BF16)
        kr = _rotary(k[:, sl], cs, s_neg, s_pos)
        for j in range(nblk_tile):
            rs = slice(j * MOBA_BLOCK, (j + 1) * MOBA_BLOCK)
            k_ref[0, hh, j] = kr[rs].astype(BF16)
            vt_ref[0, hh, j] = vt[sl, rs].astype(BF16)
            km = jnp.mean(kr[rs], axis=0, keepdims=True)
            for piece in range(KMEAN_PIECES):
                part = km.astype(BF16).astype(F32)
                km_ref[0, hh, pl.ds(piece * nblk_seq + ti * nblk_tile + j, 1), :] = part
                km = km - part


def _ffn_qkv(h1, B, g2, g3, w1, w3, w2, posb, invf, sneg, spos, gq, gkv, wq, wk, wvt):
    N, D = h1.shape
    S = N // B
    Fd = w1.shape[1]
    T = ROW_TILE
    H, Dh, blk = N_HEADS, HEAD_DIM, MOBA_BLOCK
    nb = S // blk
    nbt = T // blk
    tps = S // T
    kern = functools.partial(_ffn_qkv_kernel, nblk_tile=nbt, nblk_seq=nb, tiles_per_seq=tps)
    return pl.pallas_call(
        kern,
        grid=(N // T,),
        in_specs=[
            pl.BlockSpec((T, D), lambda i: (i, 0)),
            _const_spec((1, D)), _const_spec((1, D)),
            _const_spec((D, Fd)), _const_spec((D, Fd)), _const_spec((Fd, D)),
            pl.BlockSpec((1, T, LANES), lambda i: (i // tps, i % tps, 0)),
            _const_spec((1, LANES)), _const_spec((1, LANES)), _const_spec((1, LANES)),
            _const_spec((1, D)), _const_spec((1, D)),
            _const_spec((D, D)), _const_spec((D, D)), _const_spec((D, D)),
        ],
        out_specs=[
            pl.BlockSpec((T, D), lambda i: (i, 0)),
            pl.BlockSpec((1, H, T, Dh), lambda i: (i // tps, 0, i % tps, 0)),
            pl.BlockSpec((1, H, nbt, blk, Dh), lambda i: (i // tps, 0, i % tps, 0, 0)),
            pl.BlockSpec((1, H, nbt, Dh, blk), lambda i: (i // tps, 0, i % tps, 0, 0)),
            pl.BlockSpec((1, H, KMEAN_PIECES * nb, Dh), lambda i: (i // tps, 0, 0, 0)),
        ],
        out_shape=[
            jax.ShapeDtypeStruct((N, D), F32),
            jax.ShapeDtypeStruct((B, H, S, Dh), BF16),
            jax.ShapeDtypeStruct((B, H, nb, blk, Dh), BF16),
            jax.ShapeDtypeStruct((B, H, nb, Dh, blk), BF16),
            jax.ShapeDtypeStruct((B, H, KMEAN_PIECES * nb, Dh), F32),
        ],
        compiler_params=_cparams("arbitrary"),
        name="ffn_qkv",
    )(h1, g2, g3, w1, w3, w2, posb, invf, sneg, spos, gq, gkv, wq, wk, wvt)


def _moba_kernel(q_ref, k_ref, vt_ref, km_ref, h_ref, g1_ref, wo_ref, g2_ref, wrt_ref, br_ref,
                 o_ref, f_ref, e_ref, wcol_ref, sel_scr, st_scr, acc_scr, o_scr, *, nb, k_sel):
    j = pl.program_id(1)
    blk = MOBA_BLOCK
    nt_dims = (((1,), (1,)), ((), ()))
    blk_id = lax.broadcasted_iota(jnp.int32, (nb, blk), 0)
    past = blk_id < j
    causal = (lax.broadcasted_iota(jnp.int32, (blk, blk), 0)
              <= lax.broadcasted_iota(jnp.int32, (blk, blk), 1))

    for hh in range(N_HEADS):
        parts = lax.dot_general(km_ref[0, hh].astype(BF16), q_ref[0, hh], nt_dims,
                                preferred_element_type=F32)
        gate = parts[0:nb]
        for piece in range(1, KMEAN_PIECES):
            gate = gate + parts[piece * nb:(piece + 1) * nb]
        gate = jnp.where(past, gate, NEG_INF)
        rank = jnp.zeros((nb, blk), F32)
        for m in range(nb):
            gm = gate[m:m + 1, :]
            beats = jnp.where(gm > gate, 1.0, jnp.where(gm == gate, jnp.where(m < blk_id, 1.0, 0.0), 0.0))
            rank = rank + beats
        sel_scr[hh] = jnp.where(past, jnp.where(rank < k_sel, 1.0, 0.0), 0.0)

    def qk_pass(n, ms, own):
        out = []
        for hh in range(N_HEADS):
            st = lax.dot_general(k_ref[0, hh, n], q_ref[0, hh], nt_dims, preferred_element_type=F32)
            allowed = causal if own else sel_scr[hh, pl.ds(n, 1), :] > 0.5
            st = jnp.where(allowed, st, NEG_INF)
            st_scr[hh, n] = st
            out.append(jnp.maximum(ms[hh], jnp.max(st, axis=0, keepdims=True)))
        return tuple(out)

    m_init = tuple(jnp.full((1, blk), NEG_INF, F32) for _ in range(N_HEADS))
    m_own = qk_pass(j, m_init, True)
    def past_blocks(step, carry):
        done = 0
        for group in (4, 2, 1):
            trips = (j - done) // group

            def body(i, c, group=group, done=done):
                for g in range(group):
                    c = step(done + i * group + g, c)
                return c

            carry = lax.fori_loop(0, trips, body, carry)
            done = done + trips * group
        return carry

    m_fin = past_blocks(lambda n, ms: qk_pass(n, ms, False), m_own)

    def pv_pass(n, ls, first):
        out = []
        for hh in range(N_HEADS):
            p = jnp.exp2(st_scr[hh, n] - m_fin[hh])
            out.append(ls[hh] + jnp.sum(p, axis=0, keepdims=True))
            pv = jnp.dot(vt_ref[0, hh, n], p.astype(BF16), preferred_element_type=F32)
            acc_scr[hh] = pv if first else acc_scr[hh] + pv
        return tuple(out)

    l_own = pv_pass(j, tuple(jnp.zeros((1, blk), F32) for _ in range(N_HEADS)), True)
    l_fin = past_blocks(lambda n, ls: pv_pass(n, ls, False), l_own)
    for hh in range(N_HEADS):
        o_t = acc_scr[hh] / l_fin[hh]
        o_scr[:, hh * HEAD_DIM:(hh + 1) * HEAD_DIM] = o_t.T.astype(BF16)

    mix = jnp.dot(o_scr[...], wo_ref[...], preferred_element_type=F32)
    h3 = h_ref[0] + _rms(mix, g1_ref[...])
    o_ref[0] = h3
    f = _rms(h3, g2_ref[...])
    f_ref[0] = f.astype(BF16)
    e_ref[0], wcol_ref[0] = _route(f, wrt_ref, br_ref)


def _moba_layer(q, k5, vt5, kmean, h2, g1, wo, g2, wrt, br):
    B, S, D = h2.shape
    H, Dh, blk = N_HEADS, HEAD_DIM, MOBA_BLOCK
    nb = S // blk
    E = wrt.shape[0]
    kern = functools.partial(_moba_kernel, nb=nb, k_sel=min(MOBA_TOPK, nb))
    return pl.pallas_call(
        kern,
        grid=(B, nb),
        in_specs=[
            pl.BlockSpec((1, H, blk, Dh), lambda b, j: (b, 0, j, 0)),
            pl.BlockSpec((1, H, nb, blk, Dh), lambda b, j: (b, 0, 0, 0, 0)),
            pl.BlockSpec((1, H, nb, Dh, blk), lambda b, j: (b, 0, 0, 0, 0)),
            pl.BlockSpec((1, H, KMEAN_PIECES * nb, Dh), lambda b, j: (b, 0, 0, 0)),
            pl.BlockSpec((1, blk, D), lambda b, j: (b, j, 0)),
            _const_spec((1, D)), _const_spec((D, D)),
            _const_spec((1, D)), _const_spec((E, D)), _const_spec((E, 1)),
        ],
        out_specs=[
            pl.BlockSpec((1, blk, D), lambda b, j: (b, j, 0)),
            pl.BlockSpec((1, blk, D), lambda b, j: (b, j, 0)),
            pl.BlockSpec((1, 2, blk), lambda b, j: (b * nb + j, 0, 0)),
            pl.BlockSpec((1, blk, LANES), lambda b, j: (b * nb + j, 0, 0)),
        ],
        out_shape=[
            jax.ShapeDtypeStruct((B, S, D), F32),
            jax.ShapeDtypeStruct((B, S, D), BF16),
            jax.ShapeDtypeStruct((B * nb, 2, blk), jnp.int32),
            jax.ShapeDtypeStruct((B * nb, blk, LANES), F32),
        ],
        scratch_shapes=[
            pltpu.VMEM((H, nb, blk), F32), pltpu.VMEM((H, nb, blk, blk), F32),
            pltpu.VMEM((H, Dh, blk), F32), pltpu.VMEM((blk, D), BF16),
        ],
        compiler_params=_cparams("parallel", "arbitrary"),
        name="moba_attn",
    )(q, k5, vt5, kmean, h2, g1, wo, g2, wrt, br)


def _route(f, wrt_ref, br_ref):
    nt_dims = (((1,), (1,)), ((), ()))
    w = wrt_ref[...]
    ne = w.shape[0]
    w_hi = w.astype(BF16).astype(F32)
    f_hi = f.astype(BF16)
    f_lo = (f - f_hi.astype(F32)).astype(BF16)
    w2 = jnp.concatenate([w_hi, w - w_hi], axis=0).astype(BF16)
    a = lax.dot_general(w2, f_hi, nt_dims, preferred_element_type=F32)
    b = lax.dot_general(w2[:ne], f_lo, nt_dims, preferred_element_type=F32)
    logit = a[:ne] + a[ne:] + b + br_ref[...]
    t = logit.shape[1]
    eidx = lax.broadcasted_iota(jnp.int32, (ne, t), 0)
    m1 = jnp.max(logit, axis=0, keepdims=True)
    e1 = jnp.min(jnp.where(logit == m1, eidx, ne), axis=0, keepdims=True)
    rest = jnp.where(eidx == e1, -jnp.inf, logit)
    m2 = jnp.max(rest, axis=0, keepdims=True)
    e2 = jnp.min(jnp.where(rest == m2, eidx, ne), axis=0, keepdims=True)
    ex = jnp.exp(m2 - m1)
    den = 1.0 + ex
    wpad = jnp.concatenate([1.0 / den, ex / den, jnp.zeros((LANES - 2, t), F32)], axis=0)
    return jnp.concatenate([e1, e2], axis=0), wpad.T


META_ROWS = 32


def _rank_kernel(e_ref, lpos_ref, lcol_ref, meta_ref, pad_ref, blke_ref, nused_ref, blkv_ref,
                 *, ne, rows_log2, nblk_pad):
    ntile, _, t = e_ref.shape
    eidx = lax.broadcasted_iota(jnp.int32, (ne, t), 0)

    def onehots(i):
        e = e_ref[i]
        oh0 = jnp.where(eidx == e[0:1, :], 1.0, 0.0)
        oh1 = jnp.where(eidx == e[1:2, :], 1.0, 0.0)
        return oh0, oh1

    def excl_cumsum(col):
        run = jnp.zeros((1, 1), jnp.int32)
        parts = []
        for e in range(ne):
            parts.append(run)
            run = run + col[e:e + 1, :]
        return jnp.concatenate(parts, axis=0), run

    def count(i, cnt):
        oh0, oh1 = onehots(i)
        return cnt + jnp.sum(oh0 + oh1, axis=1, keepdims=True).astype(jnp.int32)

    cnt = lax.fori_loop(0, ntile, count, jnp.zeros((ne, 1), jnp.int32), unroll=8)
    pcnt = ((cnt + ((1 << rows_log2) - 1)) >> rows_log2) << rows_log2
    pstart, ptotal = excl_cumsum(pcnt)
    pend = pstart + pcnt

    upper = jnp.where(lax.broadcasted_iota(jnp.int32, (t, t), 0) < lax.broadcasted_iota(jnp.int32, (t, t), 1),
                      1.0, 0.0).astype(BF16)
    lane_b = lambda col: jnp.broadcast_to(col, (ne, LANES))

    def place(i, base):
        oh0, oh1 = onehots(i)
        both = oh0 + oh1
        before = jnp.dot(both.astype(BF16), upper, preferred_element_type=F32).astype(jnp.int32)
        cnt_i = jnp.sum(both, axis=1, keepdims=True).astype(jnp.int32)
        loff, _ = excl_cumsum(cnt_i)
        slot = loff + before
        l0 = jnp.sum(jnp.where(oh0 > 0.5, slot, 0), axis=0, keepdims=True)
        l1 = jnp.sum(jnp.where(oh1 > 0.5, slot, 0), axis=0, keepdims=True)
        lpos_ref[i] = jnp.concatenate([l0, l1], axis=0)
        lpad = jnp.concatenate([l0.astype(F32), l1.astype(F32), jnp.zeros((LANES - 2, t), F32)], axis=0)
        lcol_ref[i] = lpad.T
        meta_ref[i] = jnp.concatenate(
            [lane_b(cnt_i), lane_b(loff), lane_b(pstart + base),
             jnp.zeros((META_ROWS - 3 * ne, LANES), jnp.int32)], axis=0)
        return base + cnt_i

    lax.fori_loop(0, ntile, place, jnp.zeros((ne, 1), jnp.int32), unroll=4)

    pad_ref[...] = jnp.concatenate([lane_b(pstart + cnt), lane_b(pcnt - cnt),
                                    jnp.broadcast_to(ptotal, (ne, LANES))], axis=0)
    blk_row = lax.broadcasted_iota(jnp.int32, (ne, nblk_pad), 1) << rows_log2
    blke = jnp.sum(jnp.where(pend <= blk_row, 1, 0), axis=0, keepdims=True)
    blke_ref[...] = jnp.minimum(blke, ne - 1)
    nused_ref[...] = jnp.broadcast_to(ptotal >> rows_log2, (1, LANES))
    held = jnp.minimum(pstart + cnt - blk_row, 1 << rows_log2)
    blkv_ref[...] = jnp.sum(jnp.where(blk_row >= pstart, jnp.where(blk_row < pend, held, 0), 0),
                            axis=0, keepdims=True)


def _rank(e3, ne, nblk_pad):
    ntile, _, t = e3.shape
    kern = functools.partial(_rank_kernel, ne=ne, rows_log2=int(math.log2(MOE_ROWS)), nblk_pad=nblk_pad)
    return pl.pallas_call(
        kern,
        out_shape=[
            jax.ShapeDtypeStruct((ntile, 2, t), jnp.int32),
            jax.ShapeDtypeStruct((ntile, t, LANES), F32),
            jax.ShapeDtypeStruct((ntile, META_ROWS, LANES), jnp.int32),
            jax.ShapeDtypeStruct((3 * ne, LANES), jnp.int32),
            jax.ShapeDtypeStruct((1, nblk_pad), jnp.int32),
            jax.ShapeDtypeStruct((1, LANES), jnp.int32),
            jax.ShapeDtypeStruct((1, nblk_pad), jnp.int32),
        ],
        compiler_params=pltpu.CompilerParams(vmem_limit_bytes=VMEM_LIMIT),
        name="moe_rank",
    )(e3)


SUBROWS = 8


def _rows_to_interleaved(ref, x):
    n = x.shape[0]
    for s in range(SUBROWS):
        ref[pl.ds(s, n, stride=SUBROWS), :] = x[:, s * LANES:(s + 1) * LANES]


def _interleaved_to_rows(ref, n):
    return jnp.concatenate([ref[pl.ds(s, n, stride=SUBROWS), :] for s in range(SUBROWS)], axis=1)


def _segment_copies(count, src_ref, src_row, dst_ref, dst_row, sem, max_piece, wait=False, advance_src=True):
    p = max_piece
    while p >= 1:
        hit = (count & p) != 0

        @pl.when(hit)
        def _(p=p, src_row=src_row, dst_row=dst_row):
            cp = pltpu.make_async_copy(
                src_ref.at[pl.ds(pl.multiple_of(src_row * SUBROWS, SUBROWS), p * SUBROWS), :],
                dst_ref.at[pl.ds(pl.multiple_of(dst_row * SUBROWS, SUBROWS), p * SUBROWS), :], sem)
            if wait:
                cp.wait()
            else:
                cp.start()

        step = count & p
        if advance_src:
            src_row = src_row + step
        dst_row = dst_row + step
        p //= 2


def _dispatch_kernel(meta_ref, pad_ref, lpos_ref, f_ref, xs_ref, srt, zbuf, sems, zsem, *, td, ne):
    i = pl.program_id(0)

    @pl.when(i == 0)
    def _():
        zbuf[...] = jnp.zeros_like(zbuf)
        zrows = zbuf.shape[0] // SUBROWS
        total = xs_ref.shape[0] // SUBROWS
        for wait in (False, True):
            for e in range(ne):
                _segment_copies(pad_ref[ne + e], zbuf, 0, xs_ref, pad_ref[e], zsem,
                                zrows, wait=wait, advance_src=False)
            for r in range(0, ne * MOE_ROWS, zrows):
                row = pad_ref[2 * ne] + r

                @pl.when(row < total)
                def _(row=row):
                    cp = pltpu.make_async_copy(
                        zbuf, xs_ref.at[pl.ds(pl.multiple_of(row * SUBROWS, SUBROWS), zrows * SUBROWS), :], zsem)
                    if wait:
                        cp.wait()
                    else:
                        cp.start()

    slot = i % 2
    lp = lpos_ref[0]
    r = lax.broadcasted_iota(jnp.int32, (2 * td, td), 0)
    perm = (jnp.where(r == lp[0:1, :], 1.0, 0.0) + jnp.where(r == lp[1:2, :], 1.0, 0.0)).astype(BF16)
    _rows_to_interleaved(srt.at[slot], jnp.dot(perm, f_ref[...], preferred_element_type=F32))
    for e in range(ne):
        _segment_copies(meta_ref[0, 0, e], srt.at[slot], meta_ref[0, 0, ne + e], xs_ref,
                        meta_ref[0, 0, 2 * ne + e], sems.at[slot], td)

    def drain(s):
        for h in range(2):
            pltpu.make_async_copy(srt.at[s, pl.ds(h * td * SUBROWS, td * SUBROWS), :],
                                  xs_ref.at[pl.ds(0, td * SUBROWS), :], sems.at[s]).wait()

    @pl.when(i > 0)
    def _():
        drain(1 - slot)

    @pl.when(i + 1 == pl.num_programs(0))
    def _():
        drain(slot)


def _dispatch(meta, pad, lpos, f, total):
    N, D = f.shape
    ntile, _, td = lpos.shape
    ne = N_EXPERTS
    assert D == SUBROWS * LANES
    kern = functools.partial(_dispatch_kernel, td=td, ne=ne)
    return pl.pallas_call(
        kern,
        grid=(ntile,),
        in_specs=[
            pl.BlockSpec((1, 1, META_ROWS), lambda i: (i, 0, 0), memory_space=pltpu.SMEM),
            pl.BlockSpec(memory_space=pltpu.SMEM),
            pl.BlockSpec((1, 2, td), lambda i: (i, 0, 0)),
            pl.BlockSpec((td, D), lambda i: (i, 0)),
        ],
        out_specs=pl.BlockSpec(memory_space=pl.ANY),
        out_shape=jax.ShapeDtypeStruct((total * SUBROWS, LANES), F32),
        scratch_shapes=[pltpu.VMEM((2, 2 * td * SUBROWS, LANES), F32),
                        pltpu.VMEM((MOE_ROWS // 2 * SUBROWS, LANES), F32),
                        pltpu.SemaphoreType.DMA((2,)), pltpu.SemaphoreType.DMA(())],
        compiler_params=_cparams("arbitrary"),
        name="moe_dispatch",
    )(meta, pad, lpos, f)


def _expert_kernel(blke_ref, nused_ref, blkv_ref, xs_ref, w1_ref, w3_ref, w2_ref, y_ref, *, rb):
    held = blkv_ref[pl.program_id(0)]
    half = rb // 2

    def swiglu(rows):
        x = _interleaved_to_rows(xs_ref, rows).astype(BF16)
        a = jnp.dot(x, w1_ref[0], preferred_element_type=F32)
        b = jnp.dot(x, w3_ref[0], preferred_element_type=F32)
        act = (jax.nn.silu(a) * b).astype(BF16)
        _rows_to_interleaved(y_ref, jnp.dot(act, w2_ref[0], preferred_element_type=F32))

    @pl.when(held > half)
    def _():
        swiglu(rb)

    @pl.when(jnp.logical_and(held > 0, held <= half))
    def _():
        swiglu(half)
        y_ref[pl.ds(half * SUBROWS, half * SUBROWS), :] = jnp.zeros((half * SUBROWS, LANES), F32)

    @pl.when(held == 0)
    def _():
        y_ref[...] = jnp.zeros_like(y_ref)


def _experts(blke, nused, blkv, xs, w1, w3, w2):
    E, D, Fe = w1.shape
    total = xs.shape[0] // SUBROWS
    rb = MOE_ROWS
    nblk = total // rb
    wspec = lambda shape: pl.BlockSpec(shape, lambda i, be, nu, bv: (be[i], 0, 0), pipeline_mode=pl.Buffered(1))
    return pl.pallas_call(
        functools.partial(_expert_kernel, rb=rb),
        grid_spec=pltpu.PrefetchScalarGridSpec(
            num_scalar_prefetch=3,
            grid=(nblk,),
            in_specs=[
                pl.BlockSpec((rb * SUBROWS, LANES), lambda i, be, nu, bv: (jnp.minimum(i, nu[0] - 1), 0)),
                wspec((1, D, Fe)), wspec((1, D, Fe)), wspec((1, Fe, D)),
            ],
            out_specs=pl.BlockSpec((rb * SUBROWS, LANES), lambda i, be, nu, bv: (i, 0)),
        ),
        out_shape=jax.ShapeDtypeStruct((total * SUBROWS, LANES), F32),
        compiler_params=_cparams("arbitrary"),
        name="moe_experts",
    )(blke, nused, blkv, xs, w1, w3, w2)


def _combine_kernel(meta_ref, nxt_ref, yb_ref, lcol_ref, wcol_ref, h_ref, g3_ref, o_ref, buf, sems, *, td, ne):
    i = pl.program_id(0)
    slot = i % 2

    def fetch(m_ref, s):
        for e in range(ne):
            _segment_copies(m_ref[0, 0, e], yb_ref, m_ref[0, 0, 2 * ne + e], buf.at[s], m_ref[0, 0, ne + e],
                            sems.at[s], td)

    @pl.when(i == 0)
    def _():
        fetch(meta_ref, 0)

    @pl.when(i + 1 < pl.num_programs(0))
    def _():
        fetch(nxt_ref, 1 - slot)

    for h in range(2):
        pltpu.make_async_copy(yb_ref.at[pl.ds(0, td * SUBROWS), :],
                              buf.at[slot, pl.ds(h * td * SUBROWS, td * SUBROWS), :], sems.at[slot]).wait()
    yb = _interleaved_to_rows(buf.at[slot], 2 * td).astype(BF16)
    lc = lcol_ref[0]
    w = wcol_ref[...]
    c = lax.broadcasted_iota(jnp.int32, (td, 2 * td), 1).astype(F32)
    pick0 = jnp.where(c == lc[:, 0:1], 1.0, 0.0).astype(BF16)
    pick1 = jnp.where(c == lc[:, 1:2], 1.0, 0.0).astype(BF16)
    f = (w[:, 0:1] * jnp.dot(pick0, yb, preferred_element_type=F32)
         + w[:, 1:2] * jnp.dot(pick1, yb, preferred_element_type=F32))
    o_ref[...] = h_ref[...] + _rms(f, g3_ref[...])


def _combine(meta, yb, lcol, wcol, h3, g3):
    N, D = h3.shape
    ntile, td, _ = lcol.shape
    ne = N_EXPERTS
    kern = functools.partial(_combine_kernel, td=td, ne=ne)
    return pl.pallas_call(
        kern,
        grid=(ntile,),
        in_specs=[
            pl.BlockSpec((1, 1, META_ROWS), lambda i: (i, 0, 0), memory_space=pltpu.SMEM),
            pl.BlockSpec((1, 1, META_ROWS), lambda i: (jnp.minimum(i + 1, ntile - 1), 0, 0),
                         memory_space=pltpu.SMEM),
            pl.BlockSpec(memory_space=pl.ANY),
            pl.BlockSpec((1, td, LANES), lambda i: (i, 0, 0)),
            pl.BlockSpec((td, LANES), lambda i: (i, 0)),
            pl.BlockSpec((td, D), lambda i: (i, 0)),
            _const_spec((1, D)),
        ],
        out_specs=pl.BlockSpec((td, D), lambda i: (i, 0)),
        out_shape=jax.ShapeDtypeStruct((N, D), F32),
        scratch_shapes=[pltpu.VMEM((2, 2 * td * SUBROWS, LANES), F32), pltpu.SemaphoreType.DMA((2,))],
        compiler_params=_cparams("arbitrary"),
        name="moe_combine",
    )(meta, meta, yb, lcol, wcol, h3, g3)


def _rope_lane_tables():
    half = ROT_DIM // 2
    inv_freq = 1.0 / (ROPE_THETA ** (np.arange(half, dtype=np.float32) / half))
    invf = np.zeros((1, LANES), np.float32)
    invf[0, :half] = inv_freq
    invf[0, half:ROT_DIM] = inv_freq
    sneg = np.zeros((1, LANES), np.float32)
    sneg[0, :half] = -1.0
    spos = np.zeros((1, LANES), np.float32)
    spos[0, half:ROT_DIM] = 1.0
    return jnp.asarray(invf), jnp.asarray(sneg), jnp.asarray(spos)


def kernel(x, positions, norms, s5_lam_re, s5_lam_im, s5_log_dt, s5_b_re, s5_b_im, s5_c_re, s5_c_im, s5_d, s5_w_glu, kv_norm, w_kv, w_q, w_o, ffn_w1, ffn_w3, ffn_w2, moe_router, moe_bias, moe_w1, moe_w3, moe_w2):
    B, S, D = x.shape
    N = B * S
    HD = N_HEADS * HEAD_DIM
    E = moe_router.shape[-1]
    row = lambda v: v.astype(F32).reshape(1, -1)

    a2_r, a2_i, b2, c2, d2 = _s5_params(s5_lam_re[0], s5_lam_im[0], s5_log_dt[0], s5_b_re[0], s5_b_im[0],
                                        s5_c_re[0], s5_c_im[0])
    h1 = _s5_layer(x, row(norms[0, 0]), row(norms[0, 1]), row(s5_d[0]), a2_r, a2_i, b2, c2, d2,
                   s5_w_glu[0].astype(BF16))
    invf, sneg, spos = _rope_lane_tables()
    posb = jnp.broadcast_to(positions.astype(F32)[:, :, None], (B, S, LANES))
    h2, q, k5, vt5, kmean = _ffn_qkv(
        h1.reshape(N, D), B, row(norms[0, 2]), row(norms[0, 3]),
        ffn_w1[0].astype(BF16), ffn_w3[0].astype(BF16), ffn_w2[0].astype(BF16),
        posb, invf, sneg, spos, row(norms[1, 0]), row(kv_norm),
        w_q[0].astype(BF16), w_kv[:, :HD].astype(BF16), w_kv[:, HD:].T.astype(BF16))
    h3, f_in, e_tiles, wcol = _moba_layer(q, k5, vt5, kmean, h2.reshape(B, S, D), row(norms[1, 1]), w_o[0].astype(BF16),
                                          row(norms[1, 2]), moe_router[0].T.astype(F32),
                                          moe_bias[0].astype(F32).reshape(E, 1))
    h3 = h3.reshape(N, D)

    assert MOBA_BLOCK == DISPATCH_TILE
    total = 2 * N + E * MOE_ROWS
    nblk = total // MOE_ROWS
    nblk_pad = -(-nblk // LANES) * LANES
    lpos, lcol, meta, pad, blke, nused, blkv = _rank(e_tiles, E, nblk_pad)
    meta = meta[:, :, 0].reshape(-1, 1, META_ROWS)
    xs = _dispatch(meta, pad[:, 0], lpos, f_in.reshape(N, D), total)
    yb = _experts(blke[0, :nblk], nused[0, :1], blkv[0, :nblk], xs,
                  moe_w1[0].astype(BF16), moe_w3[0].astype(BF16), moe_w2[0].astype(BF16))
    out = _combine(meta, yb, lcol, wcol.reshape(N, LANES), h3, row(norms[1, 3]))
    return out.reshape(B, S, D)
```

```python
import functools
import math

import numpy as np
import jax
import jax.numpy as jnp
from jax import lax
from jax.experimental import pallas as pl
from jax.experimental.pallas import tpu as pltpu

F32 = jnp.float32
BF16 = jnp.bfloat16

NORM_EPS = 1e-6
NEG_INF = -1e30

SSM_GROUP = 16
N_HEADS = 8
HEAD_DIM = 128
ROT_DIM = HEAD_DIM // 4
ROPE_THETA = 500000.0
MOBA_BLOCK = 256
MOBA_TOPK = 3
N_EXPERTS = 8

Q_SCALE_LOG2 = HEAD_DIM ** -0.5 * math.log2(math.e)

KMEAN_PIECES = 3

LANES = 128
GROUPS_PER_TILE = LANES // SSM_GROUP

VMEM_LIMIT = 56 * 1024 * 1024

S5_CHUNK = 64
S5_ROW_PAD = 8
ROW_TILE = 512
MOE_ROWS = 512
EXPERT_W_CHUNK = 256
DISPATCH_TILE = 256


def _cparams(*sem):
    return pltpu.CompilerParams(dimension_semantics=sem, vmem_limit_bytes=VMEM_LIMIT)


def _rms(x, g):
    return x * lax.rsqrt(jnp.mean(x * x, axis=-1, keepdims=True) + NORM_EPS) * g


def _const_spec(shape):
    n = len(shape)
    return pl.BlockSpec(shape, lambda *_: (0,) * n, pipeline_mode=pl.Buffered(1))


def _zoh(lr, li, dt):
    zr = lr * dt
    zi = li * dt
    mag = jnp.exp(zr)
    return mag * jnp.cos(zr * 0 + zi), mag * jnp.sin(zi)


def _cmul(xr, xi, yr, yi):
    return xr * yr - xi * yi, xr * yi + xi * yr


def _s5_prep_kernel(lrb_ref, lib_ref, lrc_ref, lic_ref, ldt_ref, br_ref, bi_ref, cr_ref, ci_ref,
                    a2r_ref, a2i_ref, bbr_ref, bbi_ref, abr_ref, abi_ref, car_ref, cai_ref, c2r_ref, c2i_ref):
    dt = jnp.exp(ldt_ref[...])
    lr = lrb_ref[...]
    li = lib_ref[...]
    a_r, a_i = _zoh(lr, li, dt)
    den = lr * lr + li * li
    coef_r = ((a_r - 1.0) * lr + a_i * li) / den
    coef_i = (a_i * lr - (a_r - 1.0) * li) / den
    bb_r, bb_i = _cmul(coef_r, coef_i, br_ref[...], bi_ref[...])
    bbr_ref[...] = bb_r
    bbi_ref[...] = bb_i
    abr_ref[...], abi_ref[...] = _cmul(a_r, a_i, bb_r, bb_i)
    a2r_ref[...], a2i_ref[...] = _cmul(a_r, a_i, a_r, a_i)
    ac_r, ac_i = _zoh(lrc_ref[...], lic_ref[...], dt)
    ca_r, ca_i = _cmul(cr_ref[...], ci_ref[...], ac_r, ac_i)
    car_ref[...] = ca_r
    cai_ref[...] = ca_i
    c2r_ref[...], c2i_ref[...] = _cmul(ca_r, ca_i, ac_r, ac_i)


def _s5_fold_kernel(b_ref, ab_ref, c_ref, d0_ref, d1_ref):
    for k in range(b_ref.shape[0]):
        d0_ref[k] = jnp.dot(b_ref[k], c_ref[k], precision=lax.Precision.HIGHEST, preferred_element_type=F32)
        d1_ref[k] = jnp.dot(ab_ref[k], c_ref[k], precision=lax.Precision.HIGHEST, preferred_element_type=F32)


def _s5_params(lam_re, lam_im, log_dt, b_re, b_im, c_re, c_im):
    G, P = lam_re.shape
    Q = b_re.shape[-1]
    gt = GROUPS_PER_TILE
    nt = G // gt
    f = lambda a: a.astype(F32)
    rep = lambda a: jnp.repeat(f(a), Q, axis=1)
    til = lambda a: jnp.tile(f(a), (1, Q))
    out = jax.ShapeDtypeStruct((G, P * Q), F32)
    a2_r, a2_i, bb_r, bb_i, ab_r, ab_i, ca_r, ca_i, c2_r, c2_i = pl.pallas_call(
        _s5_prep_kernel, out_shape=(out,) * 10, name="s5_prep",
    )(rep(lam_re), rep(lam_im), til(lam_re), til(lam_im), f(log_dt).reshape(G, 1),
      f(b_re).reshape(G, P * Q), f(b_im).reshape(G, P * Q), f(c_re).reshape(G, Q * P), f(c_im).reshape(G, Q * P))

    eye = jnp.eye(gt, dtype=F32)
    def bmat(bb):
        return jnp.einsum('kgpi,gh->kgihp', bb.reshape(nt, gt, P, Q), eye).reshape(nt, gt * Q, gt * P)
    def cmat(c):
        return jnp.einsum('kgip,gh->kgphi', c.reshape(nt, gt, Q, P), eye).reshape(nt, gt * P, gt * Q)
    bmat2 = lambda r, i: jnp.concatenate([bmat(r), bmat(i)], axis=-1)
    cmat2 = lambda r, i: jnp.concatenate([cmat(r), -cmat(i)], axis=1)
    b_bd, ab_bd = bmat2(bb_r, bb_i), bmat2(ab_r, ab_i)
    c_bd = cmat2(f(c_re).reshape(G, Q * P), f(c_im).reshape(G, Q * P))
    dshape = jax.ShapeDtypeStruct((nt, LANES, LANES), F32)
    d0, d1 = pl.pallas_call(_s5_fold_kernel, out_shape=(dshape, dshape), name="s5_fold",
                            compiler_params=pltpu.CompilerParams(vmem_limit_bytes=VMEM_LIMIT))(b_bd, ab_bd, c_bd)
    b2 = jnp.concatenate([ab_bd, b_bd], axis=1).astype(BF16)
    c2 = jnp.concatenate([cmat2(ca_r, ca_i), cmat2(c2_r, c2_i)], axis=2).astype(BF16)
    d2 = jnp.concatenate([jnp.concatenate([d0, d1], axis=2),
                          jnp.concatenate([jnp.zeros_like(d0), d0], axis=2)], axis=1).astype(BF16)
    sel = lambda a: a.reshape(G, P, Q)[:, :, 0].reshape(nt, 1, gt * P)
    return sel(a2_r), sel(a2_i), b2, c2, d2


def _s5_layer_kernel(x_ref, g0_ref, g1_ref, d_ref, ar_ref, ai_ref, b2_ref, c2_ref, d2_ref, wglu_ref,
                     o_ref, bm, tm, bu, y_tm, state, *, nb, lc, d, nt, hw):
    c = pl.program_id(0)
    pitch = lc + S5_ROW_PAD
    npair = lc // 2

    @pl.when(c == 0)
    def _():
        state[...] = jnp.zeros_like(state)

    for b in range(nb):
        u = _rms(x_ref[b], g0_ref[...])
        for k in range(nt):
            bm[k, b * pitch:b * pitch + lc, :] = u[:, k * LANES:(k + 1) * LANES]
    for t in range(lc):
        for k in range(nt):
            tm[k, t * nb:(t + 1) * nb, :] = bm[k, pl.ds(t, nb, stride=pitch), :]

    for k in range(nt):
        uk = tm[k]
        u4 = uk.reshape(npair, 2, nb, LANES)
        u2 = jnp.concatenate([u4[:, 0].reshape(npair * nb, LANES), u4[:, 1].reshape(npair * nb, LANES)],
                             axis=1).astype(BF16)
        bu[...] = jnp.dot(u2, b2_ref[k], preferred_element_type=F32)
        a_r = jnp.broadcast_to(ar_ref[k], (nb, hw))
        a_i = jnp.broadcast_to(ai_ref[k], (nb, hw))

        def step(t, carry):
            s_r, s_i = carry
            r0 = pl.multiple_of(t * nb, nb)
            n_r = a_r * s_r - a_i * s_i + bu[pl.ds(r0, nb), 0:hw]
            n_i = a_r * s_i + a_i * s_r + bu[pl.ds(r0, nb), hw:2 * hw]
            bu[pl.ds(r0, nb), 0:hw] = s_r
            bu[pl.ds(r0, nb), hw:2 * hw] = s_i
            return n_r, n_i

        s_r, s_i = lax.fori_loop(0, npair, step, (state[k, :, 0:hw], state[k, :, hw:2 * hw]), unroll=True)
        state[k, :, 0:hw] = s_r
        state[k, :, hw:2 * hw] = s_i
        y2 = (jnp.dot(bu[...].astype(BF16), c2_ref[k], preferred_element_type=F32)
              + jnp.dot(u2, d2_ref[k], preferred_element_type=F32))
        yk = jnp.concatenate([y2[:, :LANES].reshape(npair, 1, nb, LANES),
                              y2[:, LANES:].reshape(npair, 1, nb, LANES)], axis=1).reshape(lc * nb, LANES)
        y_tm[:, k * LANES:(k + 1) * LANES] = yk + d_ref[:, k * LANES:(k + 1) * LANES] * uk

    gl = jax.nn.gelu(y_tm[...]).astype(BF16)
    z = jnp.dot(gl, wglu_ref[...], preferred_element_type=F32)
    mix = z[:, :d] * jax.nn.sigmoid(z[:, d:])
    mixn = _rms(mix, g1_ref[...])
    for k in range(nt):
        tm[k] = mixn[:, k * LANES:(k + 1) * LANES]
    for t in range(lc):
        for k in range(nt):
            bm[k, pl.ds(t, nb, stride=pitch), :] = tm[k, t * nb:(t + 1) * nb, :]
    for b in range(nb):
        for k in range(nt):
            sl = slice(k * LANES, (k + 1) * LANES)
            o_ref[b, :, sl] = x_ref[b, :, sl] + bm[k, b * pitch:b * pitch + lc, :]


def _s5_layer(x, g0, g1, d_skip, a2_r_t, a2_i_t, b2, c2, d2, w_glu):
    B, S, D = x.shape
    lc = S5_CHUNK
    nt = D // LANES
    hw = a2_r_t.shape[-1]
    rows = B * lc
    kern = functools.partial(_s5_layer_kernel, nb=B, lc=lc, d=D, nt=nt, hw=hw)
    return pl.pallas_call(
        kern,
        grid=(S // lc,),
        in_specs=[
            pl.BlockSpec((B, lc, D), lambda c: (0, c, 0)),
            _const_spec((1, D)), _const_spec((1, D)), _const_spec((1, D)),
            _const_spec((nt, 1, hw)), _const_spec((nt, 1, hw)),
            _const_spec((nt, 2 * LANES, 2 * hw)), _const_spec((nt, 2 * hw, 2 * LANES)),
            _const_spec((nt, 2 * LANES, 2 * LANES)),
            _const_spec((D, 2 * D)),
        ],
        out_specs=pl.BlockSpec((B, lc, D), lambda c: (0, c, 0)),
        out_shape=jax.ShapeDtypeStruct((B, S, D), F32),
        scratch_shapes=[
            pltpu.VMEM((nt, B * (lc + S5_ROW_PAD), LANES), F32), pltpu.VMEM((nt, rows, LANES), F32),
            pltpu.VMEM((rows // 2, 2 * hw), F32), pltpu.VMEM((rows, D), F32),
            pltpu.VMEM((nt, B, 2 * hw), F32),
        ],
        compiler_params=_cparams("arbitrary"),
        name="s5_layer",
    )(x, g0, g1, d_skip, a2_r_t, a2_i_t, b2, c2, d2, w_glu)


def _rotary(xh, cs, s_neg, s_pos):
    half = ROT_DIM // 2
    return xh * cs + pltpu.roll(xh, LANES - half, 1) * s_neg + pltpu.roll(xh, half, 1) * s_pos


def _ffn_qkv_kernel(h_ref, g2_ref, g3_ref, w1_ref, w3_ref, w2_ref,
                    pos_ref, invf_ref, sneg_ref, spos_ref, gq_ref, gkv_ref, wq_ref, wk_ref, wvt_ref,
                    o_ref, q_ref, k_ref, vt_ref, km_ref, *, nblk_tile, nblk_seq, tiles_per_seq):
    ti = pl.program_id(0) % tiles_per_seq
    h1 = h_ref[...]
    f_in = _rms(h1, g2_ref[...]).astype(BF16)
    a = jnp.dot(f_in, w1_ref[...], preferred_element_type=F32)
    b = jnp.dot(f_in, w3_ref[...], preferred_element_type=F32)
    act = (jax.nn.silu(a) * b).astype(BF16)
    f = jnp.dot(act, w2_ref[...], preferred_element_type=F32)
    h = h1 + _rms(f, g3_ref[...])
    o_ref[...] = h

    r = lax.rsqrt(jnp.mean(h * h, axis=-1, keepdims=True) + NORM_EPS)
    hq = (h * r * gq_ref[...]).astype(BF16)
    hk = (h * r * gkv_ref[...]).astype(BF16)
    ang = pos_ref[0] * invf_ref[...]
    cs = jnp.cos(ang)
    sn = jnp.sin(ang)
    s_neg = sn * sneg_ref[...]
    s_pos = sn * spos_ref[...]
    q = jnp.dot(hq, wq_ref[...], preferred_element_type=F32)
    k = jnp.dot(hk, wk_ref[...], preferred_element_type=F32)
    vt = lax.dot_general(wvt_ref[...], hk, (((1,), (1,)), ((), ())), preferred_element_type=F32)
    for hh in range(N_HEADS):
        sl = slice(hh * HEAD_DIM, (hh + 1) * HEAD_DIM)
        q_ref[0, hh] = (_rotary(q[:, sl], cs, s_neg, s_pos) * Q_SCALE_LOG2).astype(BF16)
        kr = _rotary(k[:, sl], cs, s_neg, s_pos)
        for j in range(nblk_tile):
            rs = slice(j * MOBA_BLOCK, (j + 1) * MOBA_BLOCK)
            k_ref[0, hh, j] = kr[rs].astype(BF16)
            vt_ref[0, hh, j] = vt[sl, rs].astype(BF16)
            km = jnp.mean(kr[rs], axis=0, keepdims=True)
            for piece in range(KMEAN_PIECES):
                part = km.astype(BF16).astype(F32)
                km_ref[0, hh, pl.ds(piece * nblk_seq + ti * nblk_tile + j, 1), :] = part
                km = km - part


def _ffn_qkv(h1, B, g2, g3, w1, w3, w2, posb, invf, sneg, spos, gq, gkv, wq, wk, wvt):
    N, D = h1.shape
    S = N // B
    Fd = w1.shape[1]
    T = ROW_TILE
    H, Dh, blk = N_HEADS, HEAD_DIM, MOBA_BLOCK
    nb = S // blk
    nbt = T // blk
    tps = S // T
    kern = functools.partial(_ffn_qkv_kernel, nblk_tile=nbt, nblk_seq=nb, tiles_per_seq=tps)
    return pl.pallas_call(
        kern,
        grid=(N // T,),
        in_specs=[
            pl.BlockSpec((T, D), lambda i: (i, 0)),
            _const_spec((1, D)), _const_spec((1, D)),
            _const_spec((D, Fd)), _const_spec((D, Fd)), _const_spec((Fd, D)),
            pl.BlockSpec((1, T, LANES), lambda i: (i // tps, i % tps, 0)),
            _const_spec((1, LANES)), _const_spec((1, LANES)), _const_spec((1, LANES)),
            _const_spec((1, D)), _const_spec((1, D)),
            _const_spec((D, D)), _const_spec((D, D)), _const_spec((D, D)),
        ],
        out_specs=[
            pl.BlockSpec((T, D), lambda i: (i, 0)),
            pl.BlockSpec((1, H, T, Dh), lambda i: (i // tps, 0, i % tps, 0)),
            pl.BlockSpec((1, H, nbt, blk, Dh), lambda i: (i // tps, 0, i % tps, 0, 0)),
            pl.BlockSpec((1, H, nbt, Dh, blk), lambda i: (i // tps, 0, i % tps, 0, 0)),
            pl.BlockSpec((1, H, KMEAN_PIECES * nb, Dh), lambda i: (i // tps, 0, 0, 0)),
        ],
        out_shape=[
            jax.ShapeDtypeStruct((N, D), F32),
            jax.ShapeDtypeStruct((B, H, S, Dh), BF16),
            jax.ShapeDtypeStruct((B, H, nb, blk, Dh), BF16),
            jax.ShapeDtypeStruct((B, H, nb, Dh, blk), BF16),
            jax.ShapeDtypeStruct((B, H, KMEAN_PIECES * nb, Dh), F32),
        ],
        compiler_params=_cparams("arbitrary"),
        name="ffn_qkv",
    )(h1, g2, g3, w1, w3, w2, posb, invf, sneg, spos, gq, gkv, wq, wk, wvt)


def _moba_kernel(q_ref, k_ref, vt_ref, km_ref, h_ref, g1_ref, wo_ref, g2_ref, wrt_ref, br_ref,
                 o_ref, f_ref, e_ref, wcol_ref, sel_scr, st_scr, acc_scr, o_scr, *, nb, k_sel):
    j = pl.program_id(1)
    blk = MOBA_BLOCK
    nt_dims = (((1,), (1,)), ((), ()))
    blk_id = lax.broadcasted_iota(jnp.int32, (nb, blk), 0)
    past = blk_id < j
    causal = (lax.broadcasted_iota(jnp.int32, (blk, blk), 0)
              <= lax.broadcasted_iota(jnp.int32, (blk, blk), 1))

    for hh in range(N_HEADS):
        parts = lax.dot_general(km_ref[0, hh].astype(BF16), q_ref[0, hh], nt_dims,
                                preferred_element_type=F32)
        gate = parts[0:nb]
        for piece in range(1, KMEAN_PIECES):
            gate = gate + parts[piece * nb:(piece + 1) * nb]
        gate = jnp.where(past, gate, NEG_INF)
        rank = jnp.zeros((nb, blk), F32)
        for m in range(nb):
            gm = gate[m:m + 1, :]
            beats = jnp.where(gm > gate, 1.0, jnp.where(gm == gate, jnp.where(m < blk_id, 1.0, 0.0), 0.0))
            rank = rank + beats
        sel_scr[hh] = jnp.where(past, jnp.where(rank < k_sel, 1.0, 0.0), 0.0)

    def qk_pass(n, ms, own):
        out = []
        for hh in range(N_HEADS):
            st = lax.dot_general(k_ref[0, hh, n], q_ref[0, hh], nt_dims, preferred_element_type=F32)
            allowed = causal if own else sel_scr[hh, pl.ds(n, 1), :] > 0.5
            st = jnp.where(allowed, st, NEG_INF)
            st_scr[hh, n] = st
            out.append(jnp.maximum(ms[hh], jnp.max(st, axis=0, keepdims=True)))
        return tuple(out)

    m_init = tuple(jnp.full((1, blk), NEG_INF, F32) for _ in range(N_HEADS))
    m_own = qk_pass(j, m_init, True)
    def past_blocks(step, carry):
        done = 0
        for group in (4, 2, 1):
            trips = (j - done) // group

            def body(i, c, group=group, done=done):
                for g in range(group):
                    c = step(done + i * group + g, c)
                return c

            carry = lax.fori_loop(0, trips, body, carry)
            done = done + trips * group
        return carry

    m_fin = past_blocks(lambda n, ms: qk_pass(n, ms, False), m_own)

    def pv_pass(n, ls, first):
        out = []
        for hh in range(N_HEADS):
            p = jnp.exp2(st_scr[hh, n] - m_fin[hh])
            out.append(ls[hh] + jnp.sum(p, axis=0, keepdims=True))
            pv = jnp.dot(vt_ref[0, hh, n], p.astype(BF16), preferred_element_type=F32)
            acc_scr[hh] = pv if first else acc_scr[hh] + pv
        return tuple(out)

    l_own = pv_pass(j, tuple(jnp.zeros((1, blk), F32) for _ in range(N_HEADS)), True)
    l_fin = past_blocks(lambda n, ls: pv_pass(n, ls, False), l_own)
    for hh in range(N_HEADS):
        o_t = acc_scr[hh] / l_fin[hh]
        o_scr[:, hh * HEAD_DIM:(hh + 1) * HEAD_DIM] = o_t.T.astype(BF16)

    mix = jnp.dot(o_scr[...], wo_ref[...], preferred_element_type=F32)
    h3 = h_ref[0] + _rms(mix, g1_ref[...])
    o_ref[0] = h3
    f = _rms(h3, g2_ref[...])
    f_ref[0] = f.astype(BF16)
    e_ref[0], wcol_ref[0] = _route(f, wrt_ref, br_ref)


def _moba_layer(q, k5, vt5, kmean, h2, g1, wo, g2, wrt, br):
    B, S, D = h2.shape
    H, Dh, blk = N_HEADS, HEAD_DIM, MOBA_BLOCK
    nb = S // blk
    E = wrt.shape[0]
    kern = functools.partial(_moba_kernel, nb=nb, k_sel=min(MOBA_TOPK, nb))
    return pl.pallas_call(
        kern,
        grid=(B, nb),
        in_specs=[
            pl.BlockSpec((1, H, blk, Dh), lambda b, j: (b, 0, j, 0)),
            pl.BlockSpec((1, H, nb, blk, Dh), lambda b, j: (b, 0, 0, 0, 0)),
            pl.BlockSpec((1, H, nb, Dh, blk), lambda b, j: (b, 0, 0, 0, 0)),
            pl.BlockSpec((1, H, KMEAN_PIECES * nb, Dh), lambda b, j: (b, 0, 0, 0)),
            pl.BlockSpec((1, blk, D), lambda b, j: (b, j, 0)),
            _const_spec((1, D)), _const_spec((D, D)),
            _const_spec((1, D)), _const_spec((E, D)), _const_spec((E, 1)),
        ],
        out_specs=[
            pl.BlockSpec((1, blk, D), lambda b, j: (b, j, 0)),
            pl.BlockSpec((1, blk, D), lambda b, j: (b, j, 0)),
            pl.BlockSpec((1, 2, blk), lambda b, j: (b * nb + j, 0, 0)),
            pl.BlockSpec((1, blk, LANES), lambda b, j: (b * nb + j, 0, 0)),
        ],
        out_shape=[
            jax.ShapeDtypeStruct((B, S, D), F32),
            jax.ShapeDtypeStruct((B, S, D), BF16),
            jax.ShapeDtypeStruct((B * nb, 2, blk), jnp.int32),
            jax.ShapeDtypeStruct((B * nb, blk, LANES), F32),
        ],
        scratch_shapes=[
            pltpu.VMEM((H, nb, blk), F32), pltpu.VMEM((H, nb, blk, blk), F32),
            pltpu.VMEM((H, Dh, blk), F32), pltpu.VMEM((blk, D), BF16),
        ],
        compiler_params=_cparams("parallel", "arbitrary"),
        name="moba_attn",
    )(q, k5, vt5, kmean, h2, g1, wo, g2, wrt, br)


def _route(f, wrt_ref, br_ref):
    nt_dims = (((1,), (1,)), ((), ()))
    w = wrt_ref[...]
    ne = w.shape[0]
    w_hi = w.astype(BF16).astype(F32)
    f_hi = f.astype(BF16)
    f_lo = (f - f_hi.astype(F32)).astype(BF16)
    w2 = jnp.concatenate([w_hi, w - w_hi], axis=0).astype(BF16)
    a = lax.dot_general(w2, f_hi, nt_dims, preferred_element_type=F32)
    b = lax.dot_general(w2[:ne], f_lo, nt_dims, preferred_element_type=F32)
    logit = a[:ne] + a[ne:] + b + br_ref[...]
    t = logit.shape[1]
    eidx = lax.broadcasted_iota(jnp.int32, (ne, t), 0)
    m1 = jnp.max(logit, axis=0, keepdims=True)
    e1 = jnp.min(jnp.where(logit == m1, eidx, ne), axis=0, keepdims=True)
    rest = jnp.where(eidx == e1, -jnp.inf, logit)
    m2 = jnp.max(rest, axis=0, keepdims=True)
    e2 = jnp.min(jnp.where(rest == m2, eidx, ne), axis=0, keepdims=True)
    ex = jnp.exp(m2 - m1)
    den = 1.0 + ex
    wpad = jnp.concatenate([1.0 / den, ex / den, jnp.zeros((LANES - 2, t), F32)], axis=0)
    return jnp.concatenate([e1, e2], axis=0), wpad.T


META_ROWS = 32


def _rank_kernel(e_ref, lpos_ref, lcol_ref, meta_ref, pad_ref, blke_ref, nused_ref, blkv_ref,
                 *, ne, rows_log2, nblk_pad):
    ntile, _, t = e_ref.shape
    eidx = lax.broadcasted_iota(jnp.int32, (ne, t), 0)

    def onehots(i):
        e = e_ref[i]
        oh0 = jnp.where(eidx == e[0:1, :], 1.0, 0.0)
        oh1 = jnp.where(eidx == e[1:2, :], 1.0, 0.0)
        return oh0, oh1

    def excl_cumsum(col):
        run = jnp.zeros((1, 1), jnp.int32)
        parts = []
        for e in range(ne):
            parts.append(run)
            run = run + col[e:e + 1, :]
        return jnp.concatenate(parts, axis=0), run

    def count(i, cnt):
        oh0, oh1 = onehots(i)
        return cnt + jnp.sum(oh0 + oh1, axis=1, keepdims=True).astype(jnp.int32)

    cnt = lax.fori_loop(0, ntile, count, jnp.zeros((ne, 1), jnp.int32), unroll=8)
    pcnt = ((cnt + ((1 << rows_log2) - 1)) >> rows_log2) << rows_log2
    pstart, ptotal = excl_cumsum(pcnt)
    pend = pstart + pcnt

    upper = jnp.where(lax.broadcasted_iota(jnp.int32, (t, t), 0) < lax.broadcasted_iota(jnp.int32, (t, t), 1),
                      1.0, 0.0).astype(BF16)
    lane_b = lambda col: jnp.broadcast_to(col, (ne, LANES))

    def place(i, base):
        oh0, oh1 = onehots(i)
        both = oh0 + oh1
        before = jnp.dot(both.astype(BF16), upper, preferred_element_type=F32).astype(jnp.int32)
        cnt_i = jnp.sum(both, axis=1, keepdims=True).astype(jnp.int32)
        loff, _ = excl_cumsum(cnt_i)
        slot = loff + before
        l0 = jnp.sum(jnp.where(oh0 > 0.5, slot, 0), axis=0, keepdims=True)
        l1 = jnp.sum(jnp.where(oh1 > 0.5, slot, 0), axis=0, keepdims=True)
        lpos_ref[i] = jnp.concatenate([l0, l1], axis=0)
        lpad = jnp.concatenate([l0.astype(F32), l1.astype(F32), jnp.zeros((LANES - 2, t), F32)], axis=0)
        lcol_ref[i] = lpad.T
        meta_ref[i] = jnp.concatenate(
            [lane_b(cnt_i), lane_b(loff), lane_b(pstart + base),
             jnp.zeros((META_ROWS - 3 * ne, LANES), jnp.int32)], axis=0)
        return base + cnt_i

    lax.fori_loop(0, ntile, place, jnp.zeros((ne, 1), jnp.int32), unroll=4)

    pad_ref[...] = jnp.concatenate([lane_b(pstart + cnt), lane_b(pcnt - cnt),
                                    jnp.broadcast_to(ptotal, (ne, LANES))], axis=0)
    blk_row = lax.broadcasted_iota(jnp.int32, (ne, nblk_pad), 1) << rows_log2
    blke = jnp.sum(jnp.where(pend <= blk_row, 1, 0), axis=0, keepdims=True)
    blke_ref[...] = jnp.minimum(blke, ne - 1)
    nused_ref[...] = jnp.broadcast_to(ptotal >> rows_log2, (1, LANES))
    held = jnp.minimum(pstart + cnt - blk_row, 1 << rows_log2)
    blkv_ref[...] = jnp.sum(jnp.where(blk_row >= pstart, jnp.where(blk_row < pend, held, 0), 0),
                            axis=0, keepdims=True)


def _rank(e3, ne, nblk_pad):
    ntile, _, t = e3.shape
    kern = functools.partial(_rank_kernel, ne=ne, rows_log2=int(math.log2(MOE_ROWS)), nblk_pad=nblk_pad)
    return pl.pallas_call(
        kern,
        out_shape=[
            jax.ShapeDtypeStruct((ntile, 2, t), jnp.int32),
            jax.ShapeDtypeStruct((ntile, t, LANES), F32),
            jax.ShapeDtypeStruct((ntile, META_ROWS, LANES), jnp.int32),
            jax.ShapeDtypeStruct((3 * ne, LANES), jnp.int32),
            jax.ShapeDtypeStruct((1, nblk_pad), jnp.int32),
            jax.ShapeDtypeStruct((1, LANES), jnp.int32),
            jax.ShapeDtypeStruct((1, nblk_pad), jnp.int32),
        ],
        compiler_params=pltpu.CompilerParams(vmem_limit_bytes=VMEM_LIMIT),
        name="moe_rank",
    )(e3)


SUBROWS = 8


def _rows_to_interleaved(ref, x):
    n = x.shape[0]
    for s in range(SUBROWS):
        ref[pl.ds(s, n, stride=SUBROWS), :] = x[:, s * LANES:(s + 1) * LANES]


def _interleaved_to_rows(ref, n):
    return jnp.concatenate([ref[pl.ds(s, n, stride=SUBROWS), :] for s in range(SUBROWS)], axis=1)


def _segment_copies(count, src_ref, src_row, dst_ref, dst_row, sem, max_piece, wait=False, advance_src=True):
    p = max_piece
    while p >= 1:
        hit = (count & p) != 0

        @pl.when(hit)
        def _(p=p, src_row=src_row, dst_row=dst_row):
            cp = pltpu.make_async_copy(
                src_ref.at[pl.ds(pl.multiple_of(src_row * SUBROWS, SUBROWS), p * SUBROWS), :],
                dst_ref.at[pl.ds(pl.multiple_of(dst_row * SUBROWS, SUBROWS), p * SUBROWS), :], sem)
            if wait:
                cp.wait()
            else:
                cp.start()

        step = count & p
        if advance_src:
            src_row = src_row + step
        dst_row = dst_row + step
        p //= 2


def _dispatch_kernel(meta_ref, pad_ref, lpos_ref, f_ref, xs_ref, srt, zbuf, sems, zsem, *, td, ne):
    i = pl.program_id(0)

    @pl.when(i == 0)
    def _():
        zbuf[...] = jnp.zeros_like(zbuf)
        zrows = zbuf.shape[0] // SUBROWS
        total = xs_ref.shape[0] // SUBROWS
        for wait in (False, True):
            for e in range(ne):
                _segment_copies(pad_ref[ne + e], zbuf, 0, xs_ref, pad_ref[e], zsem,
                                zrows, wait=wait, advance_src=False)
            for r in range(0, ne * MOE_ROWS, zrows):
                row = pad_ref[2 * ne] + r

                @pl.when(row < total)
                def _(row=row):
                    cp = pltpu.make_async_copy(
                        zbuf, xs_ref.at[pl.ds(pl.multiple_of(row * SUBROWS, SUBROWS), zrows * SUBROWS), :], zsem)
                    if wait:
                        cp.wait()
                    else:
                        cp.start()

    slot = i % 2
    lp = lpos_ref[0]
    r = lax.broadcasted_iota(jnp.int32, (2 * td, td), 0)
    perm = (jnp.where(r == lp[0:1, :], 1.0, 0.0) + jnp.where(r == lp[1:2, :], 1.0, 0.0)).astype(BF16)
    _rows_to_interleaved(srt.at[slot], jnp.dot(perm, f_ref[...], preferred_element_type=F32))
    for e in range(ne):
        _segment_copies(meta_ref[0, 0, e], srt.at[slot], meta_ref[0, 0, ne + e], xs_ref,
                        meta_ref[0, 0, 2 * ne + e], sems.at[slot], td)

    def drain(s):
        for h in range(2):
            pltpu.make_async_copy(srt.at[s, pl.ds(h * td * SUBROWS, td * SUBROWS), :],
                                  xs_ref.at[pl.ds(0, td * SUBROWS), :], sems.at[s]).wait()

    @pl.when(i > 0)
    def _():
        drain(1 - slot)

    @pl.when(i + 1 == pl.num_programs(0))
    def _():
        drain(slot)


def _dispatch(meta, pad, lpos, f, total):
    N, D = f.shape
    ntile, _, td = lpos.shape
    ne = N_EXPERTS
    assert D == SUBROWS * LANES
    kern = functools.partial(_dispatch_kernel, td=td, ne=ne)
    return pl.pallas_call(
        kern,
        grid=(ntile,),
        in_specs=[
            pl.BlockSpec((1, 1, META_ROWS), lambda i: (i, 0, 0), memory_space=pltpu.SMEM),
            pl.BlockSpec(memory_space=pltpu.SMEM),
            pl.BlockSpec((1, 2, td), lambda i: (i, 0, 0)),
            pl.BlockSpec((td, D), lambda i: (i, 0)),
        ],
        out_specs=pl.BlockSpec(memory_space=pl.ANY),
        out_shape=jax.ShapeDtypeStruct((total * SUBROWS, LANES), F32),
        scratch_shapes=[pltpu.VMEM((2, 2 * td * SUBROWS, LANES), F32),
                        pltpu.VMEM((MOE_ROWS // 2 * SUBROWS, LANES), F32),
                        pltpu.SemaphoreType.DMA((2,)), pltpu.SemaphoreType.DMA(())],
        compiler_params=_cparams("arbitrary"),
        name="moe_dispatch",
    )(meta, pad, lpos, f)


def _load_expert_weights(e, w1_hbm, w3_hbm, w2_hbm, w1b, w3b, w2b, st_in, st_out, sems):
    d, fe = w1b.shape
    ch = EXPERT_W_CHUNK
    items = ([(w1_hbm, w1b, c, True) for c in range(fe // ch)] + [(w3_hbm, w3b, c, True) for c in range(fe // ch)]
             + [(w2_hbm, w2b, c, False) for c in range(fe // ch)])

    def copy(k):
        src, _, c, by_col = items[k]
        slot = k % 2
        if by_col:
            return pltpu.make_async_copy(src.at[e, :, pl.ds(c * ch, ch)], st_in.at[slot], sems.at[slot])
        return pltpu.make_async_copy(src.at[e, pl.ds(c * ch, ch), :], st_out.at[slot], sems.at[slot])

    copy(0).start()
    for k in range(len(items)):
        if k + 1 < len(items):
            copy(k + 1).start()
        copy(k).wait()
        _, dst, c, by_col = items[k]
        if by_col:
            dst[:, c * ch:(c + 1) * ch] = st_in[k % 2].astype(BF16)
        else:
            dst[c * ch:(c + 1) * ch, :] = st_out[k % 2].astype(BF16)


def _expert_kernel(blke_ref, nused_ref, blkv_ref, xs_ref, w1_hbm, w3_hbm, w2_hbm, y_ref,
                   w1b, w3b, w2b, st_in, st_out, sems, *, rb):
    i = pl.program_id(0)
    held = blkv_ref[i]
    half = rb // 2
    e = blke_ref[i]
    e_prev = blke_ref[jnp.maximum(i - 1, 0)]

    @pl.when(jnp.logical_and(held > 0, jnp.logical_or(i == 0, e != e_prev)))
    def _():
        _load_expert_weights(e, w1_hbm, w3_hbm, w2_hbm, w1b, w3b, w2b, st_in, st_out, sems)

    def swiglu(rows):
        x = _interleaved_to_rows(xs_ref, rows).astype(BF16)
        fe = w1b.shape[1]
        y = None
        for fs in range(0, fe, fe // 2):
            sl = slice(fs, fs + fe // 2)
            a = jnp.dot(x, w1b[:, sl], preferred_element_type=F32)
            b = jnp.dot(x, w3b[:, sl], preferred_element_type=F32)
            act = (jax.nn.silu(a) * b).astype(BF16)
            part = jnp.dot(act, w2b[sl, :], preferred_element_type=F32)
            y = part if y is None else y + part
        _rows_to_interleaved(y_ref, y)

    @pl.when(held > half)
    def _():
        swiglu(rb)

    @pl.when(jnp.logical_and(held > 0, held <= half))
    def _():
        swiglu(half)
        y_ref[pl.ds(half * SUBROWS, half * SUBROWS), :] = jnp.zeros((half * SUBROWS, LANES), F32)

    @pl.when(held == 0)
    def _():
        y_ref[...] = jnp.zeros_like(y_ref)


def _experts(blke, nused, blkv, xs, w1, w3, w2):
    E, D, Fe = w1.shape
    total = xs.shape[0] // SUBROWS
    rb = MOE_ROWS
    nblk = total // rb
    ch = EXPERT_W_CHUNK
    assert Fe % (2 * LANES) == 0 and Fe % ch == 0
    hbm = pl.BlockSpec(memory_space=pl.ANY)
    return pl.pallas_call(
        functools.partial(_expert_kernel, rb=rb),
        grid_spec=pltpu.PrefetchScalarGridSpec(
            num_scalar_prefetch=3,
            grid=(nblk,),
            in_specs=[
                pl.BlockSpec((rb * SUBROWS, LANES), lambda i, be, nu, bv: (jnp.minimum(i, nu[0] - 1), 0)),
                hbm, hbm, hbm,
            ],
            out_specs=pl.BlockSpec((rb * SUBROWS, LANES), lambda i, be, nu, bv: (i, 0)),
            scratch_shapes=[
                pltpu.VMEM((D, Fe), BF16), pltpu.VMEM((D, Fe), BF16), pltpu.VMEM((Fe, D), BF16),
                pltpu.VMEM((2, D, ch), F32), pltpu.VMEM((2, ch, D), F32), pltpu.SemaphoreType.DMA((2,)),
            ],
        ),
        out_shape=jax.ShapeDtypeStruct((total * SUBROWS, LANES), F32),
        compiler_params=_cparams("arbitrary"),
        name="moe_experts",
    )(blke, nused, blkv, xs, w1, w3, w2)


def _combine_kernel(meta_ref, nxt_ref, yb_ref, lcol_ref, wcol_ref, h_ref, g3_ref, o_ref, buf, sems, *, td, ne):
    i = pl.program_id(0)
    slot = i % 2

    def fetch(m_ref, s):
        for e in range(ne):
            _segment_copies(m_ref[0, 0, e], yb_ref, m_ref[0, 0, 2 * ne + e], buf.at[s], m_ref[0, 0, ne + e],
                            sems.at[s], td)

    @pl.when(i == 0)
    def _():
        fetch(meta_ref, 0)

    @pl.when(i + 1 < pl.num_programs(0))
    def _():
        fetch(nxt_ref, 1 - slot)

    for h in range(2):
        pltpu.make_async_copy(yb_ref.at[pl.ds(0, td * SUBROWS), :],
                              buf.at[slot, pl.ds(h * td * SUBROWS, td * SUBROWS), :], sems.at[slot]).wait()
    yb = _interleaved_to_rows(buf.at[slot], 2 * td).astype(BF16)
    lc = lcol_ref[0]
    w = wcol_ref[...]
    c = lax.broadcasted_iota(jnp.int32, (td, 2 * td), 1).astype(F32)
    pick0 = jnp.where(c == lc[:, 0:1], 1.0, 0.0).astype(BF16)
    pick1 = jnp.where(c == lc[:, 1:2], 1.0, 0.0).astype(BF16)
    f = (w[:, 0:1] * jnp.dot(pick0, yb, preferred_element_type=F32)
         + w[:, 1:2] * jnp.dot(pick1, yb, preferred_element_type=F32))
    o_ref[...] = h_ref[...] + _rms(f, g3_ref[...])


def _combine(meta, yb, lcol, wcol, h3, g3):
    N, D = h3.shape
    ntile, td, _ = lcol.shape
    ne = N_EXPERTS
    kern = functools.partial(_combine_kernel, td=td, ne=ne)
    return pl.pallas_call(
        kern,
        grid=(ntile,),
        in_specs=[
            pl.BlockSpec((1, 1, META_ROWS), lambda i: (i, 0, 0), memory_space=pltpu.SMEM),
            pl.BlockSpec((1, 1, META_ROWS), lambda i: (jnp.minimum(i + 1, ntile - 1), 0, 0),
                         memory_space=pltpu.SMEM),
            pl.BlockSpec(memory_space=pl.ANY),
            pl.BlockSpec((1, td, LANES), lambda i: (i, 0, 0)),
            pl.BlockSpec((td, LANES), lambda i: (i, 0)),
            pl.BlockSpec((td, D), lambda i: (i, 0)),
            _const_spec((1, D)),
        ],
        out_specs=pl.BlockSpec((td, D), lambda i: (i, 0)),
        out_shape=jax.ShapeDtypeStruct((N, D), F32),
        scratch_shapes=[pltpu.VMEM((2, 2 * td * SUBROWS, LANES), F32), pltpu.SemaphoreType.DMA((2,))],
        compiler_params=_cparams("arbitrary"),
        name="moe_combine",
    )(meta, meta, yb, lcol, wcol, h3, g3)


def _rope_lane_tables():
    half = ROT_DIM // 2
    inv_freq = 1.0 / (ROPE_THETA ** (np.arange(half, dtype=np.float32) / half))
    invf = np.zeros((1, LANES), np.float32)
    invf[0, :half] = inv_freq
    invf[0, half:ROT_DIM] = inv_freq
    sneg = np.zeros((1, LANES), np.float32)
    sneg[0, :half] = -1.0
    spos = np.zeros((1, LANES), np.float32)
    spos[0, half:ROT_DIM] = 1.0
    return jnp.asarray(invf), jnp.asarray(sneg), jnp.asarray(spos)


def kernel(x, positions, norms, s5_lam_re, s5_lam_im, s5_log_dt, s5_b_re, s5_b_im, s5_c_re, s5_c_im, s5_d, s5_w_glu, kv_norm, w_kv, w_q, w_o, ffn_w1, ffn_w3, ffn_w2, moe_router, moe_bias, moe_w1, moe_w3, moe_w2):
    B, S, D = x.shape
    N = B * S
    HD = N_HEADS * HEAD_DIM
    E = moe_router.shape[-1]
    row = lambda v: v.astype(F32).reshape(1, -1)

    a2_r, a2_i, b2, c2, d2 = _s5_params(s5_lam_re[0], s5_lam_im[0], s5_log_dt[0], s5_b_re[0], s5_b_im[0],
                                        s5_c_re[0], s5_c_im[0])
    h1 = _s5_layer(x, row(norms[0, 0]), row(norms[0, 1]), row(s5_d[0]), a2_r, a2_i, b2, c2, d2,
                   s5_w_glu[0].astype(BF16))
    invf, sneg, spos = _rope_lane_tables()
    posb = jnp.broadcast_to(positions.astype(F32)[:, :, None], (B, S, LANES))
    h2, q, k5, vt5, kmean = _ffn_qkv(
        h1.reshape(N, D), B, row(norms[0, 2]), row(norms[0, 3]),
        ffn_w1[0].astype(BF16), ffn_w3[0].astype(BF16), ffn_w2[0].astype(BF16),
        posb, invf, sneg, spos, row(norms[1, 0]), row(kv_norm),
        w_q[0].astype(BF16), w_kv[:, :HD].astype(BF16), w_kv[:, HD:].T.astype(BF16))
    h3, f_in, e_tiles, wcol = _moba_layer(q, k5, vt5, kmean, h2.reshape(B, S, D), row(norms[1, 1]), w_o[0].astype(BF16),
                                          row(norms[1, 2]), moe_router[0].T.astype(F32),
                                          moe_bias[0].astype(F32).reshape(E, 1))
    h3 = h3.reshape(N, D)

    assert MOBA_BLOCK == DISPATCH_TILE
    total = 2 * N + E * MOE_ROWS
    nblk = total // MOE_ROWS
    nblk_pad = -(-nblk // LANES) * LANES
    lpos, lcol, meta, pad, blke, nused, blkv = _rank(e_tiles, E, nblk_pad)
    meta = meta[:, :, 0].reshape(-1, 1, META_ROWS)
    xs = _dispatch(meta, pad[:, 0], lpos, f_in.reshape(N, D), total)
    yb = _experts(blke[0, :nblk], nused[0, :1], blkv[0, :nblk], xs,
                  moe_w1[0].astype(F32), moe_w3[0].astype(F32), moe_w2[0].astype(F32))
    out = _combine(meta, yb, lcol, wcol.reshape(N, LANES), h3, row(norms[1, 3]))
    return out.reshape(B, S, D)
```

```python
import functools
import math

import numpy as np
import jax
import jax.numpy as jnp
from jax import lax
from jax.experimental import pallas as pl
from jax.experimental.pallas import tpu as pltpu

F32 = jnp.float32
BF16 = jnp.bfloat16

NORM_EPS = 1e-6
NEG_INF = -1e30

SSM_GROUP = 16
N_HEADS = 8
HEAD_DIM = 128
ROT_DIM = HEAD_DIM // 4
ROPE_THETA = 500000.0
MOBA_BLOCK = 256
MOBA_TOPK = 3
N_EXPERTS = 8

Q_SCALE_LOG2 = HEAD_DIM ** -0.5 * math.log2(math.e)

KMEAN_PIECES = 3

LANES = 128
GROUPS_PER_TILE = LANES // SSM_GROUP

VMEM_LIMIT = 56 * 1024 * 1024

S5_CHUNK = 64
S5_ROW_PAD = 8
ROW_TILE = 512
MOE_ROWS = 512
EXPERT_W_CHUNK = 256
DISPATCH_TILE = 256


def _cparams(*sem):
    return pltpu.CompilerParams(dimension_semantics=sem, vmem_limit_bytes=VMEM_LIMIT)


def _rms(x, g):
    return x * lax.rsqrt(jnp.mean(x * x, axis=-1, keepdims=True) + NORM_EPS) * g


def _const_spec(shape):
    n = len(shape)
    return pl.BlockSpec(shape, lambda *_: (0,) * n, pipeline_mode=pl.Buffered(1))


def _zoh(lr, li, dt):
    zr = lr * dt
    zi = li * dt
    mag = jnp.exp(zr)
    return mag * jnp.cos(zr * 0 + zi), mag * jnp.sin(zi)


def _cmul(xr, xi, yr, yi):
    return xr * yr - xi * yi, xr * yi + xi * yr


def _s5_prep_kernel(lrb_ref, lib_ref, lrc_ref, lic_ref, ldt_ref, br_ref, bi_ref, cr_ref, ci_ref,
                    a2r_ref, a2i_ref, bbr_ref, bbi_ref, abr_ref, abi_ref, car_ref, cai_ref, c2r_ref, c2i_ref):
    dt = jnp.exp(ldt_ref[...])
    lr = lrb_ref[...]
    li = lib_ref[...]
    a_r, a_i = _zoh(lr, li, dt)
    den = lr * lr + li * li
    coef_r = ((a_r - 1.0) * lr + a_i * li) / den
    coef_i = (a_i * lr - (a_r - 1.0) * li) / den
    bb_r, bb_i = _cmul(coef_r, coef_i, br_ref[...], bi_ref[...])
    bbr_ref[...] = bb_r
    bbi_ref[...] = bb_i
    abr_ref[...], abi_ref[...] = _cmul(a_r, a_i, bb_r, bb_i)
    a2r_ref[...], a2i_ref[...] = _cmul(a_r, a_i, a_r, a_i)
    ac_r, ac_i = _zoh(lrc_ref[...], lic_ref[...], dt)
    ca_r, ca_i = _cmul(cr_ref[...], ci_ref[...], ac_r, ac_i)
    car_ref[...] = ca_r
    cai_ref[...] = ca_i
    c2r_ref[...], c2i_ref[...] = _cmul(ca_r, ca_i, ac_r, ac_i)


def _s5_fold_kernel(b_ref, ab_ref, c_ref, d0_ref, d1_ref):
    for k in range(b_ref.shape[0]):
        d0_ref[k] = jnp.dot(b_ref[k], c_ref[k], precision=lax.Precision.HIGHEST, preferred_element_type=F32)
        d1_ref[k] = jnp.dot(ab_ref[k], c_ref[k], precision=lax.Precision.HIGHEST, preferred_element_type=F32)


def _s5_params(lam_re, lam_im, log_dt, b_re, b_im, c_re, c_im):
    G, P = lam_re.shape
    Q = b_re.shape[-1]
    gt = GROUPS_PER_TILE
    nt = G // gt
    f = lambda a: a.astype(F32)
    rep = lambda a: jnp.repeat(f(a), Q, axis=1)
    til = lambda a: jnp.tile(f(a), (1, Q))
    out = jax.ShapeDtypeStruct((G, P * Q), F32)
    a2_r, a2_i, bb_r, bb_i, ab_r, ab_i, ca_r, ca_i, c2_r, c2_i = pl.pallas_call(
        _s5_prep_kernel, out_shape=(out,) * 10, name="s5_prep",
    )(rep(lam_re), rep(lam_im), til(lam_re), til(lam_im), f(log_dt).reshape(G, 1),
      f(b_re).reshape(G, P * Q), f(b_im).reshape(G, P * Q), f(c_re).reshape(G, Q * P), f(c_im).reshape(G, Q * P))

    eye = jnp.eye(gt, dtype=F32)
    def bmat(bb):
        return jnp.einsum('kgpi,gh->kgihp', bb.reshape(nt, gt, P, Q), eye).reshape(nt, gt * Q, gt * P)
    def cmat(c):
        return jnp.einsum('kgip,gh->kgphi', c.reshape(nt, gt, Q, P), eye).reshape(nt, gt * P, gt * Q)
    bmat2 = lambda r, i: jnp.concatenate([bmat(r), bmat(i)], axis=-1)
    cmat2 = lambda r, i: jnp.concatenate([cmat(r), -cmat(i)], axis=1)
    b_bd, ab_bd = bmat2(bb_r, bb_i), bmat2(ab_r, ab_i)
    c_bd = cmat2(f(c_re).reshape(G, Q * P), f(c_im).reshape(G, Q * P))
    dshape = jax.ShapeDtypeStruct((nt, LANES, LANES), F32)
    d0, d1 = pl.pallas_call(_s5_fold_kernel, out_shape=(dshape, dshape), name="s5_fold",
                            compiler_params=pltpu.CompilerParams(vmem_limit_bytes=VMEM_LIMIT))(b_bd, ab_bd, c_bd)
    b2 = jnp.concatenate([ab_bd, b_bd], axis=1).astype(BF16)
    c2 = jnp.concatenate([cmat2(ca_r, ca_i), cmat2(c2_r, c2_i)], axis=2).astype(BF16)
    d2 = jnp.concatenate([jnp.concatenate([d0, d1], axis=2),
                          jnp.concatenate([jnp.zeros_like(d0), d0], axis=2)], axis=1).astype(BF16)
    sel = lambda a: a.reshape(G, P, Q)[:, :, 0].reshape(nt, 1, gt * P)
    return sel(a2_r), sel(a2_i), b2, c2, d2


def _s5_layer_kernel(x_ref, g0_ref, g1_ref, d_ref, ar_ref, ai_ref, b2_ref, c2_ref, d2_ref, wglu_ref,
                     o_ref, bm, tm, bu, y_tm, state, *, nb, lc, d, nt, hw):
    c = pl.program_id(0)
    pitch = lc + S5_ROW_PAD
    npair = lc // 2

    @pl.when(c == 0)
    def _():
        state[...] = jnp.zeros_like(state)

    for b in range(nb):
        u = _rms(x_ref[b], g0_ref[...])
        for k in range(nt):
            bm[k, b * pitch:b * pitch + lc, :] = u[:, k * LANES:(k + 1) * LANES]
    for t in range(lc):
        for k in range(nt):
            tm[k, t * nb:(t + 1) * nb, :] = bm[k, pl.ds(t, nb, stride=pitch), :]

    for k in range(nt):
        uk = tm[k]
        u4 = uk.reshape(npair, 2, nb, LANES)
        u2 = jnp.concatenate([u4[:, 0].reshape(npair * nb, LANES), u4[:, 1].reshape(npair * nb, LANES)],
                             axis=1).astype(BF16)
        bu[...] = jnp.dot(u2, b2_ref[k], preferred_element_type=F32)
        a_r = jnp.broadcast_to(ar_ref[k], (nb, hw))
        a_i = jnp.broadcast_to(ai_ref[k], (nb, hw))

        def step(t, carry):
            s_r, s_i = carry
            r0 = pl.multiple_of(t * nb, nb)
            n_r = a_r * s_r - a_i * s_i + bu[pl.ds(r0, nb), 0:hw]
            n_i = a_r * s_i + a_i * s_r + bu[pl.ds(r0, nb), hw:2 * hw]
            bu[pl.ds(r0, nb), 0:hw] = s_r
            bu[pl.ds(r0, nb), hw:2 * hw] = s_i
            return n_r, n_i

        s_r, s_i = lax.fori_loop(0, npair, step, (state[k, :, 0:hw], state[k, :, hw:2 * hw]), unroll=True)
        state[k, :, 0:hw] = s_r
        state[k, :, hw:2 * hw] = s_i
        y2 = (jnp.dot(bu[...].astype(BF16), c2_ref[k], preferred_element_type=F32)
              + jnp.dot(u2, d2_ref[k], preferred_element_type=F32))
        yk = jnp.concatenate([y2[:, :LANES].reshape(npair, 1, nb, LANES),
                              y2[:, LANES:].reshape(npair, 1, nb, LANES)], axis=1).reshape(lc * nb, LANES)
        y_tm[:, k * LANES:(k + 1) * LANES] = yk + d_ref[:, k * LANES:(k + 1) * LANES] * uk

    gl = jax.nn.gelu(y_tm[...]).astype(BF16)
    z = jnp.dot(gl, wglu_ref[...], preferred_element_type=F32)
    mix = z[:, :d] * jax.nn.sigmoid(z[:, d:])
    mixn = _rms(mix, g1_ref[...])
    for k in range(nt):
        tm[k] = mixn[:, k * LANES:(k + 1) * LANES]
    for t in range(lc):
        for k in range(nt):
            bm[k, pl.ds(t, nb, stride=pitch), :] = tm[k, t * nb:(t + 1) * nb, :]
    for b in range(nb):
        for k in range(nt):
            sl = slice(k * LANES, (k + 1) * LANES)
            o_ref[b, :, sl] = x_ref[b, :, sl] + bm[k, b * pitch:b * pitch + lc, :]


def _s5_layer(x, g0, g1, d_skip, a2_r_t, a2_i_t, b2, c2, d2, w_glu):
    B, S, D = x.shape
    lc = S5_CHUNK
    nt = D // LANES
    hw = a2_r_t.shape[-1]
    rows = B * lc
    kern = functools.partial(_s5_layer_kernel, nb=B, lc=lc, d=D, nt=nt, hw=hw)
    return pl.pallas_call(
        kern,
        grid=(S // lc,),
        in_specs=[
            pl.BlockSpec((B, lc, D), lambda c: (0, c, 0)),
            _const_spec((1, D)), _const_spec((1, D)), _const_spec((1, D)),
            _const_spec((nt, 1, hw)), _const_spec((nt, 1, hw)),
            _const_spec((nt, 2 * LANES, 2 * hw)), _const_spec((nt, 2 * hw, 2 * LANES)),
            _const_spec((nt, 2 * LANES, 2 * LANES)),
            _const_spec((D, 2 * D)),
        ],
        out_specs=pl.BlockSpec((B, lc, D), lambda c: (0, c, 0)),
        out_shape=jax.ShapeDtypeStruct((B, S, D), F32),
        scratch_shapes=[
            pltpu.VMEM((nt, B * (lc + S5_ROW_PAD), LANES), F32), pltpu.VMEM((nt, rows, LANES), F32),
            pltpu.VMEM((rows // 2, 2 * hw), F32), pltpu.VMEM((rows, D), F32),
            pltpu.VMEM((nt, B, 2 * hw), F32),
        ],
        compiler_params=_cparams("arbitrary"),
        name="s5_layer",
    )(x, g0, g1, d_skip, a2_r_t, a2_i_t, b2, c2, d2, w_glu)


def _rotary(xh, cs, s_neg, s_pos):
    half = ROT_DIM // 2
    return xh * cs + pltpu.roll(xh, LANES - half, 1) * s_neg + pltpu.roll(xh, half, 1) * s_pos


def _ffn_qkv_kernel(h_ref, g2_ref, g3_ref, w1_ref, w3_ref, w2_ref,
                    pos_ref, invf_ref, sneg_ref, spos_ref, gq_ref, gkv_ref, wq_ref, wk_ref, wvt_ref,
                    o_ref, q_ref, k_ref, vt_ref, km_ref, *, nblk_tile, nblk_seq, tiles_per_seq):
    ti = pl.program_id(0) % tiles_per_seq
    h1 = h_ref[...]
    f_in = _rms(h1, g2_ref[...]).astype(BF16)
    a = jnp.dot(f_in, w1_ref[...], preferred_element_type=F32)
    b = jnp.dot(f_in, w3_ref[...], preferred_element_type=F32)
    act = (jax.nn.silu(a) * b).astype(BF16)
    f = jnp.dot(act, w2_ref[...], preferred_element_type=F32)
    h = h1 + _rms(f, g3_ref[...])
    o_ref[...] = h

    r = lax.rsqrt(jnp.mean(h * h, axis=-1, keepdims=True) + NORM_EPS)
    hq = (h * r * gq_ref[...]).astype(BF16)
    hk = (h * r * gkv_ref[...]).astype(BF16)
    ang = pos_ref[0] * invf_ref[...]
    cs = jnp.cos(ang)
    sn = jnp.sin(ang)
    s_neg = sn * sneg_ref[...]
    s_pos = sn * spos_ref[...]
    q = jnp.dot(hq, wq_ref[...], preferred_element_type=F32)
    k = jnp.dot(hk, wk_ref[...], preferred_element_type=F32)
    vt = lax.dot_general(wvt_ref[...], hk, (((1,), (1,)), ((), ())), preferred_element_type=F32)
    for hh in range(N_HEADS):
        sl = slice(hh * HEAD_DIM, (hh + 1) * HEAD_DIM)
        q_ref[0, hh] = (_rotary(q[:, sl], cs, s_neg, s_pos) * Q_SCALE_LOG2).astype(BF16)
        kr = _rotary(k[:, sl], cs, s_neg, s_pos)
        for j in range(nblk_tile):
            rs = slice(j * MOBA_BLOCK, (j + 1) * MOBA_BLOCK)
            k_ref[0, hh, j] = kr[rs].astype(BF16)
            vt_ref[0, hh, j] = vt[sl, rs].astype(BF16)
            km = jnp.mean(kr[rs], axis=0, keepdims=True)
            for piece in range(KMEAN_PIECES):
                part = km.astype(BF16).astype(F32)
                km_ref[0, hh, pl.ds(piece * nblk_seq + ti * nblk_tile + j, 1), :] = part
                km = km - part


def _ffn_qkv(h1, B, g2, g3, w1, w3, w2, posb, invf, sneg, spos, gq, gkv, wq, wk, wvt):
    N, D = h1.shape
    S = N // B
    Fd = w1.shape[1]
    T = ROW_TILE
    H, Dh, blk = N_HEADS, HEAD_DIM, MOBA_BLOCK
    nb = S // blk
    nbt = T // blk
    tps = S // T
    kern = functools.partial(_ffn_qkv_kernel, nblk_tile=nbt, nblk_seq=nb, tiles_per_seq=tps)
    return pl.pallas_call(
        kern,
        grid=(N // T,),
        in_specs=[
            pl.BlockSpec((T, D), lambda i: (i, 0)),
            _const_spec((1, D)), _const_spec((1, D)),
            _const_spec((D, Fd)), _const_spec((D, Fd)), _const_spec((Fd, D)),
            pl.BlockSpec((1, T, LANES), lambda i: (i // tps, i % tps, 0)),
            _const_spec((1, LANES)), _const_spec((1, LANES)), _const_spec((1, LANES)),
            _const_spec((1, D)), _const_spec((1, D)),
            _const_spec((D, D)), _const_spec((D, D)), _const_spec((D, D)),
        ],
        out_specs=[
            pl.BlockSpec((T, D), lambda i: (i, 0)),
            pl.BlockSpec((1, H, T, Dh), lambda i: (i // tps, 0, i % tps, 0)),
            pl.BlockSpec((1, H, nbt, blk, Dh), lambda i: (i // tps, 0, i % tps, 0, 0)),
            pl.BlockSpec((1, H, nbt, Dh, blk), lambda i: (i // tps, 0, i % tps, 0, 0)),
            pl.BlockSpec((1, H, KMEAN_PIECES * nb, Dh), lambda i: (i // tps, 0, 0, 0)),
        ],
        out_shape=[
            jax.ShapeDtypeStruct((N, D), F32),
            jax.ShapeDtypeStruct((B, H, S, Dh), BF16),
            jax.ShapeDtypeStruct((B, H, nb, blk, Dh), BF16),
            jax.ShapeDtypeStruct((B, H, nb, Dh, blk), BF16),
            jax.ShapeDtypeStruct((B, H, KMEAN_PIECES * nb, Dh), F32),
        ],
        compiler_params=_cparams("arbitrary"),
        name="ffn_qkv",
    )(h1, g2, g3, w1, w3, w2, posb, invf, sneg, spos, gq, gkv, wq, wk, wvt)


def _moba_kernel(q_ref, k_ref, vt_ref, km_ref, h_ref, g1_ref, wo_ref, g2_ref, wrt_ref, br_ref,
                 o_ref, f_ref, e_ref, wcol_ref, sel_scr, st_scr, acc_scr, o_scr, *, nb, k_sel):
    j = pl.program_id(1)
    blk = MOBA_BLOCK
    nt_dims = (((1,), (1,)), ((), ()))
    blk_id = lax.broadcasted_iota(jnp.int32, (nb, blk), 0)
    past = blk_id < j
    causal = (lax.broadcasted_iota(jnp.int32, (blk, blk), 0)
              <= lax.broadcasted_iota(jnp.int32, (blk, blk), 1))

    for hh in range(N_HEADS):
        parts = lax.dot_general(km_ref[0, hh].astype(BF16), q_ref[0, hh], nt_dims,
                                preferred_element_type=F32)
        gate = parts[0:nb]
        for piece in range(1, KMEAN_PIECES):
            gate = gate + parts[piece * nb:(piece + 1) * nb]
        gate = jnp.where(past, gate, NEG_INF)
        rank = jnp.zeros((nb, blk), F32)
        for m in range(nb):
            gm = gate[m:m + 1, :]
            beats = jnp.where(gm > gate, 1.0, jnp.where(gm == gate, jnp.where(m < blk_id, 1.0, 0.0), 0.0))
            rank = rank + beats
        sel_scr[hh] = jnp.where(past, jnp.where(rank < k_sel, 1.0, 0.0), 0.0)

    def qk_pass(n, ms, own):
        out = []
        for hh in range(N_HEADS):
            st = lax.dot_general(k_ref[0, hh, n], q_ref[0, hh], nt_dims, preferred_element_type=F32)
            allowed = causal if own else sel_scr[hh, pl.ds(n, 1), :] > 0.5
            st = jnp.where(allowed, st, NEG_INF)
            st_scr[hh, n] = st
            out.append(jnp.maximum(ms[hh], jnp.max(st, axis=0, keepdims=True)))
        return tuple(out)

    m_init = tuple(jnp.full((1, blk), NEG_INF, F32) for _ in range(N_HEADS))
    m_own = qk_pass(j, m_init, True)
    def past_blocks(step, carry):
        done = 0
        for group in (4, 2, 1):
            trips = (j - done) // group

            def body(i, c, group=group, done=done):
                for g in range(group):
                    c = step(done + i * group + g, c)
                return c

            carry = lax.fori_loop(0, trips, body, carry)
            done = done + trips * group
        return carry

    m_fin = past_blocks(lambda n, ms: qk_pass(n, ms, False), m_own)

    def pv_pass(n, ls, first):
        out = []
        for hh in range(N_HEADS):
            p = jnp.exp2(st_scr[hh, n] - m_fin[hh])
            out.append(ls[hh] + jnp.sum(p, axis=0, keepdims=True))
            pv = jnp.dot(vt_ref[0, hh, n], p.astype(BF16), preferred_element_type=F32)
            acc_scr[hh] = pv if first else acc_scr[hh] + pv
        return tuple(out)

    l_own = pv_pass(j, tuple(jnp.zeros((1, blk), F32) for _ in range(N_HEADS)), True)
    l_fin = past_blocks(lambda n, ls: pv_pass(n, ls, False), l_own)
    for hh in range(N_HEADS):
        o_t = acc_scr[hh] / l_fin[hh]
        o_scr[:, hh * HEAD_DIM:(hh + 1) * HEAD_DIM] = o_t.T.astype(BF16)

    mix = jnp.dot(o_scr[...], wo_ref[...], preferred_element_type=F32)
    h3 = h_ref[0] + _rms(mix, g1_ref[...])
    o_ref[0] = h3
    f = _rms(h3, g2_ref[...])
    f_ref[0] = f.astype(BF16)
    e_ref[0], wcol_ref[0] = _route(f, wrt_ref, br_ref)


def _moba_layer(q, k5, vt5, kmean, h2, g1, wo, g2, wrt, br):
    B, S, D = h2.shape
    H, Dh, blk = N_HEADS, HEAD_DIM, MOBA_BLOCK
    nb = S // blk
    E = wrt.shape[0]
    kern = functools.partial(_moba_kernel, nb=nb, k_sel=min(MOBA_TOPK, nb))
    return pl.pallas_call(
        kern,
        grid=(B, nb),
        in_specs=[
            pl.BlockSpec((1, H, blk, Dh), lambda b, j: (b, 0, j, 0)),
            pl.BlockSpec((1, H, nb, blk, Dh), lambda b, j: (b, 0, 0, 0, 0)),
            pl.BlockSpec((1, H, nb, Dh, blk), lambda b, j: (b, 0, 0, 0, 0)),
            pl.BlockSpec((1, H, KMEAN_PIECES * nb, Dh), lambda b, j: (b, 0, 0, 0)),
            pl.BlockSpec((1, blk, D), lambda b, j: (b, j, 0)),
            _const_spec((1, D)), _const_spec((D, D)),
            _const_spec((1, D)), _const_spec((E, D)), _const_spec((E, 1)),
        ],
        out_specs=[
            pl.BlockSpec((1, blk, D), lambda b, j: (b, j, 0)),
            pl.BlockSpec((1, blk, D), lambda b, j: (b, j, 0)),
            pl.BlockSpec((1, 2, blk), lambda b, j: (b * nb + j, 0, 0)),
            pl.BlockSpec((1, blk, LANES), lambda b, j: (b * nb + j, 0, 0)),
        ],
        out_shape=[
            jax.ShapeDtypeStruct((B, S, D), F32),
            jax.ShapeDtypeStruct((B, S, D), BF16),
            jax.ShapeDtypeStruct((B * nb, 2, blk), jnp.int32),
            jax.ShapeDtypeStruct((B * nb, blk, LANES), F32),
        ],
        scratch_shapes=[
            pltpu.VMEM((H, nb, blk), F32), pltpu.VMEM((H, nb, blk, blk), F32),
            pltpu.VMEM((H, Dh, blk), F32), pltpu.VMEM((blk, D), BF16),
        ],
        compiler_params=_cparams("parallel", "arbitrary"),
        name="moba_attn",
    )(q, k5, vt5, kmean, h2, g1, wo, g2, wrt, br)


def _route(f, wrt_ref, br_ref):
    nt_dims = (((1,), (1,)), ((), ()))
    w = wrt_ref[...]
    ne = w.shape[0]
    w_hi = w.astype(BF16).astype(F32)
    f_hi = f.astype(BF16)
    f_lo = (f - f_hi.astype(F32)).astype(BF16)
    w2 = jnp.concatenate([w_hi, w - w_hi], axis=0).astype(BF16)
    a = lax.dot_general(w2, f_hi, nt_dims, preferred_element_type=F32)
    b = lax.dot_general(w2[:ne], f_lo, nt_dims, preferred_element_type=F32)
    logit = a[:ne] + a[ne:] + b + br_ref[...]
    t = logit.shape[1]
    eidx = lax.broadcasted_iota(jnp.int32, (ne, t), 0)
    m1 = jnp.max(logit, axis=0, keepdims=True)
    e1 = jnp.min(jnp.where(logit == m1, eidx, ne), axis=0, keepdims=True)
    rest = jnp.where(eidx == e1, -jnp.inf, logit)
    m2 = jnp.max(rest, axis=0, keepdims=True)
    e2 = jnp.min(jnp.where(rest == m2, eidx, ne), axis=0, keepdims=True)
    ex = jnp.exp(m2 - m1)
    den = 1.0 + ex
    wpad = jnp.concatenate([1.0 / den, ex / den, jnp.zeros((LANES - 2, t), F32)], axis=0)
    return jnp.concatenate([e1, e2], axis=0), wpad.T


META_ROWS = 32


def _rank_kernel(e_ref, lpos_ref, lcol_ref, meta_ref, pad_ref, blke_ref, nused_ref, blkv_ref,
                 *, ne, rows_log2, nblk_pad):
    ntile, _, t = e_ref.shape
    eidx = lax.broadcasted_iota(jnp.int32, (ne, t), 0)

    def onehots(i):
        e = e_ref[i]
        oh0 = jnp.where(eidx == e[0:1, :], 1.0, 0.0)
        oh1 = jnp.where(eidx == e[1:2, :], 1.0, 0.0)
        return oh0, oh1

    def excl_cumsum(col):
        run = jnp.zeros((1, 1), jnp.int32)
        parts = []
        for e in range(ne):
            parts.append(run)
            run = run + col[e:e + 1, :]
        return jnp.concatenate(parts, axis=0), run

    def count(i, cnt):
        oh0, oh1 = onehots(i)
        return cnt + jnp.sum(oh0 + oh1, axis=1, keepdims=True).astype(jnp.int32)

    cnt = lax.fori_loop(0, ntile, count, jnp.zeros((ne, 1), jnp.int32), unroll=8)
    pcnt = ((cnt + ((1 << rows_log2) - 1)) >> rows_log2) << rows_log2
    pstart, ptotal = excl_cumsum(pcnt)
    pend = pstart + pcnt

    upper = jnp.where(lax.broadcasted_iota(jnp.int32, (t, t), 0) < lax.broadcasted_iota(jnp.int32, (t, t), 1),
                      1.0, 0.0).astype(BF16)
    lane_b = lambda col: jnp.broadcast_to(col, (ne, LANES))

    def place(i, base):
        oh0, oh1 = onehots(i)
        both = oh0 + oh1
        before = jnp.dot(both.astype(BF16), upper, preferred_element_type=F32).astype(jnp.int32)
        cnt_i = jnp.sum(both, axis=1, keepdims=True).astype(jnp.int32)
        loff, _ = excl_cumsum(cnt_i)
        slot = loff + before
        l0 = jnp.sum(jnp.where(oh0 > 0.5, slot, 0), axis=0, keepdims=True)
        l1 = jnp.sum(jnp.where(oh1 > 0.5, slot, 0), axis=0, keepdims=True)
        lpos_ref[i] = jnp.concatenate([l0, l1], axis=0)
        lpad = jnp.concatenate([l0.astype(F32), l1.astype(F32), jnp.zeros((LANES - 2, t), F32)], axis=0)
        lcol_ref[i] = lpad.T
        meta_ref[i] = jnp.concatenate(
            [lane_b(cnt_i), lane_b(loff), lane_b(pstart + base),
             jnp.zeros((META_ROWS - 3 * ne, LANES), jnp.int32)], axis=0)
        return base + cnt_i

    lax.fori_loop(0, ntile, place, jnp.zeros((ne, 1), jnp.int32), unroll=4)

    pad_ref[...] = jnp.concatenate([lane_b(pstart + cnt), lane_b(pcnt - cnt),
                                    jnp.broadcast_to(ptotal, (ne, LANES))], axis=0)
    blk_row = lax.broadcasted_iota(jnp.int32, (ne, nblk_pad), 1) << rows_log2
    blke = jnp.sum(jnp.where(pend <= blk_row, 1, 0), axis=0, keepdims=True)
    blke_ref[...] = jnp.minimum(blke, ne - 1)
    nused_ref[...] = jnp.broadcast_to(ptotal >> rows_log2, (1, LANES))
    held = jnp.minimum(pstart + cnt - blk_row, 1 << rows_log2)
    blkv_ref[...] = jnp.sum(jnp.where(blk_row >= pstart, jnp.where(blk_row < pend, held, 0), 0),
                            axis=0, keepdims=True)


def _rank(e3, ne, nblk_pad):
    ntile, _, t = e3.shape
    kern = functools.partial(_rank_kernel, ne=ne, rows_log2=int(math.log2(MOE_ROWS)), nblk_pad=nblk_pad)
    return pl.pallas_call(
        kern,
        out_shape=[
            jax.ShapeDtypeStruct((ntile, 2, t), jnp.int32),
            jax.ShapeDtypeStruct((ntile, t, LANES), F32),
            jax.ShapeDtypeStruct((ntile, META_ROWS, LANES), jnp.int32),
            jax.ShapeDtypeStruct((3 * ne, LANES), jnp.int32),
            jax.ShapeDtypeStruct((1, nblk_pad), jnp.int32),
            jax.ShapeDtypeStruct((1, LANES), jnp.int32),
            jax.ShapeDtypeStruct((1, nblk_pad), jnp.int32),
        ],
        compiler_params=pltpu.CompilerParams(vmem_limit_bytes=VMEM_LIMIT),
        name="moe_rank",
    )(e3)


SUBROWS = 8


def _rows_to_interleaved(ref, x):
    n = x.shape[0]
    for s in range(SUBROWS):
        ref[pl.ds(s, n, stride=SUBROWS), :] = x[:, s * LANES:(s + 1) * LANES]


def _interleaved_to_rows(ref, n):
    return jnp.concatenate([ref[pl.ds(s, n, stride=SUBROWS), :] for s in range(SUBROWS)], axis=1)


def _segment_copies(count, src_ref, src_row, dst_ref, dst_row, sem, max_piece, wait=False, advance_src=True):
    p = max_piece
    while p >= 1:
        hit = (count & p) != 0

        @pl.when(hit)
        def _(p=p, src_row=src_row, dst_row=dst_row):
            cp = pltpu.make_async_copy(
                src_ref.at[pl.ds(pl.multiple_of(src_row * SUBROWS, SUBROWS), p * SUBROWS), :],
                dst_ref.at[pl.ds(pl.multiple_of(dst_row * SUBROWS, SUBROWS), p * SUBROWS), :], sem)
            if wait:
                cp.wait()
            else:
                cp.start()

        step = count & p
        if advance_src:
            src_row = src_row + step
        dst_row = dst_row + step
        p //= 2


def _dispatch_kernel(meta_ref, pad_ref, lpos_ref, f_ref, xs_ref, srt, zbuf, sems, zsem, *, td, ne):
    i = pl.program_id(0)

    @pl.when(i == 0)
    def _():
        zbuf[...] = jnp.zeros_like(zbuf)
        zrows = zbuf.shape[0] // SUBROWS
        total = xs_ref.shape[0] // SUBROWS
        for wait in (False, True):
            for e in range(ne):
                _segment_copies(pad_ref[ne + e], zbuf, 0, xs_ref, pad_ref[e], zsem,
                                zrows, wait=wait, advance_src=False)
            for r in range(0, ne * MOE_ROWS, zrows):
                row = pad_ref[2 * ne] + r

                @pl.when(row < total)
                def _(row=row):
                    cp = pltpu.make_async_copy(
                        zbuf, xs_ref.at[pl.ds(pl.multiple_of(row * SUBROWS, SUBROWS), zrows * SUBROWS), :], zsem)
                    if wait:
                        cp.wait()
                    else:
                        cp.start()

    slot = i % 2
    lp = lpos_ref[0]
    r = lax.broadcasted_iota(jnp.int32, (2 * td, td), 0)
    perm = (jnp.where(r == lp[0:1, :], 1.0, 0.0) + jnp.where(r == lp[1:2, :], 1.0, 0.0)).astype(BF16)
    _rows_to_interleaved(srt.at[slot], jnp.dot(perm, f_ref[...], preferred_element_type=F32))
    for e in range(ne):
        _segment_copies(meta_ref[0, 0, e], srt.at[slot], meta_ref[0, 0, ne + e], xs_ref,
                        meta_ref[0, 0, 2 * ne + e], sems.at[slot], td)

    def drain(s):
        for h in range(2):
            pltpu.make_async_copy(srt.at[s, pl.ds(h * td * SUBROWS, td * SUBROWS), :],
                                  xs_ref.at[pl.ds(0, td * SUBROWS), :], sems.at[s]).wait()

    @pl.when(i > 0)
    def _():
        drain(1 - slot)

    @pl.when(i + 1 == pl.num_programs(0))
    def _():
        drain(slot)


def _dispatch(meta, pad, lpos, f, total):
    N, D = f.shape
    ntile, _, td = lpos.shape
    ne = N_EXPERTS
    assert D == SUBROWS * LANES
    kern = functools.partial(_dispatch_kernel, td=td, ne=ne)
    return pl.pallas_call(
        kern,
        grid=(ntile,),
        in_specs=[
            pl.BlockSpec((1, 1, META_ROWS), lambda i: (i, 0, 0), memory_space=pltpu.SMEM),
            pl.BlockSpec(memory_space=pltpu.SMEM),
            pl.BlockSpec((1, 2, td), lambda i: (i, 0, 0)),
            pl.BlockSpec((td, D), lambda i: (i, 0)),
        ],
        out_specs=pl.BlockSpec(memory_space=pl.ANY),
        out_shape=jax.ShapeDtypeStruct((total * SUBROWS, LANES), F32),
        scratch_shapes=[pltpu.VMEM((2, 2 * td * SUBROWS, LANES), F32),
                        pltpu.VMEM((MOE_ROWS // 2 * SUBROWS, LANES), F32),
                        pltpu.SemaphoreType.DMA((2,)), pltpu.SemaphoreType.DMA(())],
        compiler_params=_cparams("arbitrary"),
        name="moe_dispatch",
    )(meta, pad, lpos, f)


def _load_expert_weights(e, w1_hbm, w3_hbm, w2_hbm, w1b, w3b, w2b, st_in, st_out, sems):
    d, fe = w1b.shape
    ch_in, ch_out = st_in.shape[1], st_out.shape[1]
    items = ([(w1_hbm, w1b, c, True) for c in range(d // ch_in)] + [(w3_hbm, w3b, c, True) for c in range(d // ch_in)]
             + [(w2_hbm, w2b, c, False) for c in range(fe // ch_out)])

    def copy(k):
        src, _, c, wide = items[k]
        slot = k % 2
        if wide:
            return pltpu.make_async_copy(src.at[e, pl.ds(c * ch_in, ch_in), :], st_in.at[slot], sems.at[slot])
        return pltpu.make_async_copy(src.at[e, pl.ds(c * ch_out, ch_out), :], st_out.at[slot], sems.at[slot])

    copy(0).start()
    for k in range(len(items)):
        if k + 1 < len(items):
            copy(k + 1).start()
        copy(k).wait()
        _, dst, c, wide = items[k]
        if wide:
            dst[c * ch_in:(c + 1) * ch_in, :] = st_in[k % 2].astype(BF16)
        else:
            dst[c * ch_out:(c + 1) * ch_out, :] = st_out[k % 2].astype(BF16)


def _expert_kernel(blke_ref, nused_ref, blkv_ref, xs_ref, w1_hbm, w3_hbm, w2_hbm, y_ref,
                   w1b, w3b, w2b, st_in, st_out, sems, *, rb):
    i = pl.program_id(0)
    held = blkv_ref[i]
    half = rb // 2
    e = blke_ref[i]
    e_prev = blke_ref[jnp.maximum(i - 1, 0)]

    @pl.when(jnp.logical_and(held > 0, jnp.logical_or(i == 0, e != e_prev)))
    def _():
        _load_expert_weights(e, w1_hbm, w3_hbm, w2_hbm, w1b, w3b, w2b, st_in, st_out, sems)

    def swiglu(rows):
        x = _interleaved_to_rows(xs_ref, rows).astype(BF16)
        fe = w1b.shape[1]
        y = None
        for fs in range(0, fe, fe // 2):
            sl = slice(fs, fs + fe // 2)
            a = jnp.dot(x, w1b[:, sl], preferred_element_type=F32)
            b = jnp.dot(x, w3b[:, sl], preferred_element_type=F32)
            act = (jax.nn.silu(a) * b).astype(BF16)
            part = jnp.dot(act, w2b[sl, :], preferred_element_type=F32)
            y = part if y is None else y + part
        _rows_to_interleaved(y_ref, y)

    @pl.when(held > half)
    def _():
        swiglu(rb)

    @pl.when(jnp.logical_and(held > 0, held <= half))
    def _():
        swiglu(half)
        y_ref[pl.ds(half * SUBROWS, half * SUBROWS), :] = jnp.zeros((half * SUBROWS, LANES), F32)

    @pl.when(held == 0)
    def _():
        y_ref[...] = jnp.zeros_like(y_ref)


def _experts(blke, nused, blkv, xs, w1, w3, w2):
    E, D, Fe = w1.shape
    total = xs.shape[0] // SUBROWS
    rb = MOE_ROWS
    nblk = total // rb
    ch = EXPERT_W_CHUNK
    assert Fe % (2 * LANES) == 0 and Fe % ch == 0
    hbm = pl.BlockSpec(memory_space=pl.ANY)
    return pl.pallas_call(
        functools.partial(_expert_kernel, rb=rb),
        grid_spec=pltpu.PrefetchScalarGridSpec(
            num_scalar_prefetch=3,
            grid=(nblk,),
            in_specs=[
                pl.BlockSpec((rb * SUBROWS, LANES), lambda i, be, nu, bv: (jnp.minimum(i, nu[0] - 1), 0)),
                hbm, hbm, hbm,
            ],
            out_specs=pl.BlockSpec((rb * SUBROWS, LANES), lambda i, be, nu, bv: (i, 0)),
            scratch_shapes=[
                pltpu.VMEM((D, Fe), BF16), pltpu.VMEM((D, Fe), BF16), pltpu.VMEM((Fe, D), BF16),
                pltpu.VMEM((2, ch // 2, Fe), F32), pltpu.VMEM((2, ch, D), F32), pltpu.SemaphoreType.DMA((2,)),
            ],
        ),
        out_shape=jax.ShapeDtypeStruct((total * SUBROWS, LANES), F32),
        compiler_params=_cparams("arbitrary"),
        name="moe_experts",
    )(blke, nused, blkv, xs, w1, w3, w2)


def _combine_kernel(meta_ref, nxt_ref, yb_ref, lcol_ref, wcol_ref, h_ref, g3_ref, o_ref, buf, sems, *, td, ne):
    i = pl.program_id(0)
    slot = i % 2

    def fetch(m_ref, s):
        for e in range(ne):
            _segment_copies(m_ref[0, 0, e], yb_ref, m_ref[0, 0, 2 * ne + e], buf.at[s], m_ref[0, 0, ne + e],
                            sems.at[s], td)

    @pl.when(i == 0)
    def _():
        fetch(meta_ref, 0)

    @pl.when(i + 1 < pl.num_programs(0))
    def _():
        fetch(nxt_ref, 1 - slot)

    for h in range(2):
        pltpu.make_async_copy(yb_ref.at[pl.ds(0, td * SUBROWS), :],
                              buf.at[slot, pl.ds(h * td * SUBROWS, td * SUBROWS), :], sems.at[slot]).wait()
    yb = _interleaved_to_rows(buf.at[slot], 2 * td).astype(BF16)
    lc = lcol_ref[0]
    w = wcol_ref[...]
    c = lax.broadcasted_iota(jnp.int32, (td, 2 * td), 1).astype(F32)
    pick0 = jnp.where(c == lc[:, 0:1], 1.0, 0.0).astype(BF16)
    pick1 = jnp.where(c == lc[:, 1:2], 1.0, 0.0).astype(BF16)
    f = (w[:, 0:1] * jnp.dot(pick0, yb, preferred_element_type=F32)
         + w[:, 1:2] * jnp.dot(pick1, yb, preferred_element_type=F32))
    o_ref[...] = h_ref[...] + _rms(f, g3_ref[...])


def _combine(meta, yb, lcol, wcol, h3, g3):
    N, D = h3.shape
    ntile, td, _ = lcol.shape
    ne = N_EXPERTS
    kern = functools.partial(_combine_kernel, td=td, ne=ne)
    return pl.pallas_call(
        kern,
        grid=(ntile,),
        in_specs=[
            pl.BlockSpec((1, 1, META_ROWS), lambda i: (i, 0, 0), memory_space=pltpu.SMEM),
            pl.BlockSpec((1, 1, META_ROWS), lambda i: (jnp.minimum(i + 1, ntile - 1), 0, 0),
                         memory_space=pltpu.SMEM),
            pl.BlockSpec(memory_space=pl.ANY),
            pl.BlockSpec((1, td, LANES), lambda i: (i, 0, 0)),
            pl.BlockSpec((td, LANES), lambda i: (i, 0)),
            pl.BlockSpec((td, D), lambda i: (i, 0)),
            _const_spec((1, D)),
        ],
        out_specs=pl.BlockSpec((td, D), lambda i: (i, 0)),
        out_shape=jax.ShapeDtypeStruct((N, D), F32),
        scratch_shapes=[pltpu.VMEM((2, 2 * td * SUBROWS, LANES), F32), pltpu.SemaphoreType.DMA((2,))],
        compiler_params=_cparams("arbitrary"),
        name="moe_combine",
    )(meta, meta, yb, lcol, wcol, h3, g3)


def _rope_lane_tables():
    half = ROT_DIM // 2
    inv_freq = 1.0 / (ROPE_THETA ** (np.arange(half, dtype=np.float32) / half))
    invf = np.zeros((1, LANES), np.float32)
    invf[0, :half] = inv_freq
    invf[0, half:ROT_DIM] = inv_freq
    sneg = np.zeros((1, LANES), np.float32)
    sneg[0, :half] = -1.0
    spos = np.zeros((1, LANES), np.float32)
    spos[0, half:ROT_DIM] = 1.0
    return jnp.asarray(invf), jnp.asarray(sneg), jnp.asarray(spos)


def kernel(x, positions, norms, s5_lam_re, s5_lam_im, s5_log_dt, s5_b_re, s5_b_im, s5_c_re, s5_c_im, s5_d, s5_w_glu, kv_norm, w_kv, w_q, w_o, ffn_w1, ffn_w3, ffn_w2, moe_router, moe_bias, moe_w1, moe_w3, moe_w2):
    B, S, D = x.shape
    N = B * S
    HD = N_HEADS * HEAD_DIM
    E = moe_router.shape[-1]
    row = lambda v: v.astype(F32).reshape(1, -1)

    a2_r, a2_i, b2, c2, d2 = _s5_params(s5_lam_re[0], s5_lam_im[0], s5_log_dt[0], s5_b_re[0], s5_b_im[0],
                                        s5_c_re[0], s5_c_im[0])
    h1 = _s5_layer(x, row(norms[0, 0]), row(norms[0, 1]), row(s5_d[0]), a2_r, a2_i, b2, c2, d2,
                   s5_w_glu[0].astype(BF16))
    invf, sneg, spos = _rope_lane_tables()
    posb = jnp.broadcast_to(positions.astype(F32)[:, :, None], (B, S, LANES))
    h2, q, k5, vt5, kmean = _ffn_qkv(
        h1.reshape(N, D), B, row(norms[0, 2]), row(norms[0, 3]),
        ffn_w1[0].astype(BF16), ffn_w3[0].astype(BF16), ffn_w2[0].astype(BF16),
        posb, invf, sneg, spos, row(norms[1, 0]), row(kv_norm),
        w_q[0].astype(BF16), w_kv[:, :HD].astype(BF16), w_kv[:, HD:].T.astype(BF16))
    h3, f_in, e_tiles, wcol = _moba_layer(q, k5, vt5, kmean, h2.reshape(B, S, D), row(norms[1, 1]), w_o[0].astype(BF16),
                                          row(norms[1, 2]), moe_router[0].T.astype(F32),
                                          moe_bias[0].astype(F32).reshape(E, 1))
    h3 = h3.reshape(N, D)

    assert MOBA_BLOCK == DISPATCH_TILE
    total = 2 * N + E * MOE_ROWS
    nblk = total // MOE_ROWS
    nblk_pad = -(-nblk // LANES) * LANES
    lpos, lcol, meta, pad, blke, nused, blkv = _rank(e_tiles, E, nblk_pad)
    meta = meta[:, :, 0].reshape(-1, 1, META_ROWS)
    xs = _dispatch(meta, pad[:, 0], lpos, f_in.reshape(N, D), total)
    yb = _experts(blke[0, :nblk], nused[0, :1], blkv[0, :nblk], xs,
                  moe_w1[0].astype(F32), moe_w3[0].astype(F32), moe_w2[0].astype(F32))
    out = _combine(meta, yb, lcol, wcol.reshape(N, LANES), h3, row(norms[1, 3]))
    return out.reshape(B, S, D)
```
